```python
import math
import jax, jax.numpy as jnp
from jax import lax
import numpy as np

D_MODEL = 1024
BATCH = 2
SEQ = 8192
DEPTH = 2
DEC_BATCH = 32
DEC_SEQ = 4
PAST_LEN = 16384
PAGE_SIZE = 128

HEAD_DIM = 64
NSA_HEADS = 8
NSA_KV = 2
NSA_REP = NSA_HEADS // NSA_KV
CMP_LEN = 32
CMP_STRIDE = 16
CMP_HIDDEN = 2 * HEAD_DIM
SEL_BLOCK = 64
SEL_TOPK = 16
WINDOW = 512
FOX_HEADS = 8
RET_HEADS = 8
RET_CHUNK = 128
BRANCH_WIDTH = 8 * HEAD_DIM
N_BRANCH = 3
Q_BLOCK = 128
EPS = 1e-6
NEG_INF = -1e30
FORCE = 1e9
IN_SPLITS = (
    NSA_HEADS * HEAD_DIM,
    3 * 2 * NSA_KV * HEAD_DIM,
    3 * NSA_HEADS,
    BRANCH_WIDTH,
    FOX_HEADS * HEAD_DIM,
    FOX_HEADS * HEAD_DIM,
    FOX_HEADS * HEAD_DIM,
    FOX_HEADS,
    BRANCH_WIDTH,
    RET_HEADS * HEAD_DIM,
    RET_HEADS * HEAD_DIM,
    RET_HEADS * HEAD_DIM,
    BRANCH_WIDTH,
    N_BRANCH * D_MODEL,
)
D_IN = sum(IN_SPLITS)

kernel_name = 'hybrid_nsa_fox_retention_decode_step'


def rmsnorm(x, g):
    xf = x.astype(jnp.float32)
    y = xf * lax.rsqrt(jnp.mean(xf * xf, axis=-1, keepdims=True) + EPS)
    return (y * g.astype(jnp.float32)).astype(x.dtype)


def head_norm(o):
    mu = jnp.mean(o, axis=-1, keepdims=True)
    var = jnp.mean(jnp.square(o - mu), axis=-1, keepdims=True)
    return (o - mu) * lax.rsqrt(var + EPS)


def masked_softmax(s, mask):
    s = jnp.where(mask, s, NEG_INF)
    m = jnp.max(s, axis=-1, keepdims=True)
    p = jnp.where(mask, jnp.exp(s - m), 0.0)
    return p / jnp.maximum(jnp.sum(p, axis=-1, keepdims=True), 1e-30)


def alibi_slopes():
    h = jnp.arange(1, NSA_HEADS + 1, dtype=jnp.float32)
    return jnp.exp2(-8.0 * h / NSA_HEADS).reshape(NSA_KV, NSA_REP)


def nsa_compress(k, w1, w2, pos):
    B, L = k.shape[:2]
    n_seg = L // CMP_STRIDE
    seg = k[:, :n_seg * CMP_STRIDE].reshape(B, n_seg, CMP_STRIDE, NSA_KV, HEAD_DIM)
    w1 = w1.reshape(2, CMP_STRIDE, HEAD_DIM, CMP_HIDDEN)
    pos = pos.reshape(2, CMP_STRIDE, 1, HEAD_DIM)
    lo = jnp.einsum('bnsgd,sdh->bngh', seg + pos[0], w1[0])
    hi = jnp.einsum('bnsgd,sdh->bngh', seg + pos[1], w1[1])
    hid = jax.nn.silu(lo[:, :-1] + hi[:, 1:])
    return jnp.einsum('bngh,hd->bngd', hid, w2)


def nsa_block(q, t, kc, vc, ksb, vsb, kw, vw, sw, gates):
    f32 = jnp.float32
    B, T = q.shape[:2]
    scale = HEAD_DIM ** -0.5
    slopes = alibi_slopes()[None, :, :, None, None]
    n_cmp = kc.shape[1]
    ends = jnp.arange(n_cmp) * CMP_STRIDE + CMP_LEN - 1
    dist = t[:, None] - ends[None, :]
    s = jnp.einsum('btgrd,bngd->bgrtn', q, kc, preferred_element_type=f32) * scale
    s = s - slopes * dist.astype(f32)
    p_cmp = masked_softmax(s, dist >= 0)
    o_cmp = jnp.einsum('bgrtn,bngd->btgrd', p_cmp.astype(vc.dtype), vc)
    n_sel = ksb.shape[2]
    blk = jnp.arange(n_sel)
    cover = ((ends[:, None] - CMP_LEN + 1 < (blk[None, :] + 1) * SEL_BLOCK)
             & (ends[:, None] >= blk[None, :] * SEL_BLOCK)).astype(f32)
    imp = jnp.einsum('bgrtn,nj->bgtj', p_cmp, cover)
    cur = (t // SEL_BLOCK)[:, None]
    forced = (blk[None, :] == 0) | (blk[None, :] == cur) | (blk[None, :] == cur - 1)
    imp = jnp.where(forced, FORCE, imp)
    imp = jnp.where(blk[None, :] * SEL_BLOCK <= t[:, None], imp, NEG_INF)
    _, idx = lax.top_k(imp, min(SEL_TOPK, n_sel))
    n_top = idx.shape[-1]
    take = jax.vmap(jax.vmap(lambda blocks, ix: blocks[ix]))
    kg = take(ksb, idx)
    vg = take(vsb, idx)
    spos = idx[..., None] * SEL_BLOCK + jnp.arange(SEL_BLOCK)
    dist = (t[None, None, :, None, None] - spos).reshape(B, NSA_KV, 1, T, n_top * SEL_BLOCK)
    s = jnp.einsum('btgrd,bgtkpd->bgrtkp', q, kg, preferred_element_type=f32)
    s = s.reshape(B, NSA_KV, NSA_REP, T, n_top * SEL_BLOCK) * scale - slopes * dist.astype(f32)
    p = masked_softmax(s, dist >= 0).reshape(B, NSA_KV, NSA_REP, T, n_top, SEL_BLOCK)
    o_sel = jnp.einsum('bgrtkp,bgtkpd->btgrd', p.astype(vg.dtype), vg)
    dist = t[:, None] - sw[None, :]
    s = jnp.einsum('btgrd,bwgd->bgrtw', q, kw, preferred_element_type=f32) * scale
    s = s - slopes * dist.astype(f32)
    mask = (dist >= 0) & (dist < WINDOW) & (sw[None, :] >= 0)
    o_win = jnp.einsum('bgrtw,bwgd->btgrd', masked_softmax(s, mask).astype(vw.dtype), vw)
    g = jax.nn.sigmoid(gates.astype(f32)).astype(q.dtype)
    return g[..., 0:1] * o_cmp + g[..., 1:2] * o_sel + g[..., 2:3] * o_win


def nsa_prompt(q, cmp_rows, sel_rows, win_rows, gates, cw):
    B, S = q.shape[:2]
    kc = nsa_compress(cmp_rows[:, :, 0], cw[0], cw[1], cw[2])
    vc = nsa_compress(cmp_rows[:, :, 1], cw[3], cw[4], cw[5])
    sel = sel_rows.reshape(B, S // SEL_BLOCK, SEL_BLOCK, 2, NSA_KV, HEAD_DIM).transpose(3, 0, 4, 1, 2, 5)
    win = jnp.pad(win_rows, ((0, 0), (WINDOW, 0), (0, 0), (0, 0), (0, 0)))

    def one_block(b):
        t0 = b * Q_BLOCK
        qb = lax.dynamic_slice_in_dim(q, t0, Q_BLOCK, axis=1)
        gb = lax.dynamic_slice_in_dim(gates, t0, Q_BLOCK, axis=1)
        wb = lax.dynamic_slice_in_dim(win, t0, WINDOW + Q_BLOCK, axis=1)
        t = t0 + jnp.arange(Q_BLOCK)
        sw = t0 - WINDOW + jnp.arange(WINDOW + Q_BLOCK)
        return nsa_block(qb, t, kc, vc, sel[0], sel[1], wb[:, :, 0], wb[:, :, 1], sw, gb)

    o = lax.map(one_block, jnp.arange(S // Q_BLOCK))
    return jnp.swapaxes(o, 0, 1).reshape(q.shape)


def nsa_sample(q, cmp_rows, sel_rows, win_rows, gates, cw, cmp_past, sel_past, win_buf):
    B, T = q.shape[:2]
    P = cmp_past.shape[1]
    WB = win_buf.shape[1]
    L = P + T
    cmp_all = jnp.concatenate([cmp_past, cmp_rows], axis=1)
    kc = nsa_compress(cmp_all[:, :, 0], cw[0], cw[1], cw[2])
    vc = nsa_compress(cmp_all[:, :, 1], cw[3], cw[4], cw[5])
    n_sel = -(-L // SEL_BLOCK)
    sel_all = jnp.concatenate([sel_past, sel_rows], axis=1)
    sel_all = jnp.pad(sel_all, ((0, 0), (0, n_sel * SEL_BLOCK - L), (0, 0), (0, 0), (0, 0)))
    sel = sel_all.reshape(B, n_sel, SEL_BLOCK, 2, NSA_KV, HEAD_DIM).transpose(3, 0, 4, 1, 2, 5)
    win_all = jnp.concatenate([win_buf, win_rows], axis=1)
    t = P + jnp.arange(T)
    sw = P - WB + jnp.arange(WB + T)
    o = nsa_block(q, t, kc, vc, sel[0], sel[1], win_all[:, :, 0], win_all[:, :, 1], sw, gates)
    return o, win_all[:, T:]


def fox_prompt(q, k, v, logf):
    B, S = q.shape[:2]
    scale = HEAD_DIM ** -0.5
    c = jnp.cumsum(logf, axis=1)
    c_keys = jnp.swapaxes(c, 1, 2)[:, :, None, :]
    spos = jnp.arange(S)

    def one_block(b):
        t0 = b * Q_BLOCK
        qb = lax.dynamic_slice_in_dim(q, t0, Q_BLOCK, axis=1)
        cb = lax.dynamic_slice_in_dim(c, t0, Q_BLOCK, axis=1)
        t = t0 + jnp.arange(Q_BLOCK)
        s = jnp.einsum('bthd,bshd->bhts', qb, k, preferred_element_type=jnp.float32) * scale
        s = s + (jnp.swapaxes(cb, 1, 2)[..., :, None] - c_keys)
        p = masked_softmax(s, spos[None, :] <= t[:, None])
        return jnp.einsum('bhts,bshd->bthd', p.astype(v.dtype), v)

    o = lax.map(one_block, jnp.arange(S // Q_BLOCK))
    return jnp.swapaxes(o, 0, 1).reshape(q.shape)


def fox_sample(q, k, v, logf, k_past, v_past, logf_past):
    T = q.shape[1]
    P = k_past.shape[1]
    scale = HEAD_DIM ** -0.5
    c_past = jnp.cumsum(logf_past.astype(jnp.float32), axis=1)
    c_new = c_past[:, -1:] + jnp.cumsum(logf, axis=1)
    cq = jnp.swapaxes(c_new, 1, 2)[..., :, None]
    s_past = jnp.einsum('bthd,bshd->bhts', q, k_past, preferred_element_type=jnp.float32) * scale
    s_past = s_past + (cq - jnp.swapaxes(c_past, 1, 2)[..., None, :])
    s_new = jnp.einsum('bthd,bshd->bhts', q, k, preferred_element_type=jnp.float32) * scale
    s_new = s_new + (cq - jnp.swapaxes(c_new, 1, 2)[..., None, :])
    tri = jnp.arange(T)[None, :] <= jnp.arange(T)[:, None]
    mask = jnp.concatenate([jnp.ones((T, P), dtype=bool), tri], axis=1)
    p = masked_softmax(jnp.concatenate([s_past, s_new], axis=-1), mask)
    return (jnp.einsum('bhts,bshd->bthd', p[..., :P].astype(v.dtype), v_past)
            + jnp.einsum('bhts,bshd->bthd', p[..., P:].astype(v.dtype), v))


def retention_chunk(state, q, k, v, log_gamma):
    f32 = jnp.float32
    C = q.shape[1]
    i = jnp.arange(C, dtype=f32)
    diff = i[:, None] - i[None, :]
    decay = jnp.where(diff >= 0, jnp.exp(jnp.maximum(diff, 0.0)[None] * log_gamma[:, None, None]), 0.0)
    qf, kf, vf = q.astype(f32), k.astype(f32), v.astype(f32)
    scores = jnp.einsum('bihd,bjhd->bhij', qf, kf) * decay
    inner = jnp.einsum('bhij,bjhe->bihe', scores, vf)
    q_dec = jnp.exp((i[:, None] + 1.0) * log_gamma[None, :])
    cross = jnp.einsum('bihd,bhde->bihe', qf * q_dec[None, :, :, None], state)
    k_dec = jnp.exp((C - 1.0 - i)[:, None] * log_gamma[None, :])
    new_state = (jnp.exp(C * log_gamma)[None, :, None, None] * state
                 + jnp.einsum('bjhd,bjhe->bhde', kf * k_dec[None, :, :, None], vf))
    return inner + cross, new_state


def retention_prompt(q, k, v, log_gamma):
    B, S, H, d = q.shape
    n = S // RET_CHUNK

    def to_chunks(a):
        return jnp.swapaxes(a.reshape(B, n, RET_CHUNK, H, d), 0, 1)

    def step(state, xs):
        o, state = retention_chunk(state, xs[0], xs[1], xs[2], log_gamma)
        return state, o

    s0 = jnp.zeros((B, H, d, d), jnp.float32)
    s_fin, o = lax.scan(step, s0, (to_chunks(q), to_chunks(k), to_chunks(v)))
    return jnp.swapaxes(o, 0, 1).reshape(B, S, H, d), s_fin


def decoder_layer(x, c, norm_g, w_ada, b_ada, w_in, b_forget, cw, w_branch, w_out, wb_len, past):
    f32 = jnp.float32
    B, T, _ = x.shape
    mod = jnp.einsum('bc,ce->be', jax.nn.silu(c), w_ada) + b_ada
    shift, scale, gate = jnp.split(mod[:, None, :], 3, axis=-1)
    h = rmsnorm(x, norm_g) * (1.0 + scale) + shift
    u = jnp.einsum('btd,de->bte', h, w_in)
    (nq, nkv, ng, nz, fq, fk, fv, ff, fz, rq, rk, rv, rz, mg) = jnp.split(
        u, np.cumsum(IN_SPLITS)[:-1].tolist(), axis=-1)
    nq = nq.reshape(B, T, NSA_KV, NSA_REP, HEAD_DIM)
    nkv = nkv.reshape(B, T, 3, 2, NSA_KV, HEAD_DIM)
    cmp_rows, sel_rows, win_rows = nkv[:, :, 0], nkv[:, :, 1], nkv[:, :, 2]
    ng = ng.reshape(B, T, NSA_KV, NSA_REP, 3)
    fq = fq.reshape(B, T, FOX_HEADS, HEAD_DIM)
    fox_rows = jnp.stack([fk.reshape(B, T, FOX_HEADS, HEAD_DIM),
                          fv.reshape(B, T, FOX_HEADS, HEAD_DIM)], axis=2)
    logf = jax.nn.log_sigmoid(ff.astype(f32) + b_forget.astype(f32))
    rq = rq.reshape(B, T, RET_HEADS, HEAD_DIM)
    rk = rk.reshape(B, T, RET_HEADS, HEAD_DIM) * (HEAD_DIM ** -0.5)
    rv = rv.reshape(B, T, RET_HEADS, HEAD_DIM)
    log_gamma = jnp.log1p(-jnp.exp2(-5.0 - jnp.arange(RET_HEADS, dtype=f32)))
    if past is None:
        o_nsa = nsa_prompt(nq, cmp_rows, sel_rows, win_rows, ng, cw)
        pad = jnp.zeros((B, wb_len) + win_rows.shape[2:], win_rows.dtype)
        win_state = jnp.concatenate([pad, win_rows], axis=1)[:, -wb_len:]
        o_fox = fox_prompt(fq, fox_rows[:, :, 0], fox_rows[:, :, 1], logf)
        o_ret, ret_state = retention_prompt(rq, rk, rv, log_gamma)
    else:
        cmp_past, sel_past, win_buf, fox_past, logf_past, ret_prev = past
        o_nsa, win_state = nsa_sample(nq, cmp_rows, sel_rows, win_rows, ng, cw, cmp_past, sel_past, win_buf)
        o_fox = fox_sample(fq, fox_rows[:, :, 0], fox_rows[:, :, 1], logf,
                           fox_past[:, :, 0], fox_past[:, :, 1], logf_past)
        o_ret, ret_state = retention_chunk(ret_prev.astype(f32), rq, rk, rv, log_gamma)
    dt = x.dtype
    br = jnp.stack([o_nsa.reshape(B, T, BRANCH_WIDTH) * jax.nn.silu(nz),
                    o_fox.reshape(B, T, BRANCH_WIDTH) * jax.nn.silu(fz),
                    head_norm(o_ret).astype(dt).reshape(B, T, BRANCH_WIDTH) * jax.nn.silu(rz)], axis=2)
    proj = jnp.einsum('btnw,nwd->btnd', br, w_branch)
    merged = jnp.sum(jax.nn.sigmoid(mg.reshape(B, T, N_BRANCH, D_MODEL)) * proj, axis=2)
    y = jnp.einsum('btd,de->bte', merged, w_out)
    x = x + gate * y
    return x, (cmp_rows, sel_rows, win_state, fox_rows, logf, ret_state)


def setup_inputs(seed: int = 0) -> dict:
    key = jax.random.key(seed)
    k = jax.random.split(key, 25)
    f32 = jnp.float32
    n_pages = PAST_LEN // PAGE_SIZE
    n_used = DEC_BATCH * n_pages
    n_pool = n_used + max(1, n_used // 4)
    win_buf = min(WINDOW, PAST_LEN)

    def nrm(kk, shape, std):
        return std * jax.random.normal(kk, shape, f32)

    kv_nsa = (DEPTH, n_pool, PAGE_SIZE, 2, NSA_KV, HEAD_DIM)
    return {
        'x_prompt': nrm(k[0], (BATCH, SEQ, D_MODEL), 1.0),
        'x_sample': nrm(k[1], (DEC_BATCH, DEC_SEQ, D_MODEL), 1.0),
        'cache_nsa_cmp_kv': nrm(k[2], kv_nsa, 1.0),
        'cache_nsa_sel_kv': nrm(k[3], kv_nsa, 1.0),
        'cache_nsa_win_kv': nrm(k[4], (DEPTH, DEC_BATCH, win_buf, 2, NSA_KV, HEAD_DIM), 1.0),
        'cache_fox_kv': nrm(k[5], (DEPTH, n_pool, PAGE_SIZE, 2, FOX_HEADS, HEAD_DIM), 1.0),
        'cache_fox_logf': jax.nn.log_sigmoid(jax.random.uniform(k[6], (DEPTH, n_pool, PAGE_SIZE, FOX_HEADS), f32, 1.0, 6.0)),
        'state_ret': nrm(k[7], (DEPTH, DEC_BATCH, RET_HEADS, HEAD_DIM, HEAD_DIM), 1.0),
        'page_table': jax.random.permutation(k[8], n_pool)[:n_used].reshape(DEC_BATCH, n_pages).astype(jnp.int32),
        'c_prompt': nrm(k[9], (BATCH, D_MODEL), 1.0),
        'c_sample': nrm(k[10], (DEC_BATCH, D_MODEL), 1.0),
        'norm_g': 1.0 + nrm(k[11], (DEPTH, D_MODEL), 0.02),
        'w_ada': nrm(k[12], (DEPTH, D_MODEL, 3 * D_MODEL), 0.5 * D_MODEL ** -0.5),
        'b_ada': nrm(k[13], (DEPTH, 3 * D_MODEL), 0.02),
        'w_in': nrm(k[14], (DEPTH, D_MODEL, D_IN), D_MODEL ** -0.5),
        'b_forget': jax.random.uniform(k[15], (DEPTH, FOX_HEADS), f32, 1.0, 6.0),
        'w_cmp_k1': nrm(k[16], (DEPTH, CMP_LEN, HEAD_DIM, CMP_HIDDEN), (CMP_LEN * HEAD_DIM) ** -0.5),
        'w_cmp_k2': nrm(k[17], (DEPTH, CMP_HIDDEN, HEAD_DIM), CMP_HIDDEN ** -0.5),
        'pos_cmp_k': nrm(k[18], (DEPTH, CMP_LEN, HEAD_DIM), 0.1),
        'w_cmp_v1': nrm(k[19], (DEPTH, CMP_LEN, HEAD_DIM, CMP_HIDDEN), (CMP_LEN * HEAD_DIM) ** -0.5),
        'w_cmp_v2': nrm(k[20], (DEPTH, CMP_HIDDEN, HEAD_DIM), CMP_HIDDEN ** -0.5),
        'pos_cmp_v': nrm(k[21], (DEPTH, CMP_LEN, HEAD_DIM), 0.1),
        'w_branch': nrm(k[22], (DEPTH, N_BRANCH, BRANCH_WIDTH, D_MODEL), BRANCH_WIDTH ** -0.5),
        'w_out': nrm(k[23], (DEPTH, D_MODEL, D_MODEL), D_MODEL ** -0.5),
        'final_g': 1.0 + nrm(k[24], (D_MODEL,), 0.02),
    }


def reference(x_prompt, x_sample, cache_nsa_cmp_kv, cache_nsa_sel_kv, cache_nsa_win_kv, cache_fox_kv,
              cache_fox_logf, state_ret, page_table, c_prompt, c_sample, norm_g, w_ada, b_ada, w_in,
              b_forget, w_cmp_k1, w_cmp_k2, pos_cmp_k, w_cmp_v1, w_cmp_v2, pos_cmp_v, w_branch, w_out,
              final_g):
    n_seq, n_pages = page_table.shape
    past_len = n_pages * cache_nsa_cmp_kv.shape[2]
    wb_len = cache_nsa_win_kv.shape[2]

    def gather(pool):
        rows = pool[page_table]
        return rows.reshape((n_seq, past_len) + pool.shape[2:])

    hp, hs = x_prompt, x_sample
    prompt_rows, sample_rows = [], []
    for l in range(DEPTH):
        cw = (w_cmp_k1[l], w_cmp_k2[l], pos_cmp_k[l], w_cmp_v1[l], w_cmp_v2[l], pos_cmp_v[l])
        hp, rp = decoder_layer(hp, c_prompt, norm_g[l], w_ada[l], b_ada[l], w_in[l], b_forget[l], cw,
                               w_branch[l], w_out[l], wb_len, None)
        past = (gather(cache_nsa_cmp_kv[l]), gather(cache_nsa_sel_kv[l]), cache_nsa_win_kv[l],
                gather(cache_fox_kv[l]), gather(cache_fox_logf[l]), state_ret[l])
        hs, rs = decoder_layer(hs, c_sample, norm_g[l], w_ada[l], b_ada[l], w_in[l], b_forget[l], cw,
                               w_branch[l], w_out[l], wb_len, past)
        prompt_rows.append(rp)
        sample_rows.append(rs)

    def stacked(rows, i):
        return jnp.stack([r[i] for r in rows], axis=0)

    y_prompt = rmsnorm(hp, final_g)
    y_sample = rmsnorm(hs, final_g)
    return (y_prompt, y_sample,
            stacked(prompt_rows, 0), stacked(sample_rows, 0),
            stacked(prompt_rows, 1), stacked(sample_rows, 1),
            stacked(prompt_rows, 2), stacked(sample_rows, 2),
            stacked(prompt_rows, 3), stacked(sample_rows, 3),
            stacked(prompt_rows, 4), stacked(sample_rows, 4),
            stacked(prompt_rows, 5), stacked(sample_rows, 5))
```

```python
import functools

import jax
import jax.numpy as jnp
from jax import lax
from jax.experimental import pallas as pl
from jax.experimental.pallas import tpu as pltpu

F32 = jnp.float32
BF16 = jnp.bfloat16

D_MODEL = 1024
HEAD_DIM = 64
NSA_KV = 2
NSA_REP = 4
CMP_LEN = 32
CMP_STRIDE = 16
CMP_HIDDEN = 128
SEL_BLOCK = 64
SEL_TOPK = 16
WINDOW = 512
FOX_HEADS = 8
RET_HEADS = 8
RET_CHUNK = 128
EPS = 1e-6
NEG_INF = -1e30
FORCE = 1e9
LANES = 128
PAGE = 128
SAMPLE_ROWS = 8
VMEM_LIMIT = 48 * 1024 * 1024

C_NQ, C_FQ, C_RQ, C_RK, C_RV = 0, 512, 1024, 1536, 2048
C_NZ, C_FZ, C_RZ = 2560, 3072, 3584
C_FKV = 4096
C_MG = 5120
C_CMP, C_SEL, C_WIN = 8192, 8448, 8704
C_NG, C_FF = 8960, 9088
D_PAD = 9216
CMP_FEAT = CMP_STRIDE * 4 * HEAD_DIM


def _params(sem, vmem=VMEM_LIMIT):
    return pltpu.CompilerParams(dimension_semantics=sem, vmem_limit_bytes=vmem)


def _nt(a, b):
    return lax.dot_general(a, b, (((1,), (1,)), ((), ())), preferred_element_type=F32)


def _tn(a, b):
    return lax.dot_general(a, b, (((0,), (0,)), ((), ())), preferred_element_type=F32)


def _dot(a, b):
    return jnp.dot(a, b, preferred_element_type=F32)


def _split3(x):
    hi = x.astype(BF16)
    r1 = x - hi.astype(F32)
    mid = r1.astype(BF16)
    lo = (r1 - mid.astype(F32)).astype(BF16)
    return hi, mid, lo


def _silu(x):
    return x * jax.nn.sigmoid(x)


def _log_sigmoid(x):
    return jnp.minimum(x, 0.0) - jnp.log(1.0 + jnp.exp(-jnp.abs(x)))


def _iota(shape, dim):
    return lax.broadcasted_iota(jnp.int32, shape, dim)


def _div(x, n):
    return lax.shift_right_arithmetic(x, jnp.int32(n.bit_length() - 1))


def _mod(x, n):
    return x & (n - 1)


def _lane_tile(x, n):
    return x if n == 1 else jnp.concatenate([x] * n, axis=1)


def _adaln_kernel(c_ref, w_ref, b_ref, o_ref):
    c = c_ref[...]
    o_ref[...] = _dot(_silu(c).astype(BF16), w_ref[...]) + b_ref[...]


def _adaln(c_all, w_bf, b):
    rows, d = c_all.shape
    n = w_bf.shape[1]
    tn = 1024
    return pl.pallas_call(
        _adaln_kernel,
        out_shape=jax.ShapeDtypeStruct((rows, n), F32),
        grid=(n // tn,),
        in_specs=[pl.BlockSpec((rows, d), lambda j: (0, 0)),
                  pl.BlockSpec((d, tn), lambda j: (0, j)),
                  pl.BlockSpec((1, tn), lambda j: (0, j))],
        out_specs=pl.BlockSpec((rows, tn), lambda j: (0, j)),
        compiler_params=_params(("arbitrary",)),
        name="adaln",
    )(c_all, w_bf, b.reshape(1, n))


def _inproj_kernel(x_ref, g_ref, sc_ref, sh_ref, w_ref, o_ref, h_ref):
    @pl.when(pl.program_id(1) == 0)
    def _():
        x = x_ref[...]
        ms = jnp.mean(x * x, axis=-1, keepdims=True)
        y = x * lax.rsqrt(ms + EPS) * g_ref[...]
        h_ref[...] = (y * (1.0 + sc_ref[...]) + sh_ref[...]).astype(BF16)

    o_ref[...] = _dot(h_ref[...], w_ref[...])


def _mod_spec(mod, tm, rows_per_mod, d, nargs):
    if mod.ndim == 2:
        return pl.BlockSpec((tm, d), (lambda i, j: (i, 0)) if nargs == 2 else (lambda i: (i, 0)))
    per = rows_per_mod // tm
    return pl.BlockSpec((None, 1, d), (lambda i, j: (i // per, 0, 0)) if nargs == 2 else (lambda i: (i // per, 0, 0)))


def _inproj(x2d, g, scale, shift, w_pad, rows_per_mod):
    n, d = x2d.shape
    tm = min(1024, rows_per_mod, n)
    tn = 512
    return pl.pallas_call(
        _inproj_kernel,
        out_shape=jax.ShapeDtypeStruct((n, D_PAD), F32),
        grid=(n // tm, D_PAD // tn),
        in_specs=[pl.BlockSpec((tm, d), lambda i, j: (i, 0)),
                  pl.BlockSpec((1, d), lambda i, j: (0, 0)),
                  _mod_spec(scale, tm, rows_per_mod, d, 2),
                  _mod_spec(shift, tm, rows_per_mod, d, 2),
                  pl.BlockSpec((d, tn), lambda i, j: (0, j))],
        out_specs=pl.BlockSpec((tm, tn), lambda i, j: (i, j)),
        scratch_shapes=[pltpu.VMEM((tm, d), BF16)],
        compiler_params=_params(("arbitrary", "arbitrary")),
        name="inproj",
    )(x2d, g.reshape(1, d), scale, shift, w_pad)


def _lane_cumsum(lft, carry):
    t = lft.shape[1]
    upper = (_iota((t, t), 0) <= _iota((t, t), 1)).astype(BF16)
    c = sum(_dot(p, upper) for p in _split3(lft)) + carry
    return c, carry + jnp.sum(lft, axis=1, keepdims=True)


def _head_major_cumsum(lf, carry):
    eye = (_iota((FOX_HEADS, LANES), 0) == _iota((FOX_HEADS, LANES), 1)).astype(BF16)
    lft = sum(_nt(eye, p) for p in _split3(lf))
    return _lane_cumsum(lft, carry)


def _logf_prompt_kernel(ff_ref, bf_ref, logf_ref, c_ref, carry_ref):
    @pl.when(pl.program_id(1) == 0)
    def _():
        carry_ref[...] = jnp.zeros_like(carry_ref)

    lf = _log_sigmoid(ff_ref[...] + bf_ref[...])
    logf_ref[...] = lf[:, :FOX_HEADS]
    c, carry = _head_major_cumsum(lf, carry_ref[...])
    c_ref[...] = c
    carry_ref[...] = carry


def _logf_prompt(u, bf_pad, batch, seq):
    tb = min(512, seq)
    nt = seq // tb
    return pl.pallas_call(
        _logf_prompt_kernel,
        out_shape=(jax.ShapeDtypeStruct((batch, seq, FOX_HEADS), F32),
                   jax.ShapeDtypeStruct((batch, FOX_HEADS, seq), F32)),
        grid=(batch, nt),
        in_specs=[pl.BlockSpec((tb, LANES), lambda b, i: (b * nt + i, C_FF // LANES)),
                  pl.BlockSpec((1, LANES), lambda b, i: (0, 0))],
        out_specs=(pl.BlockSpec((None, tb, FOX_HEADS), lambda b, i: (b, i, 0)),
                   pl.BlockSpec((None, FOX_HEADS, tb), lambda b, i: (b, 0, i))),
        scratch_shapes=[pltpu.VMEM((FOX_HEADS, 1), F32)],
        compiler_params=_params(("arbitrary", "arbitrary")),
        name="logf_prompt",
    )(u, bf_pad)


def _logf_past_kernel(pt_ref, *refs, npg):
    pages, (c_ref, carry_ref) = refs[:npg], refs[npg:]

    @pl.when(pl.program_id(1) == 0)
    def _():
        carry_ref[...] = jnp.zeros_like(carry_ref)

    carry = carry_ref[...]
    for k in range(npg):
        c, carry = _lane_cumsum(pages[k][...], carry)
        c_ref[:, k * PAGE:(k + 1) * PAGE] = c
    carry_ref[...] = carry


def _logf_past(pool_logf_t, page_table, npg=16):
    nseq, n_pages = page_table.shape
    npg = min(npg, n_pages)
    steps = n_pages // npg

    def page_spec(k):
        return pl.BlockSpec((None, FOX_HEADS, PAGE), lambda b, j, pt: (pt[b, j * npg + k], 0, 0))

    grid_spec = pltpu.PrefetchScalarGridSpec(
        num_scalar_prefetch=1, grid=(nseq, steps),
        in_specs=[page_spec(k) for k in range(npg)],
        out_specs=pl.BlockSpec((None, FOX_HEADS, npg * PAGE), lambda b, j, pt: (b, 0, j)),
        scratch_shapes=[pltpu.VMEM((FOX_HEADS, 1), F32)])
    return pl.pallas_call(
        functools.partial(_logf_past_kernel, npg=npg),
        out_shape=jax.ShapeDtypeStruct((nseq, FOX_HEADS, n_pages * PAGE), F32),
        grid_spec=grid_spec,
        compiler_params=_params(("arbitrary", "arbitrary")),
        name="logf_past",
    )(page_table, *([pool_logf_t] * npg))


def _fox_prompt_kernel(q_ref, k_ref, v_ref, cq_ref, ck_ref, o_ref, qm_ref, m_ref, l_ref, acc_ref):
    i, j = pl.program_id(2), pl.program_id(3)
    tq, tk = q_ref.shape[0], k_ref.shape[0]

    @pl.when(j == 0)
    def _():
        q = q_ref[...] * (HEAD_DIM ** -0.5)
        low = _iota(q.shape, 1) < HEAD_DIM
        qm_ref[0] = jnp.where(low, q, 0.0).astype(BF16)
        qm_ref[1] = jnp.where(low, 0.0, q).astype(BF16)
        m_ref[...] = jnp.full_like(m_ref, NEG_INF)
        l_ref[...] = jnp.zeros_like(l_ref)
        acc_ref[...] = jnp.zeros_like(acc_ref)

    def step(masked):
        k = k_ref[...].astype(BF16)
        v = v_ref[...].astype(BF16)
        if masked:
            keep = _iota((tq, tk), 1) <= _iota((tq, tk), 0)
        for h in range(2):
            c_shift = jnp.max(cq_ref[h:h + 1, :], axis=1, keepdims=True)
            s = _nt(qm_ref[h], k) + (c_shift - ck_ref[h:h + 1, :])
            if masked:
                s = jnp.where(keep, s, -jnp.inf)
            m_prev = m_ref[h]
            m_new = jnp.maximum(m_prev, jnp.max(s, axis=1, keepdims=True))
            alpha = jnp.exp(m_prev - m_new)
            p = jnp.exp(s - m_new)
            l_ref[h] = alpha * l_ref[h] + jnp.sum(p, axis=1, keepdims=True)
            acc_ref[h] = alpha * acc_ref[h] + _dot(p.astype(BF16), v)
            m_ref[h] = m_new

    pl.when(j < i)(lambda: step(False))
    pl.when(j == i)(lambda: step(True))

    @pl.when(j == pl.num_programs(3) - 1)
    def _():
        o0 = acc_ref[0] / jnp.maximum(l_ref[0], 1e-30)
        o1 = acc_ref[1] / jnp.maximum(l_ref[1], 1e-30)
        o_ref[...] = jnp.where(_iota(o0.shape, 1) < HEAD_DIM, o0, o1)


def _fox_prompt(u, c_hd, batch, seq):
    tq = min(512, seq)
    nq = seq // tq
    pairs = FOX_HEADS // 2
    c4 = c_hd.reshape(batch, pairs, 2, seq)
    qcol, kcol, vcol = C_FQ // LANES, C_FKV // LANES, C_FKV // LANES + pairs
    return pl.pallas_call(
        _fox_prompt_kernel,
        out_shape=jax.ShapeDtypeStruct((batch * seq, FOX_HEADS * HEAD_DIM), F32),
        grid=(batch, pairs, nq, nq),
        in_specs=[pl.BlockSpec((tq, LANES), lambda b, hp, i, j: (b * nq + i, qcol + hp)),
                  pl.BlockSpec((tq, LANES), lambda b, hp, i, j: (b * nq + jnp.minimum(j, i), kcol + hp)),
                  pl.BlockSpec((tq, LANES), lambda b, hp, i, j: (b * nq + jnp.minimum(j, i), vcol + hp)),
                  pl.BlockSpec((None, None, 2, tq), lambda b, hp, i, j: (b, hp, 0, i)),
                  pl.BlockSpec((None, None, 2, tq), lambda b, hp, i, j: (b, hp, 0, jnp.minimum(j, i)))],
        out_specs=pl.BlockSpec((tq, LANES), lambda b, hp, i, j: (b * nq + i, hp)),
        scratch_shapes=[pltpu.VMEM((2, tq, LANES), BF16), pltpu.VMEM((2, tq, 1), F32),
                        pltpu.VMEM((2, tq, 1), F32), pltpu.VMEM((2, tq, LANES), F32)],
        compiler_params=_params(("arbitrary",) * 4),
        name="fox_prompt",
    )(u, u, u, c4, c4)


def _retention_kernel(q_ref, k_ref, v_ref, lg_ref, s0_ref, o_ref, s_ref, sbd_ref, *, c_true):
    ci = pl.program_id(2)
    c = q_ref.shape[0]
    low_row = _iota((LANES, LANES), 0) < HEAD_DIM
    low_col = _iota((LANES, LANES), 1) < HEAD_DIM

    @pl.when(ci == 0)
    def _():
        sbd_ref[...] = s0_ref[...]

    lg = lg_ref[...]
    low = _iota((c, LANES), 1) < HEAD_DIM
    pos = _iota((c, LANES), 0).astype(F32)
    q = q_ref[...]
    k = k_ref[...] * (HEAD_DIM ** -0.5)
    v = v_ref[...].astype(BF16)
    kb = k.astype(BF16)
    diff = (_iota((c, c), 0) - _iota((c, c), 1)).astype(F32)
    inner = []
    for h in range(2):
        lgh = jnp.max(jnp.where(low[:1] == (h == 0), lg, -jnp.inf), axis=1, keepdims=True)
        decay = jnp.where(diff >= 0, jnp.exp(jnp.maximum(diff, 0.0) * lgh), 0.0)
        qm = jnp.where(low == (h == 0), q, 0.0).astype(BF16)
        scores = _nt(qm, kb) * decay
        inner.append(_dot(scores.astype(BF16), v))
    sbd = sbd_ref[...]
    cross = _dot((q * jnp.exp((pos + 1.0) * lg)).astype(BF16), sbd.astype(BF16))
    o = jnp.where(low, inner[0], inner[1]) + cross

    kd = jnp.where(pos < c_true, k * jnp.exp((c_true - 1.0 - pos) * lg), 0.0).astype(BF16)
    upd = jnp.where(low_row == low_col, _tn(kd, v), 0.0)
    sbd_new = jnp.exp(c_true * lg) * sbd + upd
    sbd_ref[...] = sbd_new

    inv = 1.0 / HEAD_DIM
    s_lo = jnp.sum(jnp.where(low, o, 0.0), axis=1, keepdims=True)
    s_hi = jnp.sum(jnp.where(low, 0.0, o), axis=1, keepdims=True)
    d = o - jnp.where(low, s_lo, s_hi) * inv
    d2 = d * d
    v_lo = jnp.sum(jnp.where(low, d2, 0.0), axis=1, keepdims=True)
    v_hi = jnp.sum(jnp.where(low, 0.0, d2), axis=1, keepdims=True)
    o_ref[...] = d * lax.rsqrt(jnp.where(low, v_lo, v_hi) * inv + EPS)

    @pl.when(ci == pl.num_programs(2) - 1)
    def _():
        s_ref[...] = sbd_new


def _to_block_diag(s):
    b = s.shape[0]
    s = s.reshape(b, RET_HEADS // 2, 2, HEAD_DIM, HEAD_DIM)
    z = jnp.zeros_like(s[:, :, 0])
    return jnp.concatenate([jnp.concatenate([s[:, :, 0], z], axis=-1),
                            jnp.concatenate([z, s[:, :, 1]], axis=-1)], axis=-2)


def _from_block_diag(sbd):
    b = sbd.shape[0]
    return jnp.stack([sbd[:, :, :HEAD_DIM, :HEAD_DIM], sbd[:, :, HEAD_DIM:, HEAD_DIM:]],
                     axis=2).reshape(b, RET_HEADS, HEAD_DIM, HEAD_DIM)


def _retention(u, state0, lg_lanes, batch, rows_per_seq, chunk, c_true):
    nc = rows_per_seq // chunk
    pairs = RET_HEADS // 2
    qcol, kcol, vcol = C_RQ // LANES, C_RK // LANES, C_RV // LANES
    o, sbd = pl.pallas_call(
        functools.partial(_retention_kernel, c_true=c_true),
        out_shape=(jax.ShapeDtypeStruct((batch * rows_per_seq, RET_HEADS * HEAD_DIM), F32),
                   jax.ShapeDtypeStruct((batch, pairs, LANES, LANES), F32)),
        grid=(batch, pairs, nc),
        in_specs=[pl.BlockSpec((chunk, LANES), lambda b, hp, ci: (b * nc + ci, qcol + hp)),
                  pl.BlockSpec((chunk, LANES), lambda b, hp, ci: (b * nc + ci, kcol + hp)),
                  pl.BlockSpec((chunk, LANES), lambda b, hp, ci: (b * nc + ci, vcol + hp)),
                  pl.BlockSpec((None, 1, LANES), lambda b, hp, ci: (hp, 0, 0)),
                  pl.BlockSpec((None, None, LANES, LANES), lambda b, hp, ci: (b, hp, 0, 0))],
        out_specs=(pl.BlockSpec((chunk, LANES), lambda b, hp, ci: (b * nc + ci, hp)),
                   pl.BlockSpec((None, None, LANES, LANES), lambda b, hp, ci: (b, hp, 0, 0))),
        scratch_shapes=[pltpu.VMEM((LANES, LANES), F32)],
        compiler_params=_params(("arbitrary",) * 3),
        name="retention",
    )(u, u, u, lg_lanes, _to_block_diag(state0))
    return o, _from_block_diag(sbd)


def _cmp_bias_kernel(wt_ref, p_ref, o_ref):
    r = _nt(wt_ref[...], p_ref[...].astype(BF16))
    part = _mod(_div(_iota(r.shape, 0), CMP_HIDDEN), 2)
    col = jnp.sum(jnp.where(_iota(r.shape, 1) == part, r, 0.0), axis=1, keepdims=True)
    o_ref[...] = jnp.broadcast_to(col, o_ref.shape)


def _cmp_bias(wt, p_pad):
    return pl.pallas_call(
        _cmp_bias_kernel,
        out_shape=jax.ShapeDtypeStruct((wt.shape[0], LANES), F32),
        compiler_params=_params(None),
        name="cmp_bias",
    )(wt, p_pad)


def _cmp_pre_kernel(x_ref, wt_ref, o_ref):
    o_ref[...] = _nt(wt_ref[...], x_ref[...].astype(BF16))


def _cmp_pre_paged_kernel(pt_ref, *refs, npg):
    pages, (wt_ref, o_ref, stage_ref) = refs[:npg], refs[npg:]
    spp = PAGE // CMP_STRIDE
    feat = 4 * HEAD_DIM
    j = _iota((PAGE, PAGE), 0)
    perm = (_iota((PAGE, PAGE), 1) == _mod(j, spp) * CMP_STRIDE + _div(j, spp)).astype(BF16)
    for k in range(npg):
        xp = _nt(perm, pages[k][...].astype(BF16))
        for s in range(CMP_STRIDE):
            stage_ref[s, k * spp:(k + 1) * spp, :] = xp[s * spp:(s + 1) * spp, :]
    acc = None
    for s in range(CMP_STRIDE):
        term = _nt(wt_ref[:, s * feat:(s + 1) * feat], stage_ref[s].astype(BF16))
        acc = term if acc is None else acc + term
    o_ref[...] = acc


def _cmp_pre_prompt(x_seg, wt):
    batch, n_seg, feat = x_seg.shape
    ts = min(128, n_seg)
    return pl.pallas_call(
        _cmp_pre_kernel,
        out_shape=jax.ShapeDtypeStruct((batch, wt.shape[0], n_seg), F32),
        grid=(batch, n_seg // ts),
        in_specs=[pl.BlockSpec((None, ts, feat), lambda b, j: (b, j, 0)),
                  pl.BlockSpec(wt.shape, lambda b, j: (0, 0))],
        out_specs=pl.BlockSpec((None, wt.shape[0], ts), lambda b, j: (b, 0, j)),
        compiler_params=_params(("arbitrary", "arbitrary")),
        name="cmp_pre_prompt",
    )(x_seg, wt)


def _cmp_pre_paged(pool_t, page_table, wt, npg=16):
    nseq, n_pages = page_table.shape
    npg = min(npg, n_pages)
    spp = PAGE // CMP_STRIDE
    feat = pool_t.shape[1]

    def page_spec(k):
        return pl.BlockSpec((None, feat, PAGE), lambda b, j, pt: (pt[b, j * npg + k], 0, 0))

    grid_spec = pltpu.PrefetchScalarGridSpec(
        num_scalar_prefetch=1, grid=(nseq, n_pages // npg),
        in_specs=[page_spec(k) for k in range(npg)] + [pl.BlockSpec(wt.shape, lambda b, j, pt: (0, 0))],
        out_specs=pl.BlockSpec((None, wt.shape[0], npg * spp), lambda b, j, pt: (b, 0, j)),
        scratch_shapes=[pltpu.VMEM((CMP_STRIDE, npg * spp, feat), F32)])
    return pl.pallas_call(
        functools.partial(_cmp_pre_paged_kernel, npg=npg),
        out_shape=jax.ShapeDtypeStruct((nseq, wt.shape[0], n_pages * spp), F32),
        grid_spec=grid_spec,
        compiler_params=_params(("arbitrary", "arbitrary")),
        name="cmp_pre_paged",
    )(page_table, *([pool_t] * npg), wt)


def _cmp_post_kernel(pre_ref, bias_ref, w2t_ref, o_ref):
    n_seg = pre_ref.shape[1]
    reps = n_seg // LANES
    for c in range(4):
        base = c * 2 * CMP_HIDDEN
        lo = pre_ref[base:base + CMP_HIDDEN, :] + _lane_tile(bias_ref[base:base + CMP_HIDDEN, :], reps)
        hi = (pre_ref[base + CMP_HIDDEN:base + 2 * CMP_HIDDEN, :]
              + _lane_tile(bias_ref[base + CMP_HIDDEN:base + 2 * CMP_HIDDEN, :], reps))
        hid = _silu(lo + pltpu.roll(hi, n_seg - 1, 1))
        o_ref[c * HEAD_DIM:(c + 1) * HEAD_DIM, :] = _dot(w2t_ref[c], hid.astype(BF16))


def _cmp_post(pre_t, bias, w2t):
    batch, rows, n_seg = pre_t.shape
    return pl.pallas_call(
        _cmp_post_kernel,
        out_shape=jax.ShapeDtypeStruct((batch, 4 * HEAD_DIM, n_seg), F32),
        grid=(batch,),
        in_specs=[pl.BlockSpec((None, rows, n_seg), lambda b: (b, 0, 0)),
                  pl.BlockSpec(bias.shape, lambda b: (0, 0)),
                  pl.BlockSpec(w2t.shape, lambda b: (0, 0, 0))],
        out_specs=pl.BlockSpec((None, 4 * HEAD_DIM, n_seg), lambda b: (b, 0, 0)),
        compiler_params=_params(("arbitrary",)),
        name="cmp_post",
    )(pre_t, bias, w2t)


def _group_queries(q, g, rows):
    low = _iota((rows, LANES), 1) < HEAD_DIM
    out = []
    for r in range(NSA_REP):
        chunk = 2 * g + r // 2
        x = q[:, chunk * LANES:(chunk + 1) * LANES] * (HEAD_DIM ** -0.5)
        x = jnp.where(low == (r % 2 == 0), x, 0.0)
        if r % 2 != g:
            x = pltpu.roll(x, HEAD_DIM, 1)
        out.append(x)
    return jnp.concatenate(out, axis=0).astype(BF16)


def _slope_col(g, rows):
    r = _div(_iota((NSA_REP * rows, 1), 0), rows)
    return jnp.exp2(-(r + (NSA_REP * g + 1)).astype(F32))


def _softmax_rows(s, valid):
    s = jnp.where(valid, s, NEG_INF)
    m = jnp.max(s, axis=1, keepdims=True)
    p = jnp.where(valid, jnp.exp(s - m), 0.0)
    return p / jnp.maximum(jnp.sum(p, axis=1, keepdims=True), 1e-30)


def _tile_rows(x, n):
    return jnp.concatenate([x] * n, axis=0)


def _cover(n_cmp_pad, n_blk_pad):
    n = _iota((n_cmp_pad, n_blk_pad), 0) * CMP_STRIDE
    j = _iota((n_cmp_pad, n_blk_pad), 1) * SEL_BLOCK
    return ((n < j + SEL_BLOCK) & (n + CMP_LEN - 1 >= j)).astype(BF16)


def _select_blocks(imp, tpos, n_sel):
    blk = _iota(imp.shape, 1)
    cur = _div(tpos, SEL_BLOCK)
    forced = (blk == 0) | (blk == cur) | (blk == cur - 1)
    imp = jnp.where(forced, FORCE, imp)
    imp = jnp.where(blk * SEL_BLOCK <= tpos, imp, NEG_INF)
    imp = jnp.where(blk < n_sel, imp, -jnp.inf)
    blkf = blk.astype(F32)

    def body(_, carry):
        imp, sel = carry
        m = jnp.max(imp, axis=1, keepdims=True)
        idx = jnp.min(jnp.where(imp == m, blkf, 1e9), axis=1, keepdims=True)
        hit = blkf == idx
        return jnp.where(hit, -jnp.inf, imp), jnp.where(hit, 1.0, sel)

    _, sel = lax.fori_loop(0, min(SEL_TOPK, n_sel), body, (imp, jnp.zeros_like(imp)))
    return sel


def _cmp_branch(qm, kct, vct, dist, valid, slope, rows):
    s = _dot(qm, kct) - slope * _tile_rows(dist, NSA_REP)
    p = _softmax_rows(s, _tile_rows(valid, NSA_REP))
    o = _nt(p.astype(BF16), vct)
    psum = p[0:rows] + p[rows:2 * rows] + p[2 * rows:3 * rows] + p[3 * rows:4 * rows]
    return o, psum


def _importance(psum, cover):
    hi = psum.astype(BF16)
    lo = (psum - hi.astype(F32)).astype(BF16)
    return _dot(hi, cover) + _dot(lo, cover)


def _gate_cols(gates, g, rows):
    lane = _iota(gates.shape, 1)
    cols = []
    for c in range(3):
        per_head = [jnp.sum(jnp.where(lane == (g * NSA_REP + r) * 3 + c, gates, 0.0), axis=1, keepdims=True)
                    for r in range(NSA_REP)]
        cols.append(jnp.concatenate(per_head, axis=0))
    return cols


def _place_group(o, g, rows):
    low = _iota((rows, LANES), 1) < HEAD_DIM
    chunks = []
    for kk in range(2):
        a = o[(2 * kk) * rows:(2 * kk + 1) * rows]
        b = o[(2 * kk + 1) * rows:(2 * kk + 2) * rows]
        if g == 1:
            a = pltpu.roll(a, HEAD_DIM, 1)
        else:
            b = pltpu.roll(b, HEAD_DIM, 1)
        chunks.append(jnp.where(low, a, b))
    return chunks


def _online_update(s, v, m_ref, l_ref, acc_ref, v_transposed=False):
    m_prev = m_ref[...]
    m_new = jnp.maximum(m_prev, jnp.max(s, axis=1, keepdims=True))
    alpha = jnp.exp(m_prev - m_new)
    p = jnp.exp(s - m_new)
    l_ref[...] = alpha * l_ref[...] + jnp.sum(p, axis=1, keepdims=True)
    pb = p.astype(BF16)
    acc_ref[...] = alpha * acc_ref[...] + (_nt(pb, v) if v_transposed else _dot(pb, v))
    m_ref[...] = m_new


def _nsa_prompt_kernel(q_ref, gate_ref, kvt_ref, sel_ref, w0, w1, w2, w3, w4, o_ref,
                       selbf_ref, m_ref, l_ref, acc_ref, *, n_cmp, tk):
    i = pl.program_id(1)
    tq = q_ref.shape[0]
    seq = sel_ref.shape[0]
    n_seg = kvt_ref.shape[1]
    n_blk = seq // SEL_BLOCK
    blk_pad = max(LANES, n_blk)
    t0 = i * tq

    @pl.when(i == 0)
    def _():
        def cast(cidx, _):
            r = pl.multiple_of(cidx * 512, 512)
            selbf_ref[pl.ds(r, 512), :] = sel_ref[pl.ds(r, 512), :].astype(BF16)
            return 0
        lax.fori_loop(0, seq // 512, cast, 0)

    tpos = t0 + _iota((tq, 1), 0)
    kct = kvt_ref[0:2 * HEAD_DIM, :].astype(BF16)
    vct = kvt_ref[2 * HEAD_DIM:4 * HEAD_DIM, :].astype(BF16)
    n_idx = _iota((tq, n_seg), 1)
    dist_c = (tpos - (n_idx * CMP_STRIDE + CMP_LEN - 1)).astype(F32)
    valid_c = (dist_c >= 0) & (n_idx < n_cmp)
    cover = _cover(n_seg, blk_pad)
    gates = jax.nn.sigmoid(gate_ref[...])
    q = q_ref[...]

    wins = [w[...] for w in (w0, w1, w2, w3, w4)]
    kw = jnp.concatenate([w[:, 0:LANES] for w in wins], axis=0).astype(BF16)
    vw = jnp.concatenate([w[:, LANES:2 * LANES] for w in wins], axis=0).astype(BF16)
    nw = kw.shape[0]
    wpos = t0 - WINDOW + _iota((tq, nw), 1)
    dist_w = (tpos - wpos).astype(F32)
    valid_w = (dist_w >= 0) & (dist_w < WINDOW) & (wpos >= 0)

    n_tiles = _div(t0 + tq + tk - 1, tk)
    for g in range(NSA_KV):
        qm = _group_queries(q, g, tq)
        slope = _slope_col(g, tq)
        o_cmp, psum = _cmp_branch(qm, kct, vct, dist_c, valid_c, slope, tq)
        sel = _select_blocks(_importance(psum, cover), tpos, n_blk)
        selb = sel.astype(BF16)

        m_ref[...] = jnp.full_like(m_ref, NEG_INF)
        l_ref[...] = jnp.zeros_like(l_ref)
        acc_ref[...] = jnp.zeros_like(acc_ref)

        def tile(j, _):
            expand = (_iota((blk_pad, tk), 0) == _div(j * tk + _iota((blk_pad, tk), 1), SEL_BLOCK)).astype(BF16)
            chosen = _dot(selb, expand)

            @pl.when(jnp.max(chosen) > 0.5)
            def _():
                r = pl.multiple_of(j * tk, tk)
                k = selbf_ref[pl.ds(r, tk), 0:LANES]
                v = selbf_ref[pl.ds(r, tk), LANES:2 * LANES]
                dist = (tpos - (j * tk + _iota((tq, tk), 1))).astype(F32)
                ok = (chosen > 0.5) & (dist >= 0)
                s = _nt(qm, k) - slope * _tile_rows(dist, NSA_REP)
                s = jnp.where(_tile_rows(ok, NSA_REP), s, -jnp.inf)
                _online_update(s, v, m_ref, l_ref, acc_ref)
            return 0

        lax.fori_loop(0, n_tiles, tile, 0)
        o_sel = acc_ref[...] / jnp.maximum(l_ref[...], 1e-30)

        s = _nt(qm, kw) - slope * _tile_rows(dist_w, NSA_REP)
        o_win = _dot(_softmax_rows(s, _tile_rows(valid_w, NSA_REP)).astype(BF16), vw)

        gc, gs, gw = _gate_cols(gates, g, tq)
        chunks = _place_group(gc * o_cmp + gs * o_sel + gw * o_win, g, tq)
        for kk in range(2):
            o_ref[:, g * 2 * LANES + kk * LANES:g * 2 * LANES + (kk + 1) * LANES] = chunks[kk]


def _nsa_prompt(u, kvt, batch, seq):
    tq = 128
    nq = seq // tq
    tk = min(256, seq)
    n_seg = kvt.shape[2]
    n_cmp = seq // CMP_STRIDE - 1
    wcol = C_WIN // 256

    def win_spec(k):
        return pl.BlockSpec((tq, 256), lambda b, i: (b * nq + jnp.maximum(i - WINDOW // tq + k, 0), wcol))

    return pl.pallas_call(
        functools.partial(_nsa_prompt_kernel, n_cmp=n_cmp, tk=tk),
        out_shape=jax.ShapeDtypeStruct((batch * seq, 512), F32),
        grid=(batch, nq),
        in_specs=[pl.BlockSpec((tq, 512), lambda b, i: (b * nq + i, C_NQ // 512)),
                  pl.BlockSpec((tq, LANES), lambda b, i: (b * nq + i, C_NG // LANES)),
                  pl.BlockSpec((None, 4 * HEAD_DIM, n_seg), lambda b, i: (b, 0, 0)),
                  pl.BlockSpec((seq, 256), lambda b, i: (b, C_SEL // 256))]
                 + [win_spec(k) for k in range(WINDOW // tq + 1)],
        out_specs=pl.BlockSpec((tq, 512), lambda b, i: (b * nq + i, 0)),
        scratch_shapes=[pltpu.VMEM((seq, 256), BF16), pltpu.VMEM((NSA_REP * tq, 1), F32),
                        pltpu.VMEM((NSA_REP * tq, 1), F32), pltpu.VMEM((NSA_REP * tq, LANES), F32)],
        compiler_params=_params(("arbitrary", "arbitrary")),
        name="nsa_prompt",
    )(u, u, kvt, u, *([u] * (WINDOW // tq + 1)))


def _nsa_cmp_sample_kernel(q_ref, kvt_ref, o_ref, sel_ref, *, past, n_cmp, n_sel):
    rows = SAMPLE_ROWS
    n_seg = kvt_ref.shape[1]
    blk_pad = sel_ref.shape[1]
    tpos = past + _iota((rows, 1), 0)
    kct = kvt_ref[0:2 * HEAD_DIM, :].astype(BF16)
    vct = kvt_ref[2 * HEAD_DIM:4 * HEAD_DIM, :].astype(BF16)
    n_idx = _iota((rows, n_seg), 1)
    dist = (tpos - (n_idx * CMP_STRIDE + CMP_LEN - 1)).astype(F32)
    valid = (dist >= 0) & (n_idx < n_cmp)
    cover = _cover(n_seg, blk_pad)
    q = q_ref[...]
    for g in range(NSA_KV):
        qm = _group_queries(q, g, rows)
        o, psum = _cmp_branch(qm, kct, vct, dist, valid, _slope_col(g, rows), rows)
        o_ref[g * NSA_REP * rows:(g + 1) * NSA_REP * rows, :] = o
        sel_ref[g * rows:(g + 1) * rows, :] = _select_blocks(_importance(psum, cover), tpos, n_sel)


def _nsa_cmp_sample(u_s, kvt, nseq, past, n_sel, blk_pad):
    n_seg = kvt.shape[2]
    return pl.pallas_call(
        functools.partial(_nsa_cmp_sample_kernel, past=past, n_cmp=n_seg - 1, n_sel=n_sel),
        out_shape=(jax.ShapeDtypeStruct((nseq, NSA_KV * NSA_REP * SAMPLE_ROWS, LANES), F32),
                   jax.ShapeDtypeStruct((nseq, NSA_KV * SAMPLE_ROWS, blk_pad), F32)),
        grid=(nseq,),
        in_specs=[pl.BlockSpec((SAMPLE_ROWS, 512), lambda b: (b, C_NQ // 512)),
                  pl.BlockSpec((None, 4 * HEAD_DIM, n_seg), lambda b: (b, 0, 0))],
        out_specs=(pl.BlockSpec((None, NSA_KV * NSA_REP * SAMPLE_ROWS, LANES), lambda b: (b, 0, 0)),
                   pl.BlockSpec((None, NSA_KV * SAMPLE_ROWS, blk_pad), lambda b: (b, 0, 0))),
        compiler_params=_params(("arbitrary",)),
        name="nsa_cmp_sample",
    )(u_s, kvt)


def _stack_groups(fn):
    return jnp.concatenate([fn(g) for g in range(NSA_KV)], axis=0)


def _nsa_selwin_sample_kernel(pt_ref, *refs, npg, past):
    pages = refs[:npg]
    (q_ref, gate_ref, selm_ref, ocmp_ref, winbuf_ref, selnew_ref, winnew_ref, o_ref,
     qm_ref, selrows_ref, m_ref, l_ref, acc_ref, pad_ref, kt_ref, vt_ref) = refs[npg:]
    j = pl.program_id(1)
    rows = SAMPLE_ROWS
    nrow = NSA_KV * NSA_REP * rows
    blk_pad = selm_ref.shape[1]
    tk = npg * PAGE
    tpos = past + _mod(_iota((nrow, 1), 0), rows)
    slope = _stack_groups(lambda g: _slope_col(g, rows))

    @pl.when(j == 0)
    def _():
        q = q_ref[...]
        qm_ref[...] = _stack_groups(lambda g: _group_queries(q, g, rows))
        selrows_ref[...] = _stack_groups(
            lambda g: _tile_rows(selm_ref[g * rows:(g + 1) * rows, :], NSA_REP)).astype(BF16)
        m_ref[...] = jnp.full_like(m_ref, NEG_INF)
        l_ref[...] = jnp.zeros_like(l_ref)
        acc_ref[...] = jnp.zeros_like(acc_ref)
        pad_ref[...] = jnp.zeros_like(pad_ref)

    def attend(score_fn, v_fn, v_transposed, kpos):
        n = kpos.shape[1]
        expand = (_iota((blk_pad, n), 0) == _div(jnp.broadcast_to(kpos, (blk_pad, n)), SEL_BLOCK)).astype(BF16)
        chosen = _dot(selrows_ref[...], expand)

        @pl.when(jnp.max(chosen) > 0.5)
        def _():
            dist = (tpos - kpos).astype(F32)
            ok = (chosen > 0.5) & (dist >= 0)
            s = jnp.where(ok, score_fn() - slope * dist, -jnp.inf)
            _online_update(s, v_fn(), m_ref, l_ref, acc_ref, v_transposed)

    for k in range(npg):
        kt_ref[:, k * PAGE:(k + 1) * PAGE] = pages[k][0].astype(BF16)
        vt_ref[:, k * PAGE:(k + 1) * PAGE] = pages[k][1].astype(BF16)
    attend(lambda: _dot(qm_ref[...], kt_ref[...]), lambda: vt_ref[...], True, j * tk + _iota((1, tk), 1))

    @pl.when(j == pl.num_programs(1) - 1)
    def _():
        pad_ref[0:rows, :] = selnew_ref[...]
        new = pad_ref[...]
        attend(lambda: _nt(qm_ref[...], new[:, 0:LANES].astype(BF16)),
               lambda: new[:, LANES:2 * LANES].astype(BF16), False, past + _iota((1, PAGE), 1))
        o_sel = acc_ref[...] / jnp.maximum(l_ref[...], 1e-30)

        pad_ref[0:rows, :] = winnew_ref[...]
        wnew = pad_ref[...]
        wb = winbuf_ref.shape[2]
        nw = wb + PAGE
        widx = _iota((1, nw), 1)
        wpos = jnp.where(widx < wb, past - wb + widx, past + widx - wb)
        dist = (tpos - wpos).astype(F32)
        valid = (dist >= 0) & (dist < WINDOW) & (wpos >= 0)
        s = jnp.concatenate([_dot(qm_ref[...], winbuf_ref[0].astype(BF16)),
                             _nt(qm_ref[...], wnew[:, 0:LANES].astype(BF16))], axis=1) - slope * dist
        p = _softmax_rows(s, valid).astype(BF16)
        o_win = _nt(p[:, 0:wb], winbuf_ref[1].astype(BF16)) + _dot(p[:, wb:], wnew[:, LANES:2 * LANES].astype(BF16))

        gates = jax.nn.sigmoid(gate_ref[...])
        o_cmp = ocmp_ref[...]
        for g in range(NSA_KV):
            gc, gs, gw = _gate_cols(gates, g, rows)
            sl = slice(g * NSA_REP * rows, (g + 1) * NSA_REP * rows)
            chunks = _place_group(gc * o_cmp[sl] + gs * o_sel[sl] + gw * o_win[sl], g, rows)
            for kk in range(2):
                o_ref[:, g * 2 * LANES + kk * LANES:g * 2 * LANES + (kk + 1) * LANES] = chunks[kk]


def _nsa_selwin_sample(u_s, pool_sel_t, page_table, selm, o_cmp, win_buf_t, past, npg=16):
    nseq, n_pages = page_table.shape
    npg = min(npg, n_pages)
    blk_pad = selm.shape[2]
    nrow = NSA_KV * NSA_REP * SAMPLE_ROWS
    wb = win_buf_t.shape[3]

    def page_spec(k):
        return pl.BlockSpec((None, 2, LANES, PAGE), lambda b, j, pt: (pt[b, j * npg + k], 0, 0, 0))

    grid_spec = pltpu.PrefetchScalarGridSpec(
        num_scalar_prefetch=1, grid=(nseq, n_pages // npg),
        in_specs=[page_spec(k) for k in range(npg)] + [
            pl.BlockSpec((SAMPLE_ROWS, 512), lambda b, j, pt: (b, C_NQ // 512)),
            pl.BlockSpec((SAMPLE_ROWS, LANES), lambda b, j, pt: (b, C_NG // LANES)),
            pl.BlockSpec((None, NSA_KV * SAMPLE_ROWS, blk_pad), lambda b, j, pt: (b, 0, 0)),
            pl.BlockSpec((None, nrow, LANES), lambda b, j, pt: (b, 0, 0)),
            pl.BlockSpec((None, 2, LANES, wb), lambda b, j, pt: (b, 0, 0, 0)),
            pl.BlockSpec((SAMPLE_ROWS, 256), lambda b, j, pt: (b, C_SEL // 256)),
            pl.BlockSpec((SAMPLE_ROWS, 256), lambda b, j, pt: (b, C_WIN // 256))],
        out_specs=pl.BlockSpec((SAMPLE_ROWS, 512), lambda b, j, pt: (b, 0)),
        scratch_shapes=[pltpu.VMEM((nrow, LANES), BF16), pltpu.VMEM((nrow, blk_pad), BF16),
                        pltpu.VMEM((nrow, 1), F32), pltpu.VMEM((nrow, 1), F32),
                        pltpu.VMEM((nrow, LANES), F32), pltpu.VMEM((PAGE, 256), F32),
                        pltpu.VMEM((LANES, npg * PAGE), BF16), pltpu.VMEM((LANES, npg * PAGE), BF16)])
    return pl.pallas_call(
        functools.partial(_nsa_selwin_sample_kernel, npg=npg, past=past),
        out_shape=jax.ShapeDtypeStruct((nseq * SAMPLE_ROWS, 512), F32),
        grid_spec=grid_spec,
        compiler_params=_params(("arbitrary", "arbitrary")),
        name="nsa_selwin_sample",
    )(page_table, *([pool_sel_t] * npg), u_s, u_s, selm, o_cmp, win_buf_t, u_s, u_s)


def _fox_sample_kernel(pt_ref, *refs, npg):
    pages = refs[:npg]
    (q_ref, cpast_ref, clast_ref, knew_ref, vnew_ref, ff_ref, bf_ref, o_ref, logf_ref,
     qbd_ref, m_ref, l_ref, acc_ref, kpad_ref, vpad_ref, fpad_ref, kt_ref, vt_ref) = refs[npg:]
    j = pl.program_id(1)
    rows = SAMPLE_ROWS
    width = FOX_HEADS * HEAD_DIM
    nrow = FOX_HEADS * rows

    @pl.when(j == 0)
    def _():
        q = q_ref[...] * (HEAD_DIM ** -0.5)
        head = _div(_iota((rows, width), 1), HEAD_DIM)
        qbd_ref[...] = jnp.concatenate(
            [jnp.where(head == h, q, 0.0) for h in range(FOX_HEADS)], axis=0).astype(BF16)
        m_ref[...] = jnp.full_like(m_ref, NEG_INF)
        l_ref[...] = jnp.zeros_like(l_ref)
        acc_ref[...] = jnp.zeros_like(acc_ref)
        kpad_ref[...] = jnp.zeros_like(kpad_ref)
        vpad_ref[...] = jnp.zeros_like(vpad_ref)
        fpad_ref[...] = jnp.zeros_like(fpad_ref)

    def head_rows(x):
        return jnp.concatenate([jnp.broadcast_to(x[h:h + 1, :], (rows, x.shape[1])) for h in range(FOX_HEADS)], axis=0)

    c_last = jnp.max(jnp.where(_iota(clast_ref.shape, 1) == clast_ref.shape[1] - 1, clast_ref[...], -jnp.inf),
                     axis=1, keepdims=True)
    for k in range(npg):
        kt_ref[:, k * PAGE:(k + 1) * PAGE] = pages[k][0].astype(BF16)
        vt_ref[:, k * PAGE:(k + 1) * PAGE] = pages[k][1].astype(BF16)
    s = _dot(qbd_ref[...], kt_ref[...]) + head_rows(c_last - cpast_ref[...])
    _online_update(s, vt_ref[...], m_ref, l_ref, acc_ref, True)

    @pl.when(j == pl.num_programs(1) - 1)
    def _():
        lf = _log_sigmoid(ff_ref[...] + bf_ref[...])
        logf_ref[...] = lf
        fpad_ref[0:rows, :] = lf
        c_new, _ = _head_major_cumsum(fpad_ref[...], 0.0)
        kpad_ref[0:rows, :] = knew_ref[...]
        vpad_ref[0:rows, :] = vnew_ref[...]
        s = _nt(qbd_ref[...], kpad_ref[...].astype(BF16)) - head_rows(c_new)
        causal = _iota((nrow, PAGE), 1) <= _mod(_iota((nrow, PAGE), 0), rows)
        _online_update(jnp.where(causal, s, -jnp.inf), vpad_ref[...].astype(BF16), m_ref, l_ref, acc_ref)
        o = acc_ref[...] / jnp.maximum(l_ref[...], 1e-30)
        head = _div(_iota((rows, width), 1), HEAD_DIM)
        out = jnp.zeros((rows, width), F32)
        for h in range(FOX_HEADS):
            out = jnp.where(head == h, o[h * rows:(h + 1) * rows], out)
        o_ref[...] = out


def _fox_sample(u_s, pool_kv_t, page_table, c_past, bf_pad, npg=8):
    nseq, n_pages = page_table.shape
    npg = min(npg, n_pages)
    width = FOX_HEADS * HEAD_DIM
    nrow = FOX_HEADS * SAMPLE_ROWS
    past = n_pages * PAGE
    tk = npg * PAGE

    def page_spec(k):
        return pl.BlockSpec((None, 2, width, PAGE), lambda b, j, pt: (pt[b, j * npg + k], 0, 0, 0))

    grid_spec = pltpu.PrefetchScalarGridSpec(
        num_scalar_prefetch=1, grid=(nseq, n_pages // npg),
        in_specs=[page_spec(k) for k in range(npg)] + [
            pl.BlockSpec((SAMPLE_ROWS, width), lambda b, j, pt: (b, C_FQ // width)),
            pl.BlockSpec((None, FOX_HEADS, tk), lambda b, j, pt: (b, 0, j)),
            pl.BlockSpec((None, FOX_HEADS, LANES), lambda b, j, pt: (b, 0, past // LANES - 1)),
            pl.BlockSpec((SAMPLE_ROWS, width), lambda b, j, pt: (b, C_FKV // width)),
            pl.BlockSpec((SAMPLE_ROWS, width), lambda b, j, pt: (b, C_FKV // width + 1)),
            pl.BlockSpec((SAMPLE_ROWS, LANES), lambda b, j, pt: (b, C_FF // LANES)),
            pl.BlockSpec((1, LANES), lambda b, j, pt: (0, 0))],
        out_specs=(pl.BlockSpec((SAMPLE_ROWS, width), lambda b, j, pt: (b, 0)),
                   pl.BlockSpec((SAMPLE_ROWS, LANES), lambda b, j, pt: (b, 0))),
        scratch_shapes=[pltpu.VMEM((nrow, width), BF16), pltpu.VMEM((nrow, 1), F32), pltpu.VMEM((nrow, 1), F32),
                        pltpu.VMEM((nrow, width), F32), pltpu.VMEM((PAGE, width), F32),
                        pltpu.VMEM((PAGE, width), F32), pltpu.VMEM((PAGE, LANES), F32),
                        pltpu.VMEM((width, tk), BF16), pltpu.VMEM((width, tk), BF16)])
    return pl.pallas_call(
        functools.partial(_fox_sample_kernel, npg=npg),
        out_shape=(jax.ShapeDtypeStruct((nseq * SAMPLE_ROWS, width), F32),
                   jax.ShapeDtypeStruct((nseq * SAMPLE_ROWS, LANES), F32)),
        grid_spec=grid_spec,
        compiler_params=_params(("arbitrary", "arbitrary")),
        name="fox_sample",
    )(page_table, *([pool_kv_t] * npg), u_s, c_past, c_past, u_s, u_s, u_s, bf_pad)


def _outproj_kernel(on_ref, of_ref, or_ref, nz_ref, fz_ref, rz_ref, g0_ref, g1_ref, g2_ref, x_ref, gate_ref,
                    wb_ref, wo_ref, fg_ref, *out_refs, final):
    merged = None
    for n, (o_ref, z_ref, g_ref) in enumerate(((on_ref, nz_ref, g0_ref), (of_ref, fz_ref, g1_ref),
                                               (or_ref, rz_ref, g2_ref))):
        br = (o_ref[...] * _silu(z_ref[...])).astype(BF16)
        term = jax.nn.sigmoid(g_ref[...]) * _dot(br, wb_ref[n])
        merged = term if merged is None else merged + term
    y = _dot(merged.astype(BF16), wo_ref[...])
    x = x_ref[...] + gate_ref[...] * y
    out_refs[0][...] = x
    if final:
        ms = jnp.mean(x * x, axis=-1, keepdims=True)
        out_refs[1][...] = x * lax.rsqrt(ms + EPS) * fg_ref[...]


def _outproj(u, o_nsa, o_fox, o_ret, x2d, gate, wb, wo, final_g, rows_per_mod, final):
    n, d = x2d.shape
    tm = min(256, rows_per_mod, n)
    bw = 512

    def row(i):
        return (i, 0)

    out_shape = [jax.ShapeDtypeStruct((n, d), F32)] * (2 if final else 1)
    out_specs = [pl.BlockSpec((tm, d), row)] * (2 if final else 1)
    return pl.pallas_call(
        functools.partial(_outproj_kernel, final=final),
        out_shape=tuple(out_shape),
        grid=(n // tm,),
        in_specs=[pl.BlockSpec((tm, bw), row)] * 3
                 + [pl.BlockSpec((tm, bw), lambda i, c=c: (i, c // bw)) for c in (C_NZ, C_FZ, C_RZ)]
                 + [pl.BlockSpec((tm, d), lambda i, c=c: (i, C_MG // d + c)) for c in range(3)]
                 + [pl.BlockSpec((tm, d), row),
                    _mod_spec(gate, tm, rows_per_mod, d, 1),
                    pl.BlockSpec(wb.shape, lambda i: (0, 0, 0)),
                    pl.BlockSpec(wo.shape, lambda i: (0, 0)),
                    pl.BlockSpec((1, d), lambda i: (0, 0))],
        out_specs=tuple(out_specs),
        compiler_params=_params(("arbitrary",)),
        name="outproj_final" if final else "outproj",
    )(o_nsa, o_fox, o_ret, u, u, u, u, u, u, x2d, gate, wb, wo, final_g.reshape(1, d))


def _pad_cols(w, width):
    return jnp.pad(w, ((0, 0), (0, width - w.shape[1])))


def _pack_w_in(w):
    o = 0
    seg = {}
    for name, width in (("nq", 512), ("nkv", 768), ("ng", 24), ("nz", 512), ("fq", 512), ("fk", 512),
                        ("fv", 512), ("ff", 8), ("fz", 512), ("rq", 512), ("rk", 512), ("rv", 512),
                        ("rz", 512), ("mg", 3 * D_MODEL)):
        seg[name] = w[:, o:o + width]
        o += width
    order = [seg["nq"], seg["fq"], seg["rq"], seg["rk"], seg["rv"], seg["nz"], seg["fz"], seg["rz"],
             seg["fk"], seg["fv"], seg["mg"], seg["nkv"], _pad_cols(seg["ng"], LANES), _pad_cols(seg["ff"], LANES)]
    return jnp.concatenate(order, axis=1).astype(BF16)


def _pack_cmp(w1k, w1v, posk, posv):
    def parts(w):
        return jnp.stack([w[:CMP_STRIDE], w[CMP_STRIDE:]], axis=2)

    w4 = jnp.stack([parts(w1k), parts(w1k), parts(w1v), parts(w1v)])
    big = jnp.einsum("cC,csdph->scdCph", jnp.eye(4, dtype=w4.dtype), w4)
    wt = big.reshape(CMP_FEAT, 4 * 2 * CMP_HIDDEN).T.astype(BF16)

    def pos_row(part):
        sl = slice(part * CMP_STRIDE, (part + 1) * CMP_STRIDE)
        return jnp.stack([posk[sl], posk[sl], posv[sl], posv[sl]], axis=1).reshape(CMP_FEAT)

    p_pad = jnp.zeros((LANES, CMP_FEAT), F32).at[0].set(pos_row(0)).at[1].set(pos_row(1))
    return wt, p_pad


def kernel(x_prompt, x_sample, cache_nsa_cmp_kv, cache_nsa_sel_kv, cache_nsa_win_kv, cache_fox_kv, cache_fox_logf, state_ret, page_table, c_prompt, c_sample, norm_g, w_ada, b_ada, w_in, b_forget, w_cmp_k1, w_cmp_k2, pos_cmp_k, w_cmp_v1, w_cmp_v2, pos_cmp_v, w_branch, w_out, final_g):
    batch, seq, d = x_prompt.shape
    nseq, dec = x_sample.shape[:2]
    depth = norm_g.shape[0]
    n_pool = cache_nsa_cmp_kv.shape[1]
    n_pages = page_table.shape[1]
    past = n_pages * PAGE
    wb_len = cache_nsa_win_kv.shape[2]
    assert d == D_MODEL and dec <= SAMPLE_ROWS and past >= wb_len and seq % 512 == 0
    n_sel = -(-(past + dec) // SEL_BLOCK)
    blk_pad = -(-n_sel // LANES) * LANES
    srows = nseq * SAMPLE_ROWS

    hp = x_prompt.reshape(batch * seq, d)
    hs = jnp.pad(x_sample, ((0, 0), (0, SAMPLE_ROWS - dec), (0, 0))).reshape(srows, d)
    c_rows = 8
    c_all = jnp.concatenate([jnp.pad(c_prompt, ((0, c_rows - batch), (0, 0))), c_sample], axis=0)
    lg = jnp.log1p(-jnp.exp2(-5.0 - jnp.arange(RET_HEADS, dtype=F32)))
    lg_lanes = jnp.repeat(lg, HEAD_DIM).reshape(RET_HEADS // 2, 1, LANES)
    zero_state = jnp.zeros((batch, RET_HEADS, HEAD_DIM, HEAD_DIM), F32)

    outs_p, outs_s = [], []
    for l in range(depth):
        w_pad = _pack_w_in(w_in[l])
        wt, p_pad = _pack_cmp(w_cmp_k1[l], w_cmp_v1[l], pos_cmp_k[l], pos_cmp_v[l])
        w2t = jnp.stack([w_cmp_k2[l].T, w_cmp_k2[l].T, w_cmp_v2[l].T, w_cmp_v2[l].T]).astype(BF16)
        wb_bf = w_branch[l].astype(BF16)
        wo_bf = w_out[l].astype(BF16)
        bf_pad = _pad_cols(b_forget[l].reshape(1, FOX_HEADS), LANES)
        final = l == depth - 1

        mod = _adaln(c_all, w_ada[l].astype(BF16), b_ada[l])
        shift_p, scale_p, gate_p = [m.reshape(batch, 1, d) for m in jnp.split(mod[:batch], 3, axis=1)]
        shift_s, scale_s, gate_s = [jnp.repeat(m, SAMPLE_ROWS, axis=0) for m in jnp.split(mod[c_rows:], 3, axis=1)]
        cmp_bias = _cmp_bias(wt, p_pad)

        u = _inproj(hp, norm_g[l], scale_p, shift_p, w_pad, seq)
        cmp_rows = u[:, C_CMP:C_CMP + 256]
        logf, c_hd = _logf_prompt(u, bf_pad, batch, seq)
        o_fox = _fox_prompt(u, c_hd, batch, seq)
        o_ret, ret_state = _retention(u, zero_state, lg_lanes, batch, seq, min(RET_CHUNK, seq), min(RET_CHUNK, seq))
        pre_t = _cmp_pre_prompt(cmp_rows.reshape(batch, seq // CMP_STRIDE, CMP_FEAT), wt)
        kvt = _cmp_post(pre_t, cmp_bias, w2t)
        o_nsa = _nsa_prompt(u, kvt, batch, seq)
        res = _outproj(u, o_nsa, o_fox, o_ret, hp, gate_p, wb_bf, wo_bf, final_g, seq, final)
        hp = res[0]
        y_prompt = res[-1]
        win_rows = u[:, C_WIN:C_WIN + 256].reshape(batch, seq, 2, NSA_KV, HEAD_DIM)
        win_state = jnp.concatenate(
            [jnp.zeros((batch, wb_len) + win_rows.shape[2:], F32), win_rows], axis=1)[:, -wb_len:]
        outs_p.append((cmp_rows.reshape(batch, seq, 2, NSA_KV, HEAD_DIM),
                       u[:, C_SEL:C_SEL + 256].reshape(batch, seq, 2, NSA_KV, HEAD_DIM),
                       win_state,
                       u[:, C_FKV:C_FKV + 1024].reshape(batch, seq, 2, FOX_HEADS, HEAD_DIM),
                       logf, ret_state))

        us = _inproj(hs, norm_g[l], scale_s, shift_s, w_pad, srows)
        pool_cmp = jnp.transpose(cache_nsa_cmp_kv[l], (0, 2, 3, 4, 1)).reshape(n_pool, 256, PAGE)
        pool_sel = jnp.transpose(cache_nsa_sel_kv[l], (0, 2, 3, 4, 1)).reshape(n_pool, 2, LANES, PAGE)
        pool_fox = jnp.transpose(cache_fox_kv[l], (0, 2, 3, 4, 1)).reshape(n_pool, 2, FOX_HEADS * HEAD_DIM, PAGE)
        win_buf = jnp.transpose(cache_nsa_win_kv[l], (0, 2, 3, 4, 1)).reshape(nseq, 2, LANES, wb_len)
        kvt_s = _cmp_post(_cmp_pre_paged(pool_cmp, page_table, wt), cmp_bias, w2t)
        o_cmp_s, selm = _nsa_cmp_sample(us, kvt_s, nseq, past, n_sel, blk_pad)
        o_nsa_s = _nsa_selwin_sample(us, pool_sel, page_table, selm, o_cmp_s, win_buf, past)
        c_past = _logf_past(jnp.transpose(cache_fox_logf[l], (0, 2, 1)), page_table)
        o_fox_s, logf_s = _fox_sample(us, pool_fox, page_table, c_past, bf_pad)
        o_ret_s, ret_state_s = _retention(us, state_ret[l], lg_lanes, nseq, SAMPLE_ROWS, SAMPLE_ROWS, dec)
        res = _outproj(us, o_nsa_s, o_fox_s, o_ret_s, hs, gate_s, wb_bf, wo_bf, final_g, srows, final)
        hs = res[0]
        y_sample = res[-1]

        def new_rows(c0, width, shape):
            return us[:, c0:c0 + width].reshape(nseq, SAMPLE_ROWS, width)[:, :dec].reshape((nseq, dec) + shape)

        win_new = new_rows(C_WIN, 256, (2, NSA_KV, HEAD_DIM))
        win_all = jnp.concatenate([cache_nsa_win_kv[l], win_new], axis=1)
        outs_s.append((new_rows(C_CMP, 256, (2, NSA_KV, HEAD_DIM)),
                       new_rows(C_SEL, 256, (2, NSA_KV, HEAD_DIM)),
                       win_all[:, dec:],
                       new_rows(C_FKV, 1024, (2, FOX_HEADS, HEAD_DIM)),
                       logf_s.reshape(nseq, SAMPLE_ROWS, LANES)[:, :dec, :FOX_HEADS],
                       ret_state_s))

    def stacked(rows, i):
        return jnp.stack([r[i] for r in rows], axis=0)

    y_prompt = y_prompt.reshape(batch, seq, d)
    y_sample = y_sample.reshape(nseq, SAMPLE_ROWS, d)[:, :dec]
    return (y_prompt, y_sample,
            stacked(outs_p, 0), stacked(outs_s, 0), stacked(outs_p, 1), stacked(outs_s, 1),
            stacked(outs_p, 2), stacked(outs_s, 2), stacked(outs_p, 3), stacked(outs_s, 3),
            stacked(outs_p, 4), stacked(outs_s, 4), stacked(outs_p, 5), stacked(outs_s, 5))
```

```python
import functools

import jax
import jax.numpy as jnp
from jax import lax
from jax.experimental import pallas as pl
from jax.experimental.pallas import tpu as pltpu

F32 = jnp.float32
BF16 = jnp.bfloat16

D_MODEL = 1024
HEAD_DIM = 64
NSA_KV = 2
NSA_REP = 4
CMP_LEN = 32
CMP_STRIDE = 16
CMP_HIDDEN = 128
SEL_BLOCK = 64
SEL_TOPK = 16
WINDOW = 512
FOX_HEADS = 8
RET_HEADS = 8
RET_CHUNK = 128
EPS = 1e-6
NEG_INF = -1e30
FORCE = 1e9
LANES = 128
PAGE = 128
SAMPLE_ROWS = 8
VMEM_LIMIT = 48 * 1024 * 1024

C_NQ, C_FQ, C_RQ, C_RK, C_RV = 0, 512, 1024, 1536, 2048
C_NZ, C_FZ, C_RZ = 2560, 3072, 3584
C_FKV = 4096
C_MG = 5120
C_CMP, C_SEL, C_WIN = 8192, 8448, 8704
C_NG, C_FF = 8960, 9088
D_PAD = 9216
T_NQ, T_FQ, T_FV, T_SELV, T_WINV, T_NG = 0, 512, 1024, 1536, 1664, 1792
T_ROWS = 1920
T_TILE = 384
CMP_FEAT = CMP_STRIDE * 4 * HEAD_DIM


def _params(sem, vmem=VMEM_LIMIT):
    return pltpu.CompilerParams(dimension_semantics=sem, vmem_limit_bytes=vmem)


def _nt(a, b):
    return lax.dot_general(a, b, (((1,), (1,)), ((), ())), preferred_element_type=F32)


def _tn(a, b):
    return lax.dot_general(a, b, (((0,), (0,)), ((), ())), preferred_element_type=F32)


def _dot(a, b):
    return jnp.dot(a, b, preferred_element_type=F32)


def _split3(x):
    hi = x.astype(BF16)
    r1 = x - hi.astype(F32)
    mid = r1.astype(BF16)
    lo = (r1 - mid.astype(F32)).astype(BF16)
    return hi, mid, lo


def _silu(x):
    return x * jax.nn.sigmoid(x)


def _log_sigmoid(x):
    return jnp.minimum(x, 0.0) - jnp.log(1.0 + jnp.exp(-jnp.abs(x)))


def _iota(shape, dim):
    return lax.broadcasted_iota(jnp.int32, shape, dim)


def _div(x, n):
    return lax.shift_right_arithmetic(x, jnp.int32(n.bit_length() - 1))


def _mod(x, n):
    return x & (n - 1)


def _lane_tile(x, n):
    return x if n == 1 else jnp.concatenate([x] * n, axis=1)


def _adaln_kernel(c_ref, w_ref, b_ref, o_ref):
    c = c_ref[...]
    o_ref[...] = _dot(_silu(c).astype(BF16), w_ref[...]) + b_ref[...]


def _adaln(c_all, w_bf, b):
    rows, d = c_all.shape
    n = w_bf.shape[1]
    tn = 1024
    return pl.pallas_call(
        _adaln_kernel,
        out_shape=jax.ShapeDtypeStruct((rows, n), F32),
        grid=(n // tn,),
        in_specs=[pl.BlockSpec((rows, d), lambda j: (0, 0)),
                  pl.BlockSpec((d, tn), lambda j: (0, j)),
                  pl.BlockSpec((1, tn), lambda j: (0, j))],
        out_specs=pl.BlockSpec((rows, tn), lambda j: (0, j)),
        compiler_params=_params(("arbitrary",)),
        name="adaln",
    )(c_all, w_bf, b.reshape(1, n))


def _inproj_kernel(x_ref, g_ref, sc_ref, sh_ref, w_ref, o_ref, h_ref, *, transposed):
    @pl.when(pl.program_id(1) == 0)
    def _():
        x = x_ref[...]
        ms = jnp.mean(x * x, axis=-1, keepdims=True)
        y = x * lax.rsqrt(ms + EPS) * g_ref[...]
        h_ref[...] = (y * (1.0 + sc_ref[...]) + sh_ref[...]).astype(BF16)

    o_ref[...] = _nt(w_ref[...], h_ref[...]) if transposed else _dot(h_ref[...], w_ref[...])


def _mod_spec(mod, tm, rows_per_mod, d, nargs):
    if mod.ndim == 2:
        return pl.BlockSpec((tm, d), (lambda i, j: (i, 0)) if nargs == 2 else (lambda i: (i, 0)))
    per = rows_per_mod // tm
    return pl.BlockSpec((None, 1, d), (lambda i, j: (i // per, 0, 0)) if nargs == 2 else (lambda i: (i // per, 0, 0)))


def _inproj(x2d, g, scale, shift, w_pad, rows_per_mod):
    n, d = x2d.shape
    tm = min(1024, rows_per_mod, n)
    tn = 512
    return pl.pallas_call(
        functools.partial(_inproj_kernel, transposed=False),
        out_shape=jax.ShapeDtypeStruct((n, D_PAD), F32),
        grid=(n // tm, D_PAD // tn),
        in_specs=[pl.BlockSpec((tm, d), lambda i, j: (i, 0)),
                  pl.BlockSpec((1, d), lambda i, j: (0, 0)),
                  _mod_spec(scale, tm, rows_per_mod, d, 2),
                  _mod_spec(shift, tm, rows_per_mod, d, 2),
                  pl.BlockSpec((d, tn), lambda i, j: (0, j))],
        out_specs=pl.BlockSpec((tm, tn), lambda i, j: (i, j)),
        scratch_shapes=[pltpu.VMEM((tm, d), BF16)],
        compiler_params=_params(("arbitrary", "arbitrary")),
        name="inproj",
    )(x2d, g.reshape(1, d), scale, shift, w_pad)


def _inproj_t(x2d, g, scale, shift, wt_sel, rows_per_mod):
    n, d = x2d.shape
    tm = min(1024, rows_per_mod, n)
    tn = T_TILE
    return pl.pallas_call(
        functools.partial(_inproj_kernel, transposed=True),
        out_shape=jax.ShapeDtypeStruct((T_ROWS, n), F32),
        grid=(n // tm, T_ROWS // tn),
        in_specs=[pl.BlockSpec((tm, d), lambda i, j: (i, 0)),
                  pl.BlockSpec((1, d), lambda i, j: (0, 0)),
                  _mod_spec(scale, tm, rows_per_mod, d, 2),
                  _mod_spec(shift, tm, rows_per_mod, d, 2),
                  pl.BlockSpec((tn, d), lambda i, j: (j, 0))],
        out_specs=pl.BlockSpec((tn, tm), lambda i, j: (j, i)),
        scratch_shapes=[pltpu.VMEM((tm, d), BF16)],
        compiler_params=_params(("arbitrary", "arbitrary")),
        name="inproj_t",
    )(x2d, g.reshape(1, d), scale, shift, wt_sel)


def _lane_cumsum(lft, carry):
    t = lft.shape[1]
    upper = (_iota((t, t), 0) <= _iota((t, t), 1)).astype(BF16)
    c = sum(_dot(p, upper) for p in _split3(lft)) + carry
    return c, carry + jnp.sum(lft, axis=1, keepdims=True)


def _head_major_cumsum(lf, carry):
    eye = (_iota((FOX_HEADS, LANES), 0) == _iota((FOX_HEADS, LANES), 1)).astype(BF16)
    lft = sum(_nt(eye, p) for p in _split3(lf))
    return _lane_cumsum(lft, carry)


def _logf_prompt_kernel(ff_ref, bf_ref, logf_ref, c_ref, carry_ref):
    @pl.when(pl.program_id(1) == 0)
    def _():
        carry_ref[...] = jnp.zeros_like(carry_ref)

    lf = _log_sigmoid(ff_ref[...] + bf_ref[...])
    logf_ref[...] = lf[:, :FOX_HEADS]
    c, carry = _head_major_cumsum(lf, carry_ref[...])
    carry_ref[...] = carry
    pieces = [p.astype(F32) for p in _split3(c)]
    tb = c.shape[1]
    c3t = jnp.concatenate(pieces + [jnp.zeros((LANES - 3 * FOX_HEADS, tb), F32)], axis=0)
    c_ref[...] = c3t.T.astype(BF16)


def _logf_prompt(u, bf_pad, batch, seq):
    tb = min(512, seq)
    nt = seq // tb
    return pl.pallas_call(
        _logf_prompt_kernel,
        out_shape=(jax.ShapeDtypeStruct((batch, seq, FOX_HEADS), F32),
                   jax.ShapeDtypeStruct((batch * seq, LANES), BF16)),
        grid=(batch, nt),
        in_specs=[pl.BlockSpec((tb, LANES), lambda b, i: (b * nt + i, C_FF // LANES)),
                  pl.BlockSpec((1, LANES), lambda b, i: (0, 0))],
        out_specs=(pl.BlockSpec((None, tb, FOX_HEADS), lambda b, i: (b, i, 0)),
                   pl.BlockSpec((tb, LANES), lambda b, i: (b * nt + i, 0))),
        scratch_shapes=[pltpu.VMEM((FOX_HEADS, 1), F32)],
        compiler_params=_params(("arbitrary", "arbitrary")),
        name="logf_prompt",
    )(u, bf_pad)


def _logf_past_kernel(pt_ref, *refs, npg):
    pages, (c_ref, carry_ref) = refs[:npg], refs[npg:]

    @pl.when(pl.program_id(1) == 0)
    def _():
        carry_ref[...] = jnp.zeros_like(carry_ref)

    carry = carry_ref[...]
    for k in range(npg):
        c, carry = _lane_cumsum(pages[k][...], carry)
        c_ref[:, k * PAGE:(k + 1) * PAGE] = c
    carry_ref[...] = carry


def _logf_past(pool_logf_t, layer, page_table, npg=16):
    nseq, n_pages = page_table.shape
    npg = min(npg, n_pages)
    steps = n_pages // npg

    def page_spec(k):
        return pl.BlockSpec((None, None, FOX_HEADS, PAGE), lambda b, j, pt: (layer, pt[b, j * npg + k], 0, 0))

    grid_spec = pltpu.PrefetchScalarGridSpec(
        num_scalar_prefetch=1, grid=(nseq, steps),
        in_specs=[page_spec(k) for k in range(npg)],
        out_specs=pl.BlockSpec((None, FOX_HEADS, npg * PAGE), lambda b, j, pt: (b, 0, j)),
        scratch_shapes=[pltpu.VMEM((FOX_HEADS, 1), F32)])
    return pl.pallas_call(
        functools.partial(_logf_past_kernel, npg=npg),
        out_shape=jax.ShapeDtypeStruct((nseq, FOX_HEADS, n_pages * PAGE), F32),
        grid_spec=grid_spec,
        compiler_params=_params(("arbitrary", "arbitrary")),
        name="logf_past",
    )(page_table, *([pool_logf_t] * npg))


def _fox_prompt_kernel(ii_ref, jj_ref, qt_ref, k_ref, c3_ref, vt_ref, o_ref, qaug_ref, m_ref, l_ref, acc_ref):
    p = pl.program_id(2)
    i, j = ii_ref[p], jj_ref[p]
    tq, tk = qt_ref.shape[1], k_ref.shape[0]
    hp = pl.program_id(1)

    @pl.when(j == 0)
    def _():
        qt = qt_ref[...] * (HEAD_DIM ** -0.5)
        row = _iota(qt.shape, 0)
        for h in range(2):
            head = 2 * hp + h
            piece_row = (row == head) | (row == head + FOX_HEADS) | (row == head + 2 * FOX_HEADS)
            qaug_ref[h] = jnp.concatenate(
                [jnp.where((row < HEAD_DIM) == (h == 0), qt, 0.0),
                 jnp.where(piece_row, -1.0, 0.0)], axis=0).astype(BF16)
        m_ref[...] = jnp.full_like(m_ref, NEG_INF)
        l_ref[...] = jnp.zeros_like(l_ref)
        acc_ref[...] = jnp.zeros_like(acc_ref)

    def step(masked):
        kaug = jnp.concatenate([k_ref[...].astype(BF16), c3_ref[...]], axis=1)
        vt = vt_ref[...].astype(BF16)
        if masked:
            keep = _iota((tk, tq), 0) <= _iota((tk, tq), 1)
        for h in range(2):
            st = _dot(kaug, qaug_ref[h])
            if masked:
                st = jnp.where(keep, st, -jnp.inf)
            m_prev = m_ref[h]
            m_new = jnp.maximum(m_prev, jnp.max(st, axis=0, keepdims=True))
            alpha = jnp.exp(m_prev - m_new)
            pt = jnp.exp(st - m_new)
            l_ref[h] = alpha * l_ref[h] + jnp.sum(pt, axis=0, keepdims=True)
            acc_ref[h] = alpha * acc_ref[h] + _dot(vt[h * HEAD_DIM:(h + 1) * HEAD_DIM, :], pt.astype(BF16))
            m_ref[h] = m_new

    pl.when(j < i)(lambda: step(False))

    @pl.when(j == i)
    def _():
        step(True)
        ot = jnp.concatenate([acc_ref[h] / jnp.maximum(l_ref[h], 1e-30) for h in range(2)], axis=0)
        o_ref[...] = ot.T


def _fox_prompt(u, ut, c3, batch, seq):
    tq = min(512, seq)
    nq = seq // tq
    pairs = FOX_HEADS // 2
    ii = jnp.asarray([i for i in range(nq) for _ in range(i + 1)], jnp.int32)
    jj = jnp.asarray([j for i in range(nq) for j in range(i + 1)], jnp.int32)
    grid_spec = pltpu.PrefetchScalarGridSpec(
        num_scalar_prefetch=2, grid=(batch, pairs, ii.shape[0]),
        in_specs=[pl.BlockSpec((LANES, tq), lambda b, hp, p, ii, jj: (T_FQ // LANES + hp, b * nq + ii[p])),
                  pl.BlockSpec((tq, LANES), lambda b, hp, p, ii, jj: (b * nq + jj[p], C_FKV // LANES + hp)),
                  pl.BlockSpec((tq, LANES), lambda b, hp, p, ii, jj: (b * nq + jj[p], 0)),
                  pl.BlockSpec((LANES, tq), lambda b, hp, p, ii, jj: (T_FV // LANES + hp, b * nq + jj[p]))],
        out_specs=pl.BlockSpec((tq, LANES), lambda b, hp, p, ii, jj: (b * nq + ii[p], hp)),
        scratch_shapes=[pltpu.VMEM((2, 2 * LANES, tq), BF16), pltpu.VMEM((2, 1, tq), F32),
                        pltpu.VMEM((2, 1, tq), F32), pltpu.VMEM((2, HEAD_DIM, tq), F32)])
    return pl.pallas_call(
        _fox_prompt_kernel,
        out_shape=jax.ShapeDtypeStruct((batch * seq, FOX_HEADS * HEAD_DIM), F32),
        grid_spec=grid_spec,
        compiler_params=_params(("arbitrary",) * 3),
        name="fox_prompt",
    )(ii, jj, ut, u, c3, ut)


def _retention_kernel(q_ref, k_ref, v_ref, lg_ref, s0_ref, o_ref, s_ref, sbd_ref, *, c_true):
    ci = pl.program_id(2)
    c = q_ref.shape[0]
    low_row = _iota((LANES, LANES), 0) < HEAD_DIM
    low_col = _iota((LANES, LANES), 1) < HEAD_DIM

    @pl.when(ci == 0)
    def _():
        sbd_ref[...] = s0_ref[...]

    lg = lg_ref[...]
    low = _iota((c, LANES), 1) < HEAD_DIM
    pos = _iota((c, LANES), 0).astype(F32)
    q = q_ref[...]
    k = k_ref[...] * (HEAD_DIM ** -0.5)
    v = v_ref[...].astype(BF16)
    kb = k.astype(BF16)
    diff = (_iota((c, c), 0) - _iota((c, c), 1)).astype(F32)
    inner = []
    for h in range(2):
        lgh = jnp.max(jnp.where(low[:1] == (h == 0), lg, -jnp.inf), axis=1, keepdims=True)
        decay = jnp.where(diff >= 0, jnp.exp(jnp.maximum(diff, 0.0) * lgh), 0.0)
        qm = jnp.where(low == (h == 0), q, 0.0).astype(BF16)
        scores = _nt(qm, kb) * decay
        inner.append(_dot(scores.astype(BF16), v))
    sbd = sbd_ref[...]
    cross = _dot((q * jnp.exp((pos + 1.0) * lg)).astype(BF16), sbd.astype(BF16))
    o = jnp.where(low, inner[0], inner[1]) + cross

    kd = jnp.where(pos < c_true, k * jnp.exp((c_true - 1.0 - pos) * lg), 0.0).astype(BF16)
    upd = jnp.where(low_row == low_col, _tn(kd, v), 0.0)
    sbd_new = jnp.exp(c_true * lg) * sbd + upd
    sbd_ref[...] = sbd_new

    inv = 1.0 / HEAD_DIM
    s_lo = jnp.sum(jnp.where(low, o, 0.0), axis=1, keepdims=True)
    s_hi = jnp.sum(jnp.where(low, 0.0, o), axis=1, keepdims=True)
    d = o - jnp.where(low, s_lo, s_hi) * inv
    d2 = d * d
    v_lo = jnp.sum(jnp.where(low, d2, 0.0), axis=1, keepdims=True)
    v_hi = jnp.sum(jnp.where(low, 0.0, d2), axis=1, keepdims=True)
    o_ref[...] = d * lax.rsqrt(jnp.where(low, v_lo, v_hi) * inv + EPS)

    @pl.when(ci == pl.num_programs(2) - 1)
    def _():
        s_ref[...] = sbd_new


def _to_block_diag(s):
    b = s.shape[0]
    s = s.reshape(b, RET_HEADS // 2, 2, HEAD_DIM, HEAD_DIM)
    z = jnp.zeros_like(s[:, :, 0])
    return jnp.concatenate([jnp.concatenate([s[:, :, 0], z], axis=-1),
                            jnp.concatenate([z, s[:, :, 1]], axis=-1)], axis=-2)


def _from_block_diag(sbd):
    b = sbd.shape[0]
    return jnp.stack([sbd[:, :, :HEAD_DIM, :HEAD_DIM], sbd[:, :, HEAD_DIM:, HEAD_DIM:]],
                     axis=2).reshape(b, RET_HEADS, HEAD_DIM, HEAD_DIM)


def _retention(u, state0, lg_lanes, batch, rows_per_seq, chunk, c_true):
    nc = rows_per_seq // chunk
    pairs = RET_HEADS // 2
    qcol, kcol, vcol = C_RQ // LANES, C_RK // LANES, C_RV // LANES
    o, sbd = pl.pallas_call(
        functools.partial(_retention_kernel, c_true=c_true),
        out_shape=(jax.ShapeDtypeStruct((batch * rows_per_seq, RET_HEADS * HEAD_DIM), F32),
                   jax.ShapeDtypeStruct((batch, pairs, LANES, LANES), F32)),
        grid=(batch, pairs, nc),
        in_specs=[pl.BlockSpec((chunk, LANES), lambda b, hp, ci: (b * nc + ci, qcol + hp)),
                  pl.BlockSpec((chunk, LANES), lambda b, hp, ci: (b * nc + ci, kcol + hp)),
                  pl.BlockSpec((chunk, LANES), lambda b, hp, ci: (b * nc + ci, vcol + hp)),
                  pl.BlockSpec((None, 1, LANES), lambda b, hp, ci: (hp, 0, 0)),
                  pl.BlockSpec((None, None, LANES, LANES), lambda b, hp, ci: (b, hp, 0, 0))],
        out_specs=(pl.BlockSpec((chunk, LANES), lambda b, hp, ci: (b * nc + ci, hp)),
                   pl.BlockSpec((None, None, LANES, LANES), lambda b, hp, ci: (b, hp, 0, 0))),
        scratch_shapes=[pltpu.VMEM((LANES, LANES), F32)],
        compiler_params=_params(("arbitrary",) * 3),
        name="retention",
    )(u, u, u, lg_lanes, _to_block_diag(state0))
    return o, _from_block_diag(sbd)


def _cmp_bias_kernel(wt_ref, p_ref, o_ref):
    r = _nt(wt_ref[...], p_ref[...].astype(BF16))
    part = _mod(_div(_iota(r.shape, 0), CMP_HIDDEN), 2)
    col = jnp.sum(jnp.where(_iota(r.shape, 1) == part, r, 0.0), axis=1, keepdims=True)
    o_ref[...] = jnp.broadcast_to(col, o_ref.shape)


def _cmp_bias(wt, p_pad):
    return pl.pallas_call(
        _cmp_bias_kernel,
        out_shape=jax.ShapeDtypeStruct((wt.shape[0], LANES), F32),
        compiler_params=_params(None),
        name="cmp_bias",
    )(wt, p_pad)


def _cmp_pre_kernel(x_ref, wt_ref, o_ref):
    o_ref[...] = _nt(wt_ref[...], x_ref[...].astype(BF16))


def _cmp_pre_paged_kernel(pt_ref, *refs, npg):
    pages, (wt_ref, o_ref, stage_ref) = refs[:npg], refs[npg:]
    spp = PAGE // CMP_STRIDE
    feat = 4 * HEAD_DIM
    j = _iota((PAGE, PAGE), 0)
    perm = (_iota((PAGE, PAGE), 1) == _mod(j, spp) * CMP_STRIDE + _div(j, spp)).astype(BF16)
    for k in range(npg):
        xp = _nt(perm, pages[k][...].astype(BF16))
        for s in range(CMP_STRIDE):
            stage_ref[s, k * spp:(k + 1) * spp, :] = xp[s * spp:(s + 1) * spp, :]
    acc = None
    for s in range(CMP_STRIDE):
        term = _nt(wt_ref[:, s * feat:(s + 1) * feat], stage_ref[s].astype(BF16))
        acc = term if acc is None else acc + term
    o_ref[...] = acc


def _cmp_pre_prompt(x_seg, wt):
    batch, n_seg, feat = x_seg.shape
    ts = min(128, n_seg)
    return pl.pallas_call(
        _cmp_pre_kernel,
        out_shape=jax.ShapeDtypeStruct((batch, wt.shape[0], n_seg), F32),
        grid=(batch, n_seg // ts),
        in_specs=[pl.BlockSpec((None, ts, feat), lambda b, j: (b, j, 0)),
                  pl.BlockSpec(wt.shape, lambda b, j: (0, 0))],
        out_specs=pl.BlockSpec((None, wt.shape[0], ts), lambda b, j: (b, 0, j)),
        compiler_params=_params(("arbitrary", "arbitrary")),
        name="cmp_pre_prompt",
    )(x_seg, wt)


def _cmp_pre_paged(pool_t, layer, page_table, wt, npg=32):
    nseq, n_pages = page_table.shape
    npg = min(npg, n_pages)
    spp = PAGE // CMP_STRIDE
    feat = pool_t.shape[2]

    def page_spec(k):
        return pl.BlockSpec((None, None, feat, PAGE), lambda b, j, pt: (layer, pt[b, j * npg + k], 0, 0))

    grid_spec = pltpu.PrefetchScalarGridSpec(
        num_scalar_prefetch=1, grid=(nseq, n_pages // npg),
        in_specs=[page_spec(k) for k in range(npg)] + [pl.BlockSpec(wt.shape, lambda b, j, pt: (0, 0))],
        out_specs=pl.BlockSpec((None, wt.shape[0], npg * spp), lambda b, j, pt: (b, 0, j)),
        scratch_shapes=[pltpu.VMEM((CMP_STRIDE, npg * spp, feat), F32)])
    return pl.pallas_call(
        functools.partial(_cmp_pre_paged_kernel, npg=npg),
        out_shape=jax.ShapeDtypeStruct((nseq, wt.shape[0], n_pages * spp), F32),
        grid_spec=grid_spec,
        compiler_params=_params(("arbitrary", "arbitrary")),
        name="cmp_pre_paged",
    )(page_table, *([pool_t] * npg), wt)


def _cmp_post_kernel(pre_ref, bias_ref, w2t_ref, o_ref, k_ref):
    n_seg = pre_ref.shape[1]
    reps = n_seg // LANES
    for c in range(4):
        base = c * 2 * CMP_HIDDEN
        lo = pre_ref[base:base + CMP_HIDDEN, :] + _lane_tile(bias_ref[base:base + CMP_HIDDEN, :], reps)
        hi = (pre_ref[base + CMP_HIDDEN:base + 2 * CMP_HIDDEN, :]
              + _lane_tile(bias_ref[base + CMP_HIDDEN:base + 2 * CMP_HIDDEN, :], reps))
        hid = _silu(lo + pltpu.roll(hi, n_seg - 1, 1))
        o_ref[c * HEAD_DIM:(c + 1) * HEAD_DIM, :] = _dot(w2t_ref[c], hid.astype(BF16))
    k_ref[...] = o_ref[0:2 * HEAD_DIM, :].T


def _cmp_post(pre_t, bias, w2t):
    batch, rows, n_seg = pre_t.shape
    return pl.pallas_call(
        _cmp_post_kernel,
        out_shape=(jax.ShapeDtypeStruct((batch, 4 * HEAD_DIM, n_seg), F32),
                   jax.ShapeDtypeStruct((batch, n_seg, 2 * HEAD_DIM), F32)),
        grid=(batch,),
        in_specs=[pl.BlockSpec((None, rows, n_seg), lambda b: (b, 0, 0)),
                  pl.BlockSpec(bias.shape, lambda b: (0, 0)),
                  pl.BlockSpec(w2t.shape, lambda b: (0, 0, 0))],
        out_specs=(pl.BlockSpec((None, 4 * HEAD_DIM, n_seg), lambda b: (b, 0, 0)),
                   pl.BlockSpec((None, n_seg, 2 * HEAD_DIM), lambda b: (b, 0, 0))),
        compiler_params=_params(("arbitrary",)),
        name="cmp_post",
    )(pre_t, bias, w2t)


def _group_queries(q, g, rows):
    low = _iota((rows, LANES), 1) < HEAD_DIM
    out = []
    for r in range(NSA_REP):
        chunk = 2 * g + r // 2
        x = q[:, chunk * LANES:(chunk + 1) * LANES] * (HEAD_DIM ** -0.5)
        x = jnp.where(low == (r % 2 == 0), x, 0.0)
        if r % 2 != g:
            x = pltpu.roll(x, HEAD_DIM, 1)
        out.append(x)
    return jnp.concatenate(out, axis=0).astype(BF16)


def _slope_col(g, rows):
    r = _div(_iota((NSA_REP * rows, 1), 0), rows)
    return jnp.exp2(-(r + (NSA_REP * g + 1)).astype(F32))


def _softmax_rows(s, valid):
    s = jnp.where(valid, s, NEG_INF)
    m = jnp.max(s, axis=1, keepdims=True)
    p = jnp.where(valid, jnp.exp(s - m), 0.0)
    return p / jnp.maximum(jnp.sum(p, axis=1, keepdims=True), 1e-30)


def _tile_rows(x, n):
    return jnp.concatenate([x] * n, axis=0)


def _cover(n_cmp_pad, n_blk_pad):
    n = _iota((n_cmp_pad, n_blk_pad), 0) * CMP_STRIDE
    j = _iota((n_cmp_pad, n_blk_pad), 1) * SEL_BLOCK
    return ((n < j + SEL_BLOCK) & (n + CMP_LEN - 1 >= j)).astype(BF16)


def _select_blocks(imp, tpos, n_sel, axis=1):
    blk = _iota(imp.shape, axis)
    cur = _div(tpos, SEL_BLOCK)
    forced = (blk == 0) | (blk == cur) | (blk == cur - 1)
    imp = jnp.where(forced, FORCE, imp)
    imp = jnp.where(blk * SEL_BLOCK <= tpos, imp, NEG_INF)
    imp = jnp.where(blk < n_sel, imp, -jnp.inf)
    blkf = blk.astype(F32)

    def body(_, carry):
        imp, sel = carry
        m = jnp.max(imp, axis=axis, keepdims=True)
        idx = jnp.min(jnp.where(imp == m, blkf, 1e9), axis=axis, keepdims=True)
        hit = blkf == idx
        return jnp.where(hit, -jnp.inf, imp), jnp.where(hit, 1.0, sel)

    _, sel = lax.fori_loop(0, min(SEL_TOPK, n_sel), body, (imp, jnp.zeros_like(imp)))
    return sel


def _cmp_branch(qm, kct, vct, dist, valid, slope, rows):
    s = _dot(qm, kct) - slope * _tile_rows(dist, NSA_REP)
    p = _softmax_rows(s, _tile_rows(valid, NSA_REP))
    o = _nt(p.astype(BF16), vct)
    psum = p[0:rows] + p[rows:2 * rows] + p[2 * rows:3 * rows] + p[3 * rows:4 * rows]
    return o, psum


def _importance(psum, cover):
    hi = psum.astype(BF16)
    lo = (psum - hi.astype(F32)).astype(BF16)
    return _dot(hi, cover) + _dot(lo, cover)


def _gate_cols(gates, g, rows):
    lane = _iota(gates.shape, 1)
    cols = []
    for c in range(3):
        per_head = [jnp.sum(jnp.where(lane == (g * NSA_REP + r) * 3 + c, gates, 0.0), axis=1, keepdims=True)
                    for r in range(NSA_REP)]
        cols.append(jnp.concatenate(per_head, axis=0))
    return cols


def _place_group(o, g, rows):
    low = _iota((rows, LANES), 1) < HEAD_DIM
    chunks = []
    for kk in range(2):
        a = o[(2 * kk) * rows:(2 * kk + 1) * rows]
        b = o[(2 * kk + 1) * rows:(2 * kk + 2) * rows]
        if g == 1:
            a = pltpu.roll(a, HEAD_DIM, 1)
        else:
            b = pltpu.roll(b, HEAD_DIM, 1)
        chunks.append(jnp.where(low, a, b))
    return chunks


def _online_update(s, v, m_ref, l_ref, acc_ref, v_transposed=False):
    m_prev = m_ref[...]
    m_new = jnp.maximum(m_prev, jnp.max(s, axis=1, keepdims=True))
    alpha = jnp.exp(m_prev - m_new)
    p = jnp.exp(s - m_new)
    l_ref[...] = alpha * l_ref[...] + jnp.sum(p, axis=1, keepdims=True)
    pb = p.astype(BF16)
    acc_ref[...] = alpha * acc_ref[...] + (_nt(pb, v) if v_transposed else _dot(pb, v))
    m_ref[...] = m_new


def _softmax_cols(st, valid):
    st = jnp.where(valid, st, NEG_INF)
    m = jnp.max(st, axis=0, keepdims=True)
    p = jnp.where(valid, jnp.exp(st - m), 0.0)
    return p / jnp.maximum(jnp.sum(p, axis=0, keepdims=True), 1e-30)


def _online_update_t(st, vt, m_ref, l_ref, acc_ref):
    m_prev = m_ref[...]
    m_new = jnp.maximum(m_prev, jnp.max(st, axis=0, keepdims=True))
    alpha = jnp.exp(m_prev - m_new)
    pt = jnp.exp(st - m_new)
    l_ref[...] = alpha * l_ref[...] + jnp.sum(pt, axis=0, keepdims=True)
    acc_ref[...] = alpha * acc_ref[...] + _dot(vt, pt.astype(BF16))
    m_ref[...] = m_new


def _nsa_prompt_kernel(qt_ref, gt_ref, kc_ref, kvt_ref, selk_ref, selvt_ref, *rest, n_cmp, tk):
    nwin = WINDOW // LANES + 1
    wk, wvt = rest[:nwin], rest[nwin:2 * nwin]
    o_ref, selk_bf, selvt_bf, m_ref, l_ref, acc_ref = rest[2 * nwin:]
    i = pl.program_id(1)
    tq = qt_ref.shape[1]
    seq = selk_ref.shape[0]
    n_seg = kc_ref.shape[0]
    n_blk = seq // SEL_BLOCK
    blk_pad = max(LANES, n_blk)
    t0 = i * tq

    @pl.when(i == 0)
    def _():
        for c in range(seq // tk):
            selk_bf[c * tk:(c + 1) * tk, :] = selk_ref[c * tk:(c + 1) * tk, :].astype(BF16)
            selvt_bf[c] = selvt_ref[:, c * tk:(c + 1) * tk].astype(BF16)

    tpos = t0 + _iota((1, tq), 1)
    kc = kc_ref[...].astype(BF16)
    vct = kvt_ref[2 * HEAD_DIM:4 * HEAD_DIM, :].astype(BF16)
    n_col = _iota((n_seg, 1), 0)
    dist_c = (tpos - (n_col * CMP_STRIDE + CMP_LEN - 1)).astype(F32)
    dist_c = _lane_tile(dist_c, NSA_REP)
    valid_c = (dist_c >= 0) & (n_col < n_cmp)
    blk = _iota((blk_pad, n_seg), 0) * SEL_BLOCK
    seg = _iota((blk_pad, n_seg), 1) * CMP_STRIDE
    cover_t = ((seg < blk + SEL_BLOCK) & (seg + CMP_LEN - 1 >= blk)).astype(BF16)
    gates_t = jax.nn.sigmoid(gt_ref[...])
    qt = qt_ref[...] * (HEAD_DIM ** -0.5)

    kw = jnp.concatenate([w[...] for w in wk], axis=0).astype(BF16)
    vwt = jnp.concatenate([w[...] for w in wvt], axis=1).astype(BF16)
    wpos = t0 - WINDOW + _iota((kw.shape[0], 1), 0)
    dist_w = (tpos - wpos).astype(F32)
    dist_w = _lane_tile(dist_w, NSA_REP)
    valid_w = (dist_w >= 0) & (dist_w < WINDOW) & (wpos >= 0)

    n_tiles = _div(t0 + tq + tk - 1, tk)
    zero = jnp.zeros((HEAD_DIM, tq), F32)
    out_rows = []
    for g in range(NSA_KV):
        heads = [qt[(g * NSA_REP + r) * HEAD_DIM:(g * NSA_REP + r + 1) * HEAD_DIM, :] for r in range(NSA_REP)]
        qmt = jnp.concatenate([jnp.concatenate([h, zero] if g == 0 else [zero, h], axis=0) for h in heads],
                              axis=1).astype(BF16)
        slope = jnp.concatenate([jnp.full((1, tq), 2.0 ** -(NSA_REP * g + r + 1), F32) for r in range(NSA_REP)],
                                axis=1)

        pt = _softmax_cols(_dot(kc, qmt) - slope * dist_c, valid_c)
        o_cmp = _dot(vct, pt.astype(BF16))
        psum = pt[:, 0:tq] + pt[:, tq:2 * tq] + pt[:, 2 * tq:3 * tq] + pt[:, 3 * tq:4 * tq]
        hi = psum.astype(BF16)
        lo = (psum - hi.astype(F32)).astype(BF16)
        sel_t = _select_blocks(_dot(cover_t, hi) + _dot(cover_t, lo), tpos, n_blk, axis=0)
        selb = sel_t.astype(BF16)

        m_ref[...] = jnp.full_like(m_ref, NEG_INF)
        l_ref[...] = jnp.zeros_like(l_ref)
        acc_ref[...] = jnp.zeros_like(acc_ref)

        def tile(j, _):
            key = j * tk + _iota((tk, 1), 0)
            expand = (_div(key, SEL_BLOCK) == _iota((tk, blk_pad), 1)).astype(BF16)
            chosen = _dot(expand, selb)

            @pl.when(jnp.max(chosen) > 0.5)
            def _():
                r = pl.multiple_of(j * tk, tk)
                dist = (tpos - key).astype(F32)
                dist4 = _lane_tile(dist, NSA_REP)
                ok = (_lane_tile(chosen, NSA_REP) > 0.5) & (dist4 >= 0)
                st = _dot(selk_bf[pl.ds(r, tk), :], qmt) - slope * dist4
                _online_update_t(jnp.where(ok, st, -jnp.inf), selvt_bf[j], m_ref, l_ref, acc_ref)
            return 0

        lax.fori_loop(0, n_tiles, tile, 0)
        o_sel = acc_ref[...] / jnp.maximum(l_ref[...], 1e-30)

        o_win = _dot(vwt, _softmax_cols(_dot(kw, qmt) - slope * dist_w, valid_w).astype(BF16))

        def gate_row(c):
            return jnp.concatenate([gates_t[(g * NSA_REP + r) * 3 + c:(g * NSA_REP + r) * 3 + c + 1, :]
                                    for r in range(NSA_REP)], axis=1)

        o = gate_row(0) * o_cmp + gate_row(1) * o_sel + gate_row(2) * o_win
        out_rows += [o[g * HEAD_DIM:(g + 1) * HEAD_DIM, r * tq:(r + 1) * tq] for r in range(NSA_REP)]
    o_ref[...] = jnp.concatenate(out_rows, axis=0).T


def _nsa_prompt(u, ut, kc, kvt, batch, seq):
    tq = LANES
    nq = seq // tq
    tk = min(512, seq)
    n_seg = kvt.shape[2]
    n_cmp = seq // CMP_STRIDE - 1
    nwin = WINDOW // tq + 1

    def win_row(b, i, k):
        return b * nq + jnp.maximum(i - WINDOW // tq + k, 0)

    return pl.pallas_call(
        functools.partial(_nsa_prompt_kernel, n_cmp=n_cmp, tk=tk),
        out_shape=jax.ShapeDtypeStruct((batch * seq, 512), F32),
        grid=(batch, nq),
        in_specs=[pl.BlockSpec((512, tq), lambda b, i: (T_NQ // 512, b * nq + i)),
                  pl.BlockSpec((LANES, tq), lambda b, i: (T_NG // LANES, b * nq + i)),
                  pl.BlockSpec((None, n_seg, LANES), lambda b, i: (b, 0, 0)),
                  pl.BlockSpec((None, 4 * HEAD_DIM, n_seg), lambda b, i: (b, 0, 0)),
                  pl.BlockSpec((seq, LANES), lambda b, i: (b, C_SEL // LANES)),
                  pl.BlockSpec((LANES, seq), lambda b, i: (T_SELV // LANES, b))]
                 + [pl.BlockSpec((tq, LANES), lambda b, i, k=k: (win_row(b, i, k), C_WIN // LANES))
                    for k in range(nwin)]
                 + [pl.BlockSpec((LANES, tq), lambda b, i, k=k: (T_WINV // LANES, win_row(b, i, k)))
                    for k in range(nwin)],
        out_specs=pl.BlockSpec((tq, 512), lambda b, i: (b * nq + i, 0)),
        scratch_shapes=[pltpu.VMEM((seq, LANES), BF16), pltpu.VMEM((seq // tk, LANES, tk), BF16),
                        pltpu.VMEM((1, NSA_REP * tq), F32), pltpu.VMEM((1, NSA_REP * tq), F32),
                        pltpu.VMEM((LANES, NSA_REP * tq), F32)],
        compiler_params=_params(("arbitrary", "arbitrary")),
        name="nsa_prompt",
    )(ut, ut, kc, kvt, u, ut, *([u] * nwin), *([ut] * nwin))


def _nsa_cmp_sample_kernel(q_ref, kvt_ref, o_ref, sel_ref, *, past, n_cmp, n_sel):
    rows = SAMPLE_ROWS
    n_seg = kvt_ref.shape[1]
    blk_pad = sel_ref.shape[1]
    tpos = past + _iota((rows, 1), 0)
    kct = kvt_ref[0:2 * HEAD_DIM, :].astype(BF16)
    vct = kvt_ref[2 * HEAD_DIM:4 * HEAD_DIM, :].astype(BF16)
    n_idx = _iota((rows, n_seg), 1)
    dist = (tpos - (n_idx * CMP_STRIDE + CMP_LEN - 1)).astype(F32)
    valid = (dist >= 0) & (n_idx < n_cmp)
    cover = _cover(n_seg, blk_pad)
    q = q_ref[...]
    for g in range(NSA_KV):
        qm = _group_queries(q, g, rows)
        o, psum = _cmp_branch(qm, kct, vct, dist, valid, _slope_col(g, rows), rows)
        o_ref[g * NSA_REP * rows:(g + 1) * NSA_REP * rows, :] = o
        sel_ref[g * rows:(g + 1) * rows, :] = _select_blocks(_importance(psum, cover), tpos, n_sel)


def _nsa_cmp_sample(u_s, kvt, nseq, past, n_sel, blk_pad):
    n_seg = kvt.shape[2]
    return pl.pallas_call(
        functools.partial(_nsa_cmp_sample_kernel, past=past, n_cmp=n_seg - 1, n_sel=n_sel),
        out_shape=(jax.ShapeDtypeStruct((nseq, NSA_KV * NSA_REP * SAMPLE_ROWS, LANES), F32),
                   jax.ShapeDtypeStruct((nseq, NSA_KV * SAMPLE_ROWS, blk_pad), F32)),
        grid=(nseq,),
        in_specs=[pl.BlockSpec((SAMPLE_ROWS, 512), lambda b: (b, C_NQ // 512)),
                  pl.BlockSpec((None, 4 * HEAD_DIM, n_seg), lambda b: (b, 0, 0))],
        out_specs=(pl.BlockSpec((None, NSA_KV * NSA_REP * SAMPLE_ROWS, LANES), lambda b: (b, 0, 0)),
                   pl.BlockSpec((None, NSA_KV * SAMPLE_ROWS, blk_pad), lambda b: (b, 0, 0))),
        compiler_params=_params(("arbitrary",)),
        name="nsa_cmp_sample",
    )(u_s, kvt)


def _stack_groups(fn):
    return jnp.concatenate([fn(g) for g in range(NSA_KV)], axis=0)


def _nsa_selwin_sample_kernel(pt_ref, *refs, npg, past):
    pages = refs[:npg]
    (q_ref, gate_ref, selm_ref, ocmp_ref, winbuf_ref, selnew_ref, winnew_ref, o_ref,
     qm_ref, selrows_ref, m_ref, l_ref, acc_ref, pad_ref, kt_ref, vt_ref) = refs[npg:]
    j = pl.program_id(1)
    rows = SAMPLE_ROWS
    nrow = NSA_KV * NSA_REP * rows
    blk_pad = selm_ref.shape[1]
    tk = npg * PAGE
    tpos = past + _mod(_iota((nrow, 1), 0), rows)
    slope = _stack_groups(lambda g: _slope_col(g, rows))

    @pl.when(j == 0)
    def _():
        q = q_ref[...]
        qm_ref[...] = _stack_groups(lambda g: _group_queries(q, g, rows))
        selrows_ref[...] = _stack_groups(
            lambda g: _tile_rows(selm_ref[g * rows:(g + 1) * rows, :], NSA_REP)).astype(BF16)
        m_ref[...] = jnp.full_like(m_ref, NEG_INF)
        l_ref[...] = jnp.zeros_like(l_ref)
        acc_ref[...] = jnp.zeros_like(acc_ref)
        pad_ref[...] = jnp.zeros_like(pad_ref)

    def attend(score_fn, v_fn, v_transposed, kpos):
        n = kpos.shape[1]
        expand = (_iota((blk_pad, n), 0) == _div(jnp.broadcast_to(kpos, (blk_pad, n)), SEL_BLOCK)).astype(BF16)
        chosen = _dot(selrows_ref[...], expand)

        @pl.when(jnp.max(chosen) > 0.5)
        def _():
            dist = (tpos - kpos).astype(F32)
            ok = (chosen > 0.5) & (dist >= 0)
            s = jnp.where(ok, score_fn() - slope * dist, -jnp.inf)
            _online_update(s, v_fn(), m_ref, l_ref, acc_ref, v_transposed)

    for k in range(npg):
        kt_ref[:, k * PAGE:(k + 1) * PAGE] = pages[k][0].astype(BF16)
        vt_ref[:, k * PAGE:(k + 1) * PAGE] = pages[k][1].astype(BF16)
    attend(lambda: _dot(qm_ref[...], kt_ref[...]), lambda: vt_ref[...], True, j * tk + _iota((1, tk), 1))

    @pl.when(j == pl.num_programs(1) - 1)
    def _():
        pad_ref[0:rows, :] = selnew_ref[...]
        new = pad_ref[...]
        attend(lambda: _nt(qm_ref[...], new[:, 0:LANES].astype(BF16)),
               lambda: new[:, LANES:2 * LANES].astype(BF16), False, past + _iota((1, PAGE), 1))
        o_sel = acc_ref[...] / jnp.maximum(l_ref[...], 1e-30)

        pad_ref[0:rows, :] = winnew_ref[...]
        wnew = pad_ref[...]
        wb = winbuf_ref.shape[2]
        nw = wb + PAGE
        widx = _iota((1, nw), 1)
        wpos = jnp.where(widx < wb, past - wb + widx, past + widx - wb)
        dist = (tpos - wpos).astype(F32)
        valid = (dist >= 0) & (dist < WINDOW) & (wpos >= 0)
        s = jnp.concatenate([_dot(qm_ref[...], winbuf_ref[0].astype(BF16)),
                             _nt(qm_ref[...], wnew[:, 0:LANES].astype(BF16))], axis=1) - slope * dist
        p = _softmax_rows(s, valid).astype(BF16)
        o_win = _nt(p[:, 0:wb], winbuf_ref[1].astype(BF16)) + _dot(p[:, wb:], wnew[:, LANES:2 * LANES].astype(BF16))

        gates = jax.nn.sigmoid(gate_ref[...])
        o_cmp = ocmp_ref[...]
        for g in range(NSA_KV):
            gc, gs, gw = _gate_cols(gates, g, rows)
            sl = slice(g * NSA_REP * rows, (g + 1) * NSA_REP * rows)
            chunks = _place_group(gc * o_cmp[sl] + gs * o_sel[sl] + gw * o_win[sl], g, rows)
            for kk in range(2):
                o_ref[:, g * 2 * LANES + kk * LANES:g * 2 * LANES + (kk + 1) * LANES] = chunks[kk]


def _nsa_selwin_sample(u_s, pool_sel_t, layer, page_table, selm, o_cmp, win_buf_t, past, npg=16):
    nseq, n_pages = page_table.shape
    npg = min(npg, n_pages)
    blk_pad = selm.shape[2]
    nrow = NSA_KV * NSA_REP * SAMPLE_ROWS
    wb = win_buf_t.shape[4]

    def page_spec(k):
        return pl.BlockSpec((None, None, 2, LANES, PAGE), lambda b, j, pt: (layer, pt[b, j * npg + k], 0, 0, 0))

    grid_spec = pltpu.PrefetchScalarGridSpec(
        num_scalar_prefetch=1, grid=(nseq, n_pages // npg),
        in_specs=[page_spec(k) for k in range(npg)] + [
            pl.BlockSpec((SAMPLE_ROWS, 512), lambda b, j, pt: (b, C_NQ // 512)),
            pl.BlockSpec((SAMPLE_ROWS, LANES), lambda b, j, pt: (b, C_NG // LANES)),
            pl.BlockSpec((None, NSA_KV * SAMPLE_ROWS, blk_pad), lambda b, j, pt: (b, 0, 0)),
            pl.BlockSpec((None, nrow, LANES), lambda b, j, pt: (b, 0, 0)),
            pl.BlockSpec((None, None, 2, LANES, wb), lambda b, j, pt: (layer, b, 0, 0, 0)),
            pl.BlockSpec((SAMPLE_ROWS, 256), lambda b, j, pt: (b, C_SEL // 256)),
            pl.BlockSpec((SAMPLE_ROWS, 256), lambda b, j, pt: (b, C_WIN // 256))],
        out_specs=pl.BlockSpec((SAMPLE_ROWS, 512), lambda b, j, pt: (b, 0)),
        scratch_shapes=[pltpu.VMEM((nrow, LANES), BF16), pltpu.VMEM((nrow, blk_pad), BF16),
                        pltpu.VMEM((nrow, 1), F32), pltpu.VMEM((nrow, 1), F32),
                        pltpu.VMEM((nrow, LANES), F32), pltpu.VMEM((PAGE, 256), F32),
                        pltpu.VMEM((LANES, npg * PAGE), BF16), pltpu.VMEM((LANES, npg * PAGE), BF16)])
    return pl.pallas_call(
        functools.partial(_nsa_selwin_sample_kernel, npg=npg, past=past),
        out_shape=jax.ShapeDtypeStruct((nseq * SAMPLE_ROWS, 512), F32),
        grid_spec=grid_spec,
        compiler_params=_params(("arbitrary", "arbitrary")),
        name="nsa_selwin_sample",
    )(page_table, *([pool_sel_t] * npg), u_s, u_s, selm, o_cmp, win_buf_t, u_s, u_s)


def _fox_sample_kernel(pt_ref, *refs, npg):
    pages = refs[:npg]
    (q_ref, cpast_ref, clast_ref, knew_ref, vnew_ref, ff_ref, bf_ref, o_ref, logf_ref,
     qbd_ref, m_ref, l_ref, acc_ref, kpad_ref, vpad_ref, fpad_ref, kt_ref, vt_ref) = refs[npg:]
    j = pl.program_id(1)
    rows = SAMPLE_ROWS
    width = FOX_HEADS * HEAD_DIM
    nrow = FOX_HEADS * rows

    @pl.when(j == 0)
    def _():
        q = q_ref[...] * (HEAD_DIM ** -0.5)
        head = _div(_iota((rows, width), 1), HEAD_DIM)
        qbd_ref[...] = jnp.concatenate(
            [jnp.where(head == h, q, 0.0) for h in range(FOX_HEADS)], axis=0).astype(BF16)
        m_ref[...] = jnp.full_like(m_ref, NEG_INF)
        l_ref[...] = jnp.zeros_like(l_ref)
        acc_ref[...] = jnp.zeros_like(acc_ref)
        kpad_ref[...] = jnp.zeros_like(kpad_ref)
        vpad_ref[...] = jnp.zeros_like(vpad_ref)
        fpad_ref[...] = jnp.zeros_like(fpad_ref)

    def head_rows(x):
        return jnp.concatenate([jnp.broadcast_to(x[h:h + 1, :], (rows, x.shape[1])) for h in range(FOX_HEADS)], axis=0)

    c_last = jnp.max(jnp.where(_iota(clast_ref.shape, 1) == clast_ref.shape[1] - 1, clast_ref[...], -jnp.inf),
                     axis=1, keepdims=True)
    for k in range(npg):
        kt_ref[:, k * PAGE:(k + 1) * PAGE] = pages[k][0].astype(BF16)
        vt_ref[:, k * PAGE:(k + 1) * PAGE] = pages[k][1].astype(BF16)
    s = _dot(qbd_ref[...], kt_ref[...]) + head_rows(c_last - cpast_ref[...])
    _online_update(s, vt_ref[...], m_ref, l_ref, acc_ref, True)

    @pl.when(j == pl.num_programs(1) - 1)
    def _():
        lf = _log_sigmoid(ff_ref[...] + bf_ref[...])
        logf_ref[...] = lf
        fpad_ref[0:rows, :] = lf
        c_new, _ = _head_major_cumsum(fpad_ref[...], 0.0)
        kpad_ref[0:rows, :] = knew_ref[...]
        vpad_ref[0:rows, :] = vnew_ref[...]
        s = _nt(qbd_ref[...], kpad_ref[...].astype(BF16)) - head_rows(c_new)
        causal = _iota((nrow, PAGE), 1) <= _mod(_iota((nrow, PAGE), 0), rows)
        _online_update(jnp.where(causal, s, -jnp.inf), vpad_ref[...].astype(BF16), m_ref, l_ref, acc_ref)
        o = acc_ref[...] / jnp.maximum(l_ref[...], 1e-30)
        head = _div(_iota((rows, width), 1), HEAD_DIM)
        out = jnp.zeros((rows, width), F32)
        for h in range(FOX_HEADS):
            out = jnp.where(head == h, o[h * rows:(h + 1) * rows], out)
        o_ref[...] = out


def _fox_sample(u_s, pool_kv_t, layer, page_table, c_past, bf_pad, npg=8):
    nseq, n_pages = page_table.shape
    npg = min(npg, n_pages)
    width = FOX_HEADS * HEAD_DIM
    nrow = FOX_HEADS * SAMPLE_ROWS
    past = n_pages * PAGE
    tk = npg * PAGE

    def page_spec(k):
        return pl.BlockSpec((None, None, 2, width, PAGE), lambda b, j, pt: (layer, pt[b, j * npg + k], 0, 0, 0))

    grid_spec = pltpu.PrefetchScalarGridSpec(
        num_scalar_prefetch=1, grid=(nseq, n_pages // npg),
        in_specs=[page_spec(k) for k in range(npg)] + [
            pl.BlockSpec((SAMPLE_ROWS, width), lambda b, j, pt: (b, C_FQ // width)),
            pl.BlockSpec((None, FOX_HEADS, tk), lambda b, j, pt: (b, 0, j)),
            pl.BlockSpec((None, FOX_HEADS, LANES), lambda b, j, pt: (b, 0, past // LANES - 1)),
            pl.BlockSpec((SAMPLE_ROWS, width), lambda b, j, pt: (b, C_FKV // width)),
            pl.BlockSpec((SAMPLE_ROWS, width), lambda b, j, pt: (b, C_FKV // width + 1)),
            pl.BlockSpec((SAMPLE_ROWS, LANES), lambda b, j, pt: (b, C_FF // LANES)),
            pl.BlockSpec((1, LANES), lambda b, j, pt: (0, 0))],
        out_specs=(pl.BlockSpec((SAMPLE_ROWS, width), lambda b, j, pt: (b, 0)),
                   pl.BlockSpec((SAMPLE_ROWS, LANES), lambda b, j, pt: (b, 0))),
        scratch_shapes=[pltpu.VMEM((nrow, width), BF16), pltpu.VMEM((nrow, 1), F32), pltpu.VMEM((nrow, 1), F32),
                        pltpu.VMEM((nrow, width), F32), pltpu.VMEM((PAGE, width), F32),
                        pltpu.VMEM((PAGE, width), F32), pltpu.VMEM((PAGE, LANES), F32),
                        pltpu.VMEM((width, tk), BF16), pltpu.VMEM((width, tk), BF16)])
    return pl.pallas_call(
        functools.partial(_fox_sample_kernel, npg=npg),
        out_shape=(jax.ShapeDtypeStruct((nseq * SAMPLE_ROWS, width), F32),
                   jax.ShapeDtypeStruct((nseq * SAMPLE_ROWS, LANES), F32)),
        grid_spec=grid_spec,
        compiler_params=_params(("arbitrary", "arbitrary")),
        name="fox_sample",
    )(page_table, *([pool_kv_t] * npg), u_s, c_past, c_past, u_s, u_s, u_s, bf_pad)


def _outproj_kernel(on_ref, of_ref, or_ref, nz_ref, fz_ref, rz_ref, g0_ref, g1_ref, g2_ref, x_ref, gate_ref,
                    wb_ref, wo_ref, fg_ref, *out_refs, final):
    merged = None
    for n, (o_ref, z_ref, g_ref) in enumerate(((on_ref, nz_ref, g0_ref), (of_ref, fz_ref, g1_ref),
                                               (or_ref, rz_ref, g2_ref))):
        br = (o_ref[...] * _silu(z_ref[...])).astype(BF16)
        term = jax.nn.sigmoid(g_ref[...]) * _dot(br, wb_ref[n])
        merged = term if merged is None else merged + term
    y = _dot(merged.astype(BF16), wo_ref[...])
    x = x_ref[...] + gate_ref[...] * y
    out_refs[0][...] = x
    if final:
        ms = jnp.mean(x * x, axis=-1, keepdims=True)
        out_refs[1][...] = x * lax.rsqrt(ms + EPS) * fg_ref[...]


def _outproj(u, o_nsa, o_fox, o_ret, x2d, gate, wb, wo, final_g, rows_per_mod, final):
    n, d = x2d.shape
    tm = min(256, rows_per_mod, n)
    bw = 512

    def row(i):
        return (i, 0)

    out_shape = [jax.ShapeDtypeStruct((n, d), F32)] * (2 if final else 1)
    out_specs = [pl.BlockSpec((tm, d), row)] * (2 if final else 1)
    return pl.pallas_call(
        functools.partial(_outproj_kernel, final=final),
        out_shape=tuple(out_shape),
        grid=(n // tm,),
        in_specs=[pl.BlockSpec((tm, bw), row)] * 3
                 + [pl.BlockSpec((tm, bw), lambda i, c=c: (i, c // bw)) for c in (C_NZ, C_FZ, C_RZ)]
                 + [pl.BlockSpec((tm, d), lambda i, c=c: (i, C_MG // d + c)) for c in range(3)]
                 + [pl.BlockSpec((tm, d), row),
                    _mod_spec(gate, tm, rows_per_mod, d, 1),
                    pl.BlockSpec(wb.shape, lambda i: (0, 0, 0)),
                    pl.BlockSpec(wo.shape, lambda i: (0, 0)),
                    pl.BlockSpec((1, d), lambda i: (0, 0))],
        out_specs=tuple(out_specs),
        compiler_params=_params(("arbitrary",)),
        name="outproj_final" if final else "outproj",
    )(o_nsa, o_fox, o_ret, u, u, u, u, u, u, x2d, gate, wb, wo, final_g.reshape(1, d))


def _pad_cols(w, width):
    return jnp.pad(w, ((0, 0), (0, width - w.shape[1])))


def _pack_w_in(w):
    o = 0
    seg = {}
    for name, width in (("nq", 512), ("nkv", 768), ("ng", 24), ("nz", 512), ("fq", 512), ("fk", 512),
                        ("fv", 512), ("ff", 8), ("fz", 512), ("rq", 512), ("rk", 512), ("rv", 512),
                        ("rz", 512), ("mg", 3 * D_MODEL)):
        seg[name] = w[:, o:o + width]
        o += width
    order = [seg["nq"], seg["fq"], seg["rq"], seg["rk"], seg["rv"], seg["nz"], seg["fz"], seg["rz"],
             seg["fk"], seg["fv"], seg["mg"], seg["nkv"], _pad_cols(seg["ng"], LANES), _pad_cols(seg["ff"], LANES)]
    w_pad = jnp.concatenate(order, axis=1).astype(BF16)
    nkv = seg["nkv"]
    t_order = [seg["nq"], seg["fq"], seg["fv"], nkv[:, 256 + LANES:512], nkv[:, 512 + LANES:768],
               _pad_cols(seg["ng"], LANES)]
    wt_sel = jnp.concatenate(t_order, axis=1).T.astype(BF16)
    return w_pad, wt_sel


def _pack_cmp(w1k, w1v, posk, posv):
    def parts(w):
        return jnp.stack([w[:CMP_STRIDE], w[CMP_STRIDE:]], axis=2)

    w4 = jnp.stack([parts(w1k), parts(w1k), parts(w1v), parts(w1v)])
    big = jnp.einsum("cC,csdph->scdCph", jnp.eye(4, dtype=w4.dtype), w4)
    wt = big.reshape(CMP_FEAT, 4 * 2 * CMP_HIDDEN).T.astype(BF16)

    def pos_row(part):
        sl = slice(part * CMP_STRIDE, (part + 1) * CMP_STRIDE)
        return jnp.stack([posk[sl], posk[sl], posv[sl], posv[sl]], axis=1).reshape(CMP_FEAT)

    p_pad = jnp.zeros((LANES, CMP_FEAT), F32).at[0].set(pos_row(0)).at[1].set(pos_row(1))
    return wt, p_pad


def kernel(x_prompt, x_sample, cache_nsa_cmp_kv, cache_nsa_sel_kv, cache_nsa_win_kv, cache_fox_kv, cache_fox_logf, state_ret, page_table, c_prompt, c_sample, norm_g, w_ada, b_ada, w_in, b_forget, w_cmp_k1, w_cmp_k2, pos_cmp_k, w_cmp_v1, w_cmp_v2, pos_cmp_v, w_branch, w_out, final_g):
    batch, seq, d = x_prompt.shape
    nseq, dec = x_sample.shape[:2]
    depth = norm_g.shape[0]
    n_pool = cache_nsa_cmp_kv.shape[1]
    n_pages = page_table.shape[1]
    past = n_pages * PAGE
    wb_len = cache_nsa_win_kv.shape[2]
    assert d == D_MODEL and dec <= SAMPLE_ROWS and past >= wb_len and seq % 512 == 0
    n_sel = -(-(past + dec) // SEL_BLOCK)
    blk_pad = -(-n_sel // LANES) * LANES
    srows = nseq * SAMPLE_ROWS

    hp = x_prompt.reshape(batch * seq, d)
    hs = jnp.pad(x_sample, ((0, 0), (0, SAMPLE_ROWS - dec), (0, 0))).reshape(srows, d)
    c_rows = 8
    c_all = jnp.concatenate([jnp.pad(c_prompt, ((0, c_rows - batch), (0, 0))), c_sample], axis=0)
    lg = jnp.log1p(-jnp.exp2(-5.0 - jnp.arange(RET_HEADS, dtype=F32)))
    lg_lanes = jnp.repeat(lg, HEAD_DIM).reshape(RET_HEADS // 2, 1, LANES)
    zero_state = jnp.zeros((batch, RET_HEADS, HEAD_DIM, HEAD_DIM), F32)
    pool_cmp = jnp.transpose(cache_nsa_cmp_kv, (0, 1, 3, 4, 5, 2)).reshape(depth, n_pool, 256, PAGE)
    pool_sel = jnp.transpose(cache_nsa_sel_kv, (0, 1, 3, 4, 5, 2)).reshape(depth, n_pool, 2, LANES, PAGE)
    pool_fox = jnp.transpose(cache_fox_kv, (0, 1, 3, 4, 5, 2)).reshape(depth, n_pool, 2, FOX_HEADS * HEAD_DIM, PAGE)
    pool_logf = jnp.transpose(cache_fox_logf, (0, 1, 3, 2))
    win_buf = jnp.transpose(cache_nsa_win_kv, (0, 1, 3, 4, 5, 2)).reshape(depth, nseq, 2, LANES, wb_len)

    outs_p, outs_s = [], []
    for l in range(depth):
        w_pad, wt_sel = _pack_w_in(w_in[l])
        wt, p_pad = _pack_cmp(w_cmp_k1[l], w_cmp_v1[l], pos_cmp_k[l], pos_cmp_v[l])
        w2t = jnp.stack([w_cmp_k2[l].T, w_cmp_k2[l].T, w_cmp_v2[l].T, w_cmp_v2[l].T]).astype(BF16)
        wb_bf = w_branch[l].astype(BF16)
        wo_bf = w_out[l].astype(BF16)
        bf_pad = _pad_cols(b_forget[l].reshape(1, FOX_HEADS), LANES)
        final = l == depth - 1

        mod = _adaln(c_all, w_ada[l].astype(BF16), b_ada[l])
        shift_p, scale_p, gate_p = [m.reshape(batch, 1, d) for m in jnp.split(mod[:batch], 3, axis=1)]
        shift_s, scale_s, gate_s = [jnp.repeat(m, SAMPLE_ROWS, axis=0) for m in jnp.split(mod[c_rows:], 3, axis=1)]
        cmp_bias = _cmp_bias(wt, p_pad)

        u = _inproj(hp, norm_g[l], scale_p, shift_p, w_pad, seq)
        ut = _inproj_t(hp, norm_g[l], scale_p, shift_p, wt_sel, seq)
        cmp_rows = u[:, C_CMP:C_CMP + 256]
        logf, c3 = _logf_prompt(u, bf_pad, batch, seq)
        o_fox = _fox_prompt(u, ut, c3, batch, seq)
        o_ret, ret_state = _retention(u, zero_state, lg_lanes, batch, seq, min(RET_CHUNK, seq), min(RET_CHUNK, seq))
        pre_t = _cmp_pre_prompt(cmp_rows.reshape(batch, seq // CMP_STRIDE, CMP_FEAT), wt)
        kvt, kc = _cmp_post(pre_t, cmp_bias, w2t)
        o_nsa = _nsa_prompt(u, ut, kc, kvt, batch, seq)
        res = _outproj(u, o_nsa, o_fox, o_ret, hp, gate_p, wb_bf, wo_bf, final_g, seq, final)
        hp = res[0]
        y_prompt = res[-1]
        win_rows = u[:, C_WIN:C_WIN + 256].reshape(batch, seq, 2, NSA_KV, HEAD_DIM)
        win_state = jnp.concatenate(
            [jnp.zeros((batch, wb_len) + win_rows.shape[2:], F32), win_rows], axis=1)[:, -wb_len:]
        outs_p.append((cmp_rows.reshape(batch, seq, 2, NSA_KV, HEAD_DIM),
                       u[:, C_SEL:C_SEL + 256].reshape(batch, seq, 2, NSA_KV, HEAD_DIM),
                       win_state,
                       u[:, C_FKV:C_FKV + 1024].reshape(batch, seq, 2, FOX_HEADS, HEAD_DIM),
                       logf, ret_state))

        us = _inproj(hs, norm_g[l], scale_s, shift_s, w_pad, srows)
        kvt_s, _ = _cmp_post(_cmp_pre_paged(pool_cmp, l, page_table, wt), cmp_bias, w2t)
        o_cmp_s, selm = _nsa_cmp_sample(us, kvt_s, nseq, past, n_sel, blk_pad)
        o_nsa_s = _nsa_selwin_sample(us, pool_sel, l, page_table, selm, o_cmp_s, win_buf, past)
        c_past = _logf_past(pool_logf, l, page_table)
        o_fox_s, logf_s = _fox_sample(us, pool_fox, l, page_table, c_past, bf_pad)
        o_ret_s, ret_state_s = _retention(us, state_ret[l], lg_lanes, nseq, SAMPLE_ROWS, SAMPLE_ROWS, dec)
        res = _outproj(us, o_nsa_s, o_fox_s, o_ret_s, hs, gate_s, wb_bf, wo_bf, final_g, srows, final)
        hs = res[0]
        y_sample = res[-1]

        def new_rows(c0, width, shape):
            return us[:, c0:c0 + width].reshape(nseq, SAMPLE_ROWS, width)[:, :dec].reshape((nseq, dec) + shape)

        win_new = new_rows(C_WIN, 256, (2, NSA_KV, HEAD_DIM))
        win_all = jnp.concatenate([cache_nsa_win_kv[l], win_new], axis=1)
        outs_s.append((new_rows(C_CMP, 256, (2, NSA_KV, HEAD_DIM)),
                       new_rows(C_SEL, 256, (2, NSA_KV, HEAD_DIM)),
                       win_all[:, dec:],
                       new_rows(C_FKV, 1024, (2, FOX_HEADS, HEAD_DIM)),
                       logf_s.reshape(nseq, SAMPLE_ROWS, LANES)[:, :dec, :FOX_HEADS],
                       ret_state_s))

    def stacked(rows, i):
        return jnp.stack([r[i] for r in rows], axis=0)

    y_prompt = y_prompt.reshape(batch, seq, d)
    y_sample = y_sample.reshape(nseq, SAMPLE_ROWS, d)[:, :dec]
    return (y_prompt, y_sample,
            stacked(outs_p, 0), stacked(outs_s, 0), stacked(outs_p, 1), stacked(outs_s, 1),
            stacked(outs_p, 2), stacked(outs_s, 2), stacked(outs_p, 3), stacked(outs_s, 3),
            stacked(outs_p, 4), stacked(outs_s, 4), stacked(outs_p, 5), stacked(outs_s, 5))
```

```python
import functools

import jax
import jax.numpy as jnp
from jax import lax
from jax.experimental import pallas as pl
from jax.experimental.pallas import tpu as pltpu

F32 = jnp.float32
BF16 = jnp.bfloat16

D_MODEL = 1024
HEAD_DIM = 64
NSA_KV = 2
NSA_REP = 4
CMP_LEN = 32
CMP_STRIDE = 16
CMP_HIDDEN = 128
SEL_BLOCK = 64
SEL_TOPK = 16
WINDOW = 512
FOX_HEADS = 8
RET_HEADS = 8
RET_CHUNK = 128
EPS = 1e-6
NEG_INF = -1e30
FORCE = 1e9
LANES = 128
PAGE = 128
SAMPLE_ROWS = 8
VMEM_LIMIT = 48 * 1024 * 1024

C_NQ, C_FQ, C_RQ, C_RK, C_RV = 0, 512, 1024, 1536, 2048
C_NZ, C_FZ, C_RZ = 2560, 3072, 3584
C_FKV = 4096
C_MG = 5120
C_CMP, C_SEL, C_WIN = 8192, 8448, 8704
C_NG, C_FF = 8960, 9088
D_PAD = 9216
T_NQ, T_FQ, T_FV, T_SELV, T_WINV, T_NG = 0, 512, 1024, 1536, 1664, 1792
T_ROWS = 1920
T_TILE = 384
CMP_FEAT = CMP_STRIDE * 4 * HEAD_DIM


def _params(sem, vmem=VMEM_LIMIT):
    return pltpu.CompilerParams(dimension_semantics=sem, vmem_limit_bytes=vmem)


def _nt(a, b):
    return lax.dot_general(a, b, (((1,), (1,)), ((), ())), preferred_element_type=F32)


def _tn(a, b):
    return lax.dot_general(a, b, (((0,), (0,)), ((), ())), preferred_element_type=F32)


def _dot(a, b):
    return jnp.dot(a, b, preferred_element_type=F32)


def _split3(x):
    hi = x.astype(BF16)
    r1 = x - hi.astype(F32)
    mid = r1.astype(BF16)
    lo = (r1 - mid.astype(F32)).astype(BF16)
    return hi, mid, lo


def _silu(x):
    return x * jax.nn.sigmoid(x)


def _log_sigmoid(x):
    return jnp.minimum(x, 0.0) - jnp.log(1.0 + jnp.exp(-jnp.abs(x)))


def _iota(shape, dim):
    return lax.broadcasted_iota(jnp.int32, shape, dim)


def _div(x, n):
    return lax.shift_right_arithmetic(x, jnp.int32(n.bit_length() - 1))


def _mod(x, n):
    return x & (n - 1)


def _lane_tile(x, n):
    return x if n == 1 else jnp.concatenate([x] * n, axis=1)


def _adaln_kernel(c_ref, w_ref, b_ref, o_ref):
    c = c_ref[...]
    o_ref[...] = _dot(_silu(c).astype(BF16), w_ref[...]) + b_ref[...]


def _adaln(c_all, w_bf, b):
    rows, d = c_all.shape
    n = w_bf.shape[1]
    tn = 1024
    return pl.pallas_call(
        _adaln_kernel,
        out_shape=jax.ShapeDtypeStruct((rows, n), F32),
        grid=(n // tn,),
        in_specs=[pl.BlockSpec((rows, d), lambda j: (0, 0)),
                  pl.BlockSpec((d, tn), lambda j: (0, j)),
                  pl.BlockSpec((1, tn), lambda j: (0, j))],
        out_specs=pl.BlockSpec((rows, tn), lambda j: (0, j)),
        compiler_params=_params(("arbitrary",)),
        name="adaln",
    )(c_all, w_bf, b.reshape(1, n))


def _inproj_kernel(x_ref, g_ref, sc_ref, sh_ref, w_ref, o_ref, h_ref, *, transposed):
    @pl.when(pl.program_id(1) == 0)
    def _():
        x = x_ref[...]
        ms = jnp.mean(x * x, axis=-1, keepdims=True)
        y = x * lax.rsqrt(ms + EPS) * g_ref[...]
        h_ref[...] = (y * (1.0 + sc_ref[...]) + sh_ref[...]).astype(BF16)

    o_ref[...] = _nt(w_ref[...], h_ref[...]) if transposed else _dot(h_ref[...], w_ref[...])


def _mod_spec(mod, tm, rows_per_mod, d, nargs):
    if mod.ndim == 2:
        return pl.BlockSpec((tm, d), (lambda i, j: (i, 0)) if nargs == 2 else (lambda i: (i, 0)))
    per = rows_per_mod // tm
    return pl.BlockSpec((None, 1, d), (lambda i, j: (i // per, 0, 0)) if nargs == 2 else (lambda i: (i // per, 0, 0)))


def _inproj(x2d, g, scale, shift, w_pad, rows_per_mod):
    n, d = x2d.shape
    tm = min(1024, rows_per_mod, n)
    tn = 512
    return pl.pallas_call(
        functools.partial(_inproj_kernel, transposed=False),
        out_shape=jax.ShapeDtypeStruct((n, D_PAD), F32),
        grid=(n // tm, D_PAD // tn),
        in_specs=[pl.BlockSpec((tm, d), lambda i, j: (i, 0)),
                  pl.BlockSpec((1, d), lambda i, j: (0, 0)),
                  _mod_spec(scale, tm, rows_per_mod, d, 2),
                  _mod_spec(shift, tm, rows_per_mod, d, 2),
                  pl.BlockSpec((d, tn), lambda i, j: (0, j))],
        out_specs=pl.BlockSpec((tm, tn), lambda i, j: (i, j)),
        scratch_shapes=[pltpu.VMEM((tm, d), BF16)],
        compiler_params=_params(("arbitrary", "arbitrary")),
        name="inproj",
    )(x2d, g.reshape(1, d), scale, shift, w_pad)


def _inproj_t(x2d, g, scale, shift, wt_sel, rows_per_mod):
    n, d = x2d.shape
    tm = min(1024, rows_per_mod, n)
    tn = T_TILE
    return pl.pallas_call(
        functools.partial(_inproj_kernel, transposed=True),
        out_shape=jax.ShapeDtypeStruct((T_ROWS, n), F32),
        grid=(n // tm, T_ROWS // tn),
        in_specs=[pl.BlockSpec((tm, d), lambda i, j: (i, 0)),
                  pl.BlockSpec((1, d), lambda i, j: (0, 0)),
                  _mod_spec(scale, tm, rows_per_mod, d, 2),
                  _mod_spec(shift, tm, rows_per_mod, d, 2),
                  pl.BlockSpec((tn, d), lambda i, j: (j, 0))],
        out_specs=pl.BlockSpec((tn, tm), lambda i, j: (j, i)),
        scratch_shapes=[pltpu.VMEM((tm, d), BF16)],
        compiler_params=_params(("arbitrary", "arbitrary")),
        name="inproj_t",
    )(x2d, g.reshape(1, d), scale, shift, wt_sel)


def _lane_cumsum(lft, carry):
    t = lft.shape[1]
    upper = (_iota((t, t), 0) <= _iota((t, t), 1)).astype(BF16)
    c = sum(_dot(p, upper) for p in _split3(lft)) + carry
    return c, carry + jnp.sum(lft, axis=1, keepdims=True)


def _head_major_cumsum(lf, carry):
    eye = (_iota((FOX_HEADS, LANES), 0) == _iota((FOX_HEADS, LANES), 1)).astype(BF16)
    lft = sum(_nt(eye, p) for p in _split3(lf))
    return _lane_cumsum(lft, carry)


def _logf_prompt_kernel(ff_ref, bf_ref, logf_ref, c_ref, carry_ref):
    @pl.when(pl.program_id(1) == 0)
    def _():
        carry_ref[...] = jnp.zeros_like(carry_ref)

    lf = _log_sigmoid(ff_ref[...] + bf_ref[...])
    logf_ref[...] = lf[:, :FOX_HEADS]
    c, carry = _head_major_cumsum(lf, carry_ref[...])
    carry_ref[...] = carry
    pieces = [p.astype(F32) for p in _split3(c)]
    tb = c.shape[1]
    c3t = jnp.concatenate(pieces + [jnp.zeros((LANES - 3 * FOX_HEADS, tb), F32)], axis=0)
    c_ref[...] = c3t.T.astype(BF16)


def _logf_prompt(u, bf_pad, batch, seq):
    tb = min(512, seq)
    nt = seq // tb
    return pl.pallas_call(
        _logf_prompt_kernel,
        out_shape=(jax.ShapeDtypeStruct((batch, seq, FOX_HEADS), F32),
                   jax.ShapeDtypeStruct((batch * seq, LANES), BF16)),
        grid=(batch, nt),
        in_specs=[pl.BlockSpec((tb, LANES), lambda b, i: (b * nt + i, C_FF // LANES)),
                  pl.BlockSpec((1, LANES), lambda b, i: (0, 0))],
        out_specs=(pl.BlockSpec((None, tb, FOX_HEADS), lambda b, i: (b, i, 0)),
                   pl.BlockSpec((tb, LANES), lambda b, i: (b * nt + i, 0))),
        scratch_shapes=[pltpu.VMEM((FOX_HEADS, 1), F32)],
        compiler_params=_params(("arbitrary", "arbitrary")),
        name="logf_prompt",
    )(u, bf_pad)


def _logf_past_kernel(pt_ref, *refs, npg):
    pages, (c_ref, carry_ref) = refs[:npg], refs[npg:]

    @pl.when(pl.program_id(1) == 0)
    def _():
        carry_ref[...] = jnp.zeros_like(carry_ref)

    x = jnp.concatenate([pages[k][...] for k in range(npg)], axis=0)
    rows = npg * FOX_HEADS
    local, _ = _lane_cumsum(x, 0.0)
    tot = jnp.sum(x, axis=1, keepdims=True)
    r, c = _iota((rows, rows), 0), _iota((rows, rows), 1)
    earlier = ((_mod(c, FOX_HEADS) == _mod(r, FOX_HEADS)) & (c < r)).astype(BF16)
    tot_l = jnp.broadcast_to(tot, (rows, PAGE))
    offs = sum(_dot(earlier, p) for p in _split3(tot_l))
    carry = carry_ref[...]
    cum = local + offs + jnp.concatenate([carry] * npg, axis=0)
    for k in range(npg):
        c_ref[:, k * PAGE:(k + 1) * PAGE] = cum[k * FOX_HEADS:(k + 1) * FOX_HEADS, :]
    carry_ref[...] = carry + sum(tot[k * FOX_HEADS:(k + 1) * FOX_HEADS, :] for k in range(npg))


def _logf_past(pool_logf_t, layer, page_table, npg=16):
    nseq, n_pages = page_table.shape
    npg = min(npg, n_pages)
    steps = n_pages // npg

    def page_spec(k):
        return pl.BlockSpec((None, None, FOX_HEADS, PAGE), lambda b, j, pt: (layer, pt[b, j * npg + k], 0, 0))

    grid_spec = pltpu.PrefetchScalarGridSpec(
        num_scalar_prefetch=1, grid=(nseq, steps),
        in_specs=[page_spec(k) for k in range(npg)],
        out_specs=pl.BlockSpec((None, FOX_HEADS, npg * PAGE), lambda b, j, pt: (b, 0, j)),
        scratch_shapes=[pltpu.VMEM((FOX_HEADS, 1), F32)])
    return pl.pallas_call(
        functools.partial(_logf_past_kernel, npg=npg),
        out_shape=jax.ShapeDtypeStruct((nseq, FOX_HEADS, n_pages * PAGE), F32),
        grid_spec=grid_spec,
        compiler_params=_params(("arbitrary", "arbitrary")),
        name="logf_past",
    )(page_table, *([pool_logf_t] * npg))


def _fox_prompt_kernel(ii_ref, jj_ref, qt_ref, k_ref, c3_ref, vt_ref, o_ref, qaug_ref, m_ref, l_ref, acc_ref):
    p = pl.program_id(2)
    i, j = ii_ref[p], jj_ref[p]
    tq, tk = qt_ref.shape[1], k_ref.shape[0]
    hp = pl.program_id(1)

    @pl.when(j == 0)
    def _():
        qt = qt_ref[...] * (HEAD_DIM ** -0.5)
        row = _iota(qt.shape, 0)
        for h in range(2):
            head = 2 * hp + h
            piece_row = (row == head) | (row == head + FOX_HEADS) | (row == head + 2 * FOX_HEADS)
            qaug_ref[h] = jnp.concatenate(
                [jnp.where((row < HEAD_DIM) == (h == 0), qt, 0.0),
                 jnp.where(piece_row, -1.0, 0.0)], axis=0).astype(BF16)
        m_ref[...] = jnp.full_like(m_ref, NEG_INF)
        l_ref[...] = jnp.zeros_like(l_ref)
        acc_ref[...] = jnp.zeros_like(acc_ref)

    def step(masked):
        kaug = jnp.concatenate([k_ref[...].astype(BF16), c3_ref[...]], axis=1)
        vt = vt_ref[...].astype(BF16)
        if masked:
            keep = _iota((tk, tq), 0) <= _iota((tk, tq), 1)
        for h in range(2):
            st = _dot(kaug, qaug_ref[h])
            if masked:
                st = jnp.where(keep, st, -jnp.inf)
            m_prev = m_ref[h]
            m_new = jnp.maximum(m_prev, jnp.max(st, axis=0, keepdims=True))
            alpha = jnp.exp(m_prev - m_new)
            pt = jnp.exp(st - m_new)
            l_ref[h] = alpha * l_ref[h] + jnp.sum(pt, axis=0, keepdims=True)
            acc_ref[h] = alpha * acc_ref[h] + _dot(vt[h * HEAD_DIM:(h + 1) * HEAD_DIM, :], pt.astype(BF16))
            m_ref[h] = m_new

    pl.when(j < i)(lambda: step(False))

    @pl.when(j == i)
    def _():
        step(True)
        ot = jnp.concatenate([acc_ref[h] / jnp.maximum(l_ref[h], 1e-30) for h in range(2)], axis=0)
        o_ref[...] = ot.T


def _fox_prompt(u, ut, c3, batch, seq):
    tq = min(512, seq)
    nq = seq // tq
    pairs = FOX_HEADS // 2
    ii = jnp.asarray([i for i in range(nq) for _ in range(i + 1)], jnp.int32)
    jj = jnp.asarray([j for i in range(nq) for j in range(i + 1)], jnp.int32)
    grid_spec = pltpu.PrefetchScalarGridSpec(
        num_scalar_prefetch=2, grid=(batch, pairs, ii.shape[0]),
        in_specs=[pl.BlockSpec((LANES, tq), lambda b, hp, p, ii, jj: (T_FQ // LANES + hp, b * nq + ii[p])),
                  pl.BlockSpec((tq, LANES), lambda b, hp, p, ii, jj: (b * nq + jj[p], C_FKV // LANES + hp)),
                  pl.BlockSpec((tq, LANES), lambda b, hp, p, ii, jj: (b * nq + jj[p], 0)),
                  pl.BlockSpec((LANES, tq), lambda b, hp, p, ii, jj: (T_FV // LANES + hp, b * nq + jj[p]))],
        out_specs=pl.BlockSpec((tq, LANES), lambda b, hp, p, ii, jj: (b * nq + ii[p], hp)),
        scratch_shapes=[pltpu.VMEM((2, 2 * LANES, tq), BF16), pltpu.VMEM((2, 1, tq), F32),
                        pltpu.VMEM((2, 1, tq), F32), pltpu.VMEM((2, HEAD_DIM, tq), F32)])
    return pl.pallas_call(
        _fox_prompt_kernel,
        out_shape=jax.ShapeDtypeStruct((batch * seq, FOX_HEADS * HEAD_DIM), F32),
        grid_spec=grid_spec,
        compiler_params=_params(("arbitrary",) * 3),
        name="fox_prompt",
    )(ii, jj, ut, u, c3, ut)


def _retention_kernel(q_ref, k_ref, v_ref, lg_ref, s0_ref, o_ref, s_ref, sbd_ref, *, c_true):
    ci = pl.program_id(2)
    c = q_ref.shape[0]
    low_row = _iota((LANES, LANES), 0) < HEAD_DIM
    low_col = _iota((LANES, LANES), 1) < HEAD_DIM

    @pl.when(ci == 0)
    def _():
        sbd_ref[...] = s0_ref[...]

    lg = lg_ref[...]
    low = _iota((c, LANES), 1) < HEAD_DIM
    pos = _iota((c, LANES), 0).astype(F32)
    q = q_ref[...]
    k = k_ref[...] * (HEAD_DIM ** -0.5)
    v = v_ref[...].astype(BF16)
    kb = k.astype(BF16)
    diff = (_iota((c, c), 0) - _iota((c, c), 1)).astype(F32)
    inner = []
    for h in range(2):
        lgh = jnp.max(jnp.where(low[:1] == (h == 0), lg, -jnp.inf), axis=1, keepdims=True)
        decay = jnp.where(diff >= 0, jnp.exp(jnp.maximum(diff, 0.0) * lgh), 0.0)
        qm = jnp.where(low == (h == 0), q, 0.0).astype(BF16)
        scores = _nt(qm, kb) * decay
        inner.append(_dot(scores.astype(BF16), v))
    sbd = sbd_ref[...]
    cross = _dot((q * jnp.exp((pos + 1.0) * lg)).astype(BF16), sbd.astype(BF16))
    o = jnp.where(low, inner[0], inner[1]) + cross

    kd = jnp.where(pos < c_true, k * jnp.exp((c_true - 1.0 - pos) * lg), 0.0).astype(BF16)
    upd = jnp.where(low_row == low_col, _tn(kd, v), 0.0)
    sbd_new = jnp.exp(c_true * lg) * sbd + upd
    sbd_ref[...] = sbd_new

    inv = 1.0 / HEAD_DIM
    s_lo = jnp.sum(jnp.where(low, o, 0.0), axis=1, keepdims=True)
    s_hi = jnp.sum(jnp.where(low, 0.0, o), axis=1, keepdims=True)
    d = o - jnp.where(low, s_lo, s_hi) * inv
    d2 = d * d
    v_lo = jnp.sum(jnp.where(low, d2, 0.0), axis=1, keepdims=True)
    v_hi = jnp.sum(jnp.where(low, 0.0, d2), axis=1, keepdims=True)
    o_ref[...] = d * lax.rsqrt(jnp.where(low, v_lo, v_hi) * inv + EPS)

    @pl.when(ci == pl.num_programs(2) - 1)
    def _():
        s_ref[...] = sbd_new


def _to_block_diag(s):
    b = s.shape[0]
    s = s.reshape(b, RET_HEADS // 2, 2, HEAD_DIM, HEAD_DIM)
    z = jnp.zeros_like(s[:, :, 0])
    return jnp.concatenate([jnp.concatenate([s[:, :, 0], z], axis=-1),
                            jnp.concatenate([z, s[:, :, 1]], axis=-1)], axis=-2)


def _from_block_diag(sbd):
    b = sbd.shape[0]
    return jnp.stack([sbd[:, :, :HEAD_DIM, :HEAD_DIM], sbd[:, :, HEAD_DIM:, HEAD_DIM:]],
                     axis=2).reshape(b, RET_HEADS, HEAD_DIM, HEAD_DIM)


def _retention(u, state0, lg_lanes, batch, rows_per_seq, chunk, c_true):
    nc = rows_per_seq // chunk
    pairs = RET_HEADS // 2
    qcol, kcol, vcol = C_RQ // LANES, C_RK // LANES, C_RV // LANES
    o, sbd = pl.pallas_call(
        functools.partial(_retention_kernel, c_true=c_true),
        out_shape=(jax.ShapeDtypeStruct((batch * rows_per_seq, RET_HEADS * HEAD_DIM), F32),
                   jax.ShapeDtypeStruct((batch, pairs, LANES, LANES), F32)),
        grid=(batch, pairs, nc),
        in_specs=[pl.BlockSpec((chunk, LANES), lambda b, hp, ci: (b * nc + ci, qcol + hp)),
                  pl.BlockSpec((chunk, LANES), lambda b, hp, ci: (b * nc + ci, kcol + hp)),
                  pl.BlockSpec((chunk, LANES), lambda b, hp, ci: (b * nc + ci, vcol + hp)),
                  pl.BlockSpec((None, 1, LANES), lambda b, hp, ci: (hp, 0, 0)),
                  pl.BlockSpec((None, None, LANES, LANES), lambda b, hp, ci: (b, hp, 0, 0))],
        out_specs=(pl.BlockSpec((chunk, LANES), lambda b, hp, ci: (b * nc + ci, hp)),
                   pl.BlockSpec((None, None, LANES, LANES), lambda b, hp, ci: (b, hp, 0, 0))),
        scratch_shapes=[pltpu.VMEM((LANES, LANES), F32)],
        compiler_params=_params(("arbitrary",) * 3),
        name="retention",
    )(u, u, u, lg_lanes, _to_block_diag(state0))
    return o, _from_block_diag(sbd)


def _cmp_bias_kernel(wt_ref, p_ref, o_ref):
    r = _nt(wt_ref[...], p_ref[...].astype(BF16))
    part = _mod(_div(_iota(r.shape, 0), CMP_HIDDEN), 2)
    col = jnp.sum(jnp.where(_iota(r.shape, 1) == part, r, 0.0), axis=1, keepdims=True)
    o_ref[...] = jnp.broadcast_to(col, o_ref.shape)


def _cmp_bias(wt, p_pad):
    return pl.pallas_call(
        _cmp_bias_kernel,
        out_shape=jax.ShapeDtypeStruct((wt.shape[0], LANES), F32),
        compiler_params=_params(None),
        name="cmp_bias",
    )(wt, p_pad)


def _cmp_pre_kernel(x_ref, wt_ref, o_ref):
    o_ref[...] = _nt(wt_ref[...], x_ref[...].astype(BF16))


def _cmp_pre_paged_kernel(pt_ref, *refs, npg):
    pages, (wt_ref, o_ref, stage_ref) = refs[:npg], refs[npg:]
    spp = PAGE // CMP_STRIDE
    feat = 4 * HEAD_DIM
    j = _iota((PAGE, PAGE), 0)
    perm = (_iota((PAGE, PAGE), 1) == _mod(j, spp) * CMP_STRIDE + _div(j, spp)).astype(BF16)
    for k in range(npg):
        xp = _nt(perm, pages[k][...].astype(BF16))
        for s in range(CMP_STRIDE):
            stage_ref[s, k * spp:(k + 1) * spp, :] = xp[s * spp:(s + 1) * spp, :]
    acc = None
    for s in range(CMP_STRIDE):
        term = _nt(wt_ref[:, s * feat:(s + 1) * feat], stage_ref[s].astype(BF16))
        acc = term if acc is None else acc + term
    o_ref[...] = acc


def _cmp_pre_prompt(x_seg, wt):
    batch, n_seg, feat = x_seg.shape
    ts = min(128, n_seg)
    return pl.pallas_call(
        _cmp_pre_kernel,
        out_shape=jax.ShapeDtypeStruct((batch, wt.shape[0], n_seg), F32),
        grid=(batch, n_seg // ts),
        in_specs=[pl.BlockSpec((None, ts, feat), lambda b, j: (b, j, 0)),
                  pl.BlockSpec(wt.shape, lambda b, j: (0, 0))],
        out_specs=pl.BlockSpec((None, wt.shape[0], ts), lambda b, j: (b, 0, j)),
        compiler_params=_params(("arbitrary", "arbitrary")),
        name="cmp_pre_prompt",
    )(x_seg, wt)


def _cmp_pre_paged(pool_t, layer, page_table, wt, npg=32):
    nseq, n_pages = page_table.shape
    npg = min(npg, n_pages)
    spp = PAGE // CMP_STRIDE
    feat = pool_t.shape[2]

    def page_spec(k):
        return pl.BlockSpec((None, None, feat, PAGE), lambda b, j, pt: (layer, pt[b, j * npg + k], 0, 0))

    grid_spec = pltpu.PrefetchScalarGridSpec(
        num_scalar_prefetch=1, grid=(nseq, n_pages // npg),
        in_specs=[page_spec(k) for k in range(npg)] + [pl.BlockSpec(wt.shape, lambda b, j, pt: (0, 0))],
        out_specs=pl.BlockSpec((None, wt.shape[0], npg * spp), lambda b, j, pt: (b, 0, j)),
        scratch_shapes=[pltpu.VMEM((CMP_STRIDE, npg * spp, feat), F32)])
    return pl.pallas_call(
        functools.partial(_cmp_pre_paged_kernel, npg=npg),
        out_shape=jax.ShapeDtypeStruct((nseq, wt.shape[0], n_pages * spp), F32),
        grid_spec=grid_spec,
        compiler_params=_params(("arbitrary", "arbitrary")),
        name="cmp_pre_paged",
    )(page_table, *([pool_t] * npg), wt)


def _cmp_post_kernel(pre_ref, bias_ref, w2t_ref, o_ref, k_ref):
    n_seg = pre_ref.shape[1]
    reps = n_seg // LANES
    for c in range(4):
        base = c * 2 * CMP_HIDDEN
        lo = pre_ref[base:base + CMP_HIDDEN, :] + _lane_tile(bias_ref[base:base + CMP_HIDDEN, :], reps)
        hi = (pre_ref[base + CMP_HIDDEN:base + 2 * CMP_HIDDEN, :]
              + _lane_tile(bias_ref[base + CMP_HIDDEN:base + 2 * CMP_HIDDEN, :], reps))
        hid = _silu(lo + pltpu.roll(hi, n_seg - 1, 1))
        o_ref[c * HEAD_DIM:(c + 1) * HEAD_DIM, :] = _dot(w2t_ref[c], hid.astype(BF16))
    k_ref[...] = o_ref[0:2 * HEAD_DIM, :].T


def _cmp_post(pre_t, bias, w2t):
    batch, rows, n_seg = pre_t.shape
    return pl.pallas_call(
        _cmp_post_kernel,
        out_shape=(jax.ShapeDtypeStruct((batch, 4 * HEAD_DIM, n_seg), F32),
                   jax.ShapeDtypeStruct((batch, n_seg, 2 * HEAD_DIM), F32)),
        grid=(batch,),
        in_specs=[pl.BlockSpec((None, rows, n_seg), lambda b: (b, 0, 0)),
                  pl.BlockSpec(bias.shape, lambda b: (0, 0)),
                  pl.BlockSpec(w2t.shape, lambda b: (0, 0, 0))],
        out_specs=(pl.BlockSpec((None, 4 * HEAD_DIM, n_seg), lambda b: (b, 0, 0)),
                   pl.BlockSpec((None, n_seg, 2 * HEAD_DIM), lambda b: (b, 0, 0))),
        compiler_params=_params(("arbitrary",)),
        name="cmp_post",
    )(pre_t, bias, w2t)


def _group_queries(q, g, rows):
    low = _iota((rows, LANES), 1) < HEAD_DIM
    out = []
    for r in range(NSA_REP):
        chunk = 2 * g + r // 2
        x = q[:, chunk * LANES:(chunk + 1) * LANES] * (HEAD_DIM ** -0.5)
        x = jnp.where(low == (r % 2 == 0), x, 0.0)
        if r % 2 != g:
            x = pltpu.roll(x, HEAD_DIM, 1)
        out.append(x)
    return jnp.concatenate(out, axis=0).astype(BF16)


def _slope_col(g, rows):
    r = _div(_iota((NSA_REP * rows, 1), 0), rows)
    return jnp.exp2(-(r + (NSA_REP * g + 1)).astype(F32))


def _softmax_rows(s, valid):
    s = jnp.where(valid, s, NEG_INF)
    m = jnp.max(s, axis=1, keepdims=True)
    p = jnp.where(valid, jnp.exp(s - m), 0.0)
    return p / jnp.maximum(jnp.sum(p, axis=1, keepdims=True), 1e-30)


def _tile_rows(x, n):
    return jnp.concatenate([x] * n, axis=0)


def _cover(n_cmp_pad, n_blk_pad):
    n = _iota((n_cmp_pad, n_blk_pad), 0) * CMP_STRIDE
    j = _iota((n_cmp_pad, n_blk_pad), 1) * SEL_BLOCK
    return ((n < j + SEL_BLOCK) & (n + CMP_LEN - 1 >= j)).astype(BF16)


def _select_blocks(imp, tpos, n_sel, axis=1):
    blk = _iota(imp.shape, axis)
    cur = _div(tpos, SEL_BLOCK)
    forced = (blk == 0) | (blk == cur) | (blk == cur - 1)
    imp = jnp.where(forced, FORCE, imp)
    imp = jnp.where(blk * SEL_BLOCK <= tpos, imp, NEG_INF)
    imp = jnp.where(blk < n_sel, imp, -jnp.inf)
    blkf = blk.astype(F32)

    def body(_, carry):
        imp, sel = carry
        m = jnp.max(imp, axis=axis, keepdims=True)
        idx = jnp.min(jnp.where(imp == m, blkf, 1e9), axis=axis, keepdims=True)
        hit = blkf == idx
        return jnp.where(hit, -jnp.inf, imp), jnp.where(hit, 1.0, sel)

    _, sel = lax.fori_loop(0, min(SEL_TOPK, n_sel), body, (imp, jnp.zeros_like(imp)))
    return sel


def _cmp_branch(qm, kct, vct, dist, valid, slope, rows):
    s = _dot(qm, kct) - slope * _tile_rows(dist, NSA_REP)
    p = _softmax_rows(s, _tile_rows(valid, NSA_REP))
    o = _nt(p.astype(BF16), vct)
    psum = p[0:rows] + p[rows:2 * rows] + p[2 * rows:3 * rows] + p[3 * rows:4 * rows]
    return o, psum


def _importance(psum, cover):
    hi = psum.astype(BF16)
    lo = (psum - hi.astype(F32)).astype(BF16)
    return _dot(hi, cover) + _dot(lo, cover)


def _gate_cols(gates, g, rows):
    lane = _iota(gates.shape, 1)
    cols = []
    for c in range(3):
        per_head = [jnp.sum(jnp.where(lane == (g * NSA_REP + r) * 3 + c, gates, 0.0), axis=1, keepdims=True)
                    for r in range(NSA_REP)]
        cols.append(jnp.concatenate(per_head, axis=0))
    return cols


def _place_group(o, g, rows):
    low = _iota((rows, LANES), 1) < HEAD_DIM
    chunks = []
    for kk in range(2):
        a = o[(2 * kk) * rows:(2 * kk + 1) * rows]
        b = o[(2 * kk + 1) * rows:(2 * kk + 2) * rows]
        if g == 1:
            a = pltpu.roll(a, HEAD_DIM, 1)
        else:
            b = pltpu.roll(b, HEAD_DIM, 1)
        chunks.append(jnp.where(low, a, b))
    return chunks


def _online_update(s, v, m_ref, l_ref, acc_ref, v_transposed=False):
    m_prev = m_ref[...]
    m_new = jnp.maximum(m_prev, jnp.max(s, axis=1, keepdims=True))
    alpha = jnp.exp(m_prev - m_new)
    p = jnp.exp(s - m_new)
    l_ref[...] = alpha * l_ref[...] + jnp.sum(p, axis=1, keepdims=True)
    pb = p.astype(BF16)
    acc_ref[...] = alpha * acc_ref[...] + (_nt(pb, v) if v_transposed else _dot(pb, v))
    m_ref[...] = m_new


def _softmax_cols(st, valid):
    st = jnp.where(valid, st, NEG_INF)
    m = jnp.max(st, axis=0, keepdims=True)
    p = jnp.where(valid, jnp.exp(st - m), 0.0)
    return p / jnp.maximum(jnp.sum(p, axis=0, keepdims=True), 1e-30)


def _online_update_t(st, vt, m_ref, l_ref, acc_ref):
    m_prev = m_ref[...]
    m_new = jnp.maximum(m_prev, jnp.max(st, axis=0, keepdims=True))
    alpha = jnp.exp(m_prev - m_new)
    pt = jnp.exp(st - m_new)
    l_ref[...] = alpha * l_ref[...] + jnp.sum(pt, axis=0, keepdims=True)
    acc_ref[...] = alpha * acc_ref[...] + _dot(vt, pt.astype(BF16))
    m_ref[...] = m_new


def _nsa_prompt_kernel(qt_ref, gt_ref, kc_ref, kvt_ref, selk_ref, selvt_ref, *rest, n_cmp, tk):
    nwin = WINDOW // LANES + 1
    wk, wvt = rest[:nwin], rest[nwin:2 * nwin]
    o_ref, kaug_ref, selvt_bf, qaug_ref, selt_ref, m_ref, l_ref, acc_ref = rest[2 * nwin:]
    i = pl.program_id(1)
    tq = qt_ref.shape[1]
    seq = selk_ref.shape[0]
    n_seg = kc_ref.shape[0]
    n_blk = seq // SEL_BLOCK
    blk_pad = max(LANES, n_blk)
    bpt = tk // SEL_BLOCK
    t0 = i * tq

    @pl.when(i == 0)
    def _():
        local = _iota((tk, LANES), 0)
        lane = _iota((tk, LANES), 1)
        ext = jnp.where(lane < bpt, (_div(local, SEL_BLOCK) == lane).astype(F32),
                        jnp.where(lane == bpt, (local - _mod(local, 2)).astype(F32),
                                  jnp.where(lane == bpt + 1, _mod(local, 2).astype(F32),
                                            jnp.where(lane == bpt + 2, 1.0, 0.0)))).astype(BF16)
        for c in range(seq // tk):
            kaug_ref[c * tk:(c + 1) * tk, :] = jnp.concatenate(
                [selk_ref[c * tk:(c + 1) * tk, :].astype(BF16), ext], axis=1)
            selvt_bf[c] = selvt_ref[:, c * tk:(c + 1) * tk].astype(BF16)

    tpos = t0 + _iota((1, tq), 1)
    kc = kc_ref[...].astype(BF16)
    vct = kvt_ref[2 * HEAD_DIM:4 * HEAD_DIM, :].astype(BF16)
    n_col = _iota((n_seg, 1), 0)
    dist_c = (tpos - (n_col * CMP_STRIDE + CMP_LEN - 1)).astype(F32)
    dist_c = _lane_tile(dist_c, NSA_REP)
    valid_c = (dist_c >= 0) & (n_col < n_cmp)
    blk = _iota((blk_pad, n_seg), 0) * SEL_BLOCK
    seg = _iota((blk_pad, n_seg), 1) * CMP_STRIDE
    cover_t = ((seg < blk + SEL_BLOCK) & (seg + CMP_LEN - 1 >= blk)).astype(BF16)
    gates_t = jax.nn.sigmoid(gt_ref[...])
    qt = qt_ref[...] * (HEAD_DIM ** -0.5)

    kw = jnp.concatenate([w[...] for w in wk], axis=0).astype(BF16)
    vwt = jnp.concatenate([w[...] for w in wvt], axis=1).astype(BF16)
    wpos = t0 - WINDOW + _iota((kw.shape[0], 1), 0)
    dist_w = (tpos - wpos).astype(F32)
    dist_w = _lane_tile(dist_w, NSA_REP)
    valid_w = (dist_w >= 0) & (dist_w < WINDOW) & (wpos >= 0)

    n_tiles = _div(t0 + tq + tk - 1, tk)
    zero = jnp.zeros((HEAD_DIM, tq), F32)
    out_rows = []
    for g in range(NSA_KV):
        heads = [qt[(g * NSA_REP + r) * HEAD_DIM:(g * NSA_REP + r + 1) * HEAD_DIM, :] for r in range(NSA_REP)]
        qmt = jnp.concatenate([jnp.concatenate([h, zero] if g == 0 else [zero, h], axis=0) for h in heads],
                              axis=1).astype(BF16)
        slope = jnp.concatenate([jnp.full((1, tq), 2.0 ** -(NSA_REP * g + r + 1), F32) for r in range(NSA_REP)],
                                axis=1)

        pt = _softmax_cols(_dot(kc, qmt) - slope * dist_c, valid_c)
        o_cmp = _dot(vct, pt.astype(BF16))
        psum = pt[:, 0:tq] + pt[:, tq:2 * tq] + pt[:, 2 * tq:3 * tq] + pt[:, 3 * tq:4 * tq]
        hi = psum.astype(BF16)
        lo = (psum - hi.astype(F32)).astype(BF16)
        sel_t = _select_blocks(_dot(cover_t, hi) + _dot(cover_t, lo), tpos, n_blk, axis=0)
        selt_ref[...] = sel_t
        qaug_ref[0:LANES, :] = qmt
        qaug_ref[LANES + 16:2 * LANES, :] = jnp.zeros((LANES - 16, NSA_REP * tq), BF16)

        m_ref[...] = jnp.full_like(m_ref, -1e29)
        l_ref[...] = jnp.zeros_like(l_ref)
        acc_ref[...] = jnp.zeros_like(acc_ref)
        row8 = _iota((8, NSA_REP * tq), 0)
        tpos4 = _lane_tile(tpos, NSA_REP)

        def tile(j, _):
            picked = selt_ref[pl.ds(pl.multiple_of(j * bpt, bpt), bpt), :]

            def attend(diagonal):
                shift = (j * tk - t0).astype(F32)
                mask_rows = (_lane_tile(picked, NSA_REP) - 1.0) * 1e30
                bias_rows = jnp.where(row8 < 2, slope, jnp.where(row8 == 2, slope * shift, 0.0))
                qaug_ref[LANES:LANES + 16, :] = jnp.concatenate([mask_rows, bias_rows], axis=0).astype(BF16)
                r = pl.multiple_of(j * tk, tk)
                st = _dot(kaug_ref[pl.ds(r, tk), :], qaug_ref[...])
                if diagonal:
                    st = jnp.where(j * tk + _iota((tk, 1), 0) <= tpos4, st, -jnp.inf)
                _online_update_t(st, selvt_bf[j], m_ref, l_ref, acc_ref)

            active = jnp.max(picked) > 0.5
            pl.when(active & (j < n_tiles - 1))(lambda: attend(False))
            pl.when(active & (j == n_tiles - 1))(lambda: attend(True))
            return 0

        lax.fori_loop(0, n_tiles, tile, 0)
        o_sel = acc_ref[...] / jnp.maximum(l_ref[...], 1e-30)

        o_win = _dot(vwt, _softmax_cols(_dot(kw, qmt) - slope * dist_w, valid_w).astype(BF16))

        def gate_row(c):
            return jnp.concatenate([gates_t[(g * NSA_REP + r) * 3 + c:(g * NSA_REP + r) * 3 + c + 1, :]
                                    for r in range(NSA_REP)], axis=1)

        o = gate_row(0) * o_cmp + gate_row(1) * o_sel + gate_row(2) * o_win
        out_rows += [o[g * HEAD_DIM:(g + 1) * HEAD_DIM, r * tq:(r + 1) * tq] for r in range(NSA_REP)]
    o_ref[...] = jnp.concatenate(out_rows, axis=0).T


def _nsa_prompt(u, ut, kc, kvt, batch, seq):
    tq = LANES
    nq = seq // tq
    tk = min(512, seq)
    n_seg = kvt.shape[2]
    n_cmp = seq // CMP_STRIDE - 1
    nwin = WINDOW // tq + 1

    def win_row(b, i, k):
        return b * nq + jnp.maximum(i - WINDOW // tq + k, 0)

    return pl.pallas_call(
        functools.partial(_nsa_prompt_kernel, n_cmp=n_cmp, tk=tk),
        out_shape=jax.ShapeDtypeStruct((batch * seq, 512), F32),
        grid=(batch, nq),
        in_specs=[pl.BlockSpec((512, tq), lambda b, i: (T_NQ // 512, b * nq + i)),
                  pl.BlockSpec((LANES, tq), lambda b, i: (T_NG // LANES, b * nq + i)),
                  pl.BlockSpec((None, n_seg, LANES), lambda b, i: (b, 0, 0)),
                  pl.BlockSpec((None, 4 * HEAD_DIM, n_seg), lambda b, i: (b, 0, 0)),
                  pl.BlockSpec((seq, LANES), lambda b, i: (b, C_SEL // LANES)),
                  pl.BlockSpec((LANES, seq), lambda b, i: (T_SELV // LANES, b))]
                 + [pl.BlockSpec((tq, LANES), lambda b, i, k=k: (win_row(b, i, k), C_WIN // LANES))
                    for k in range(nwin)]
                 + [pl.BlockSpec((LANES, tq), lambda b, i, k=k: (T_WINV // LANES, win_row(b, i, k)))
                    for k in range(nwin)],
        out_specs=pl.BlockSpec((tq, 512), lambda b, i: (b * nq + i, 0)),
        scratch_shapes=[pltpu.VMEM((seq, 2 * LANES), BF16), pltpu.VMEM((seq // tk, LANES, tk), BF16),
                        pltpu.VMEM((2 * LANES, NSA_REP * tq), BF16),
                        pltpu.VMEM((max(LANES, seq // SEL_BLOCK), tq), F32),
                        pltpu.VMEM((1, NSA_REP * tq), F32), pltpu.VMEM((1, NSA_REP * tq), F32),
                        pltpu.VMEM((LANES, NSA_REP * tq), F32)],
        compiler_params=_params(("arbitrary", "arbitrary")),
        name="nsa_prompt",
    )(ut, ut, kc, kvt, u, ut, *([u] * nwin), *([ut] * nwin))


def _nsa_cmp_sample_kernel(q_ref, kvt_ref, o_ref, sel_ref, *, past, n_cmp, n_sel):
    rows = SAMPLE_ROWS
    n_seg = kvt_ref.shape[1]
    blk_pad = sel_ref.shape[1]
    tpos = past + _iota((rows, 1), 0)
    kct = kvt_ref[0:2 * HEAD_DIM, :].astype(BF16)
    vct = kvt_ref[2 * HEAD_DIM:4 * HEAD_DIM, :].astype(BF16)
    n_idx = _iota((rows, n_seg), 1)
    dist = (tpos - (n_idx * CMP_STRIDE + CMP_LEN - 1)).astype(F32)
    valid = (dist >= 0) & (n_idx < n_cmp)
    cover = _cover(n_seg, blk_pad)
    q = q_ref[...]
    for g in range(NSA_KV):
        qm = _group_queries(q, g, rows)
        o, psum = _cmp_branch(qm, kct, vct, dist, valid, _slope_col(g, rows), rows)
        o_ref[g * NSA_REP * rows:(g + 1) * NSA_REP * rows, :] = o
        sel_ref[g * rows:(g + 1) * rows, :] = _select_blocks(_importance(psum, cover), tpos, n_sel)


def _nsa_cmp_sample(u_s, kvt, nseq, past, n_sel, blk_pad):
    n_seg = kvt.shape[2]
    return pl.pallas_call(
        functools.partial(_nsa_cmp_sample_kernel, past=past, n_cmp=n_seg - 1, n_sel=n_sel),
        out_shape=(jax.ShapeDtypeStruct((nseq, NSA_KV * NSA_REP * SAMPLE_ROWS, LANES), F32),
                   jax.ShapeDtypeStruct((nseq, NSA_KV * SAMPLE_ROWS, blk_pad), F32)),
        grid=(nseq,),
        in_specs=[pl.BlockSpec((SAMPLE_ROWS, 512), lambda b: (b, C_NQ // 512)),
                  pl.BlockSpec((None, 4 * HEAD_DIM, n_seg), lambda b: (b, 0, 0))],
        out_specs=(pl.BlockSpec((None, NSA_KV * NSA_REP * SAMPLE_ROWS, LANES), lambda b: (b, 0, 0)),
                   pl.BlockSpec((None, NSA_KV * SAMPLE_ROWS, blk_pad), lambda b: (b, 0, 0))),
        compiler_params=_params(("arbitrary",)),
        name="nsa_cmp_sample",
    )(u_s, kvt)


def _stack_groups(fn):
    return jnp.concatenate([fn(g) for g in range(NSA_KV)], axis=0)


def _nsa_selwin_sample_kernel(pt_ref, *refs, npg, past):
    pages = refs[:npg]
    (q_ref, gate_ref, selm_ref, ocmp_ref, winbuf_ref, selnew_ref, winnew_ref, o_ref,
     qm_ref, selrows_ref, m_ref, l_ref, acc_ref, pad_ref, kt_ref, vt_ref) = refs[npg:]
    j = pl.program_id(1)
    rows = SAMPLE_ROWS
    nrow = NSA_KV * NSA_REP * rows
    blk_pad = selm_ref.shape[1]
    tk = npg * PAGE
    tpos = past + _mod(_iota((nrow, 1), 0), rows)
    slope = _stack_groups(lambda g: _slope_col(g, rows))

    @pl.when(j == 0)
    def _():
        q = q_ref[...]
        qm_ref[...] = _stack_groups(lambda g: _group_queries(q, g, rows))
        selrows_ref[...] = _stack_groups(
            lambda g: _tile_rows(selm_ref[g * rows:(g + 1) * rows, :], NSA_REP)).astype(BF16)
        m_ref[...] = jnp.full_like(m_ref, NEG_INF)
        l_ref[...] = jnp.zeros_like(l_ref)
        acc_ref[...] = jnp.zeros_like(acc_ref)
        pad_ref[...] = jnp.zeros_like(pad_ref)

    def attend(score_fn, v_fn, v_transposed, kpos):
        n = kpos.shape[1]
        expand = (_iota((blk_pad, n), 0) == _div(jnp.broadcast_to(kpos, (blk_pad, n)), SEL_BLOCK)).astype(BF16)
        chosen = _dot(selrows_ref[...], expand)

        @pl.when(jnp.max(chosen) > 0.5)
        def _():
            dist = (tpos - kpos).astype(F32)
            ok = (chosen > 0.5) & (dist >= 0)
            s = jnp.where(ok, score_fn() - slope * dist, -jnp.inf)
            _online_update(s, v_fn(), m_ref, l_ref, acc_ref, v_transposed)

    for k in range(npg):
        kt_ref[:, k * PAGE:(k + 1) * PAGE] = pages[k][0].astype(BF16)
        vt_ref[:, k * PAGE:(k + 1) * PAGE] = pages[k][1].astype(BF16)
    attend(lambda: _dot(qm_ref[...], kt_ref[...]), lambda: vt_ref[...], True, j * tk + _iota((1, tk), 1))

    @pl.when(j == pl.num_programs(1) - 1)
    def _():
        pad_ref[0:rows, :] = selnew_ref[...]
        new = pad_ref[...]
        attend(lambda: _nt(qm_ref[...], new[:, 0:LANES].astype(BF16)),
               lambda: new[:, LANES:2 * LANES].astype(BF16), False, past + _iota((1, PAGE), 1))
        o_sel = acc_ref[...] / jnp.maximum(l_ref[...], 1e-30)

        pad_ref[0:rows, :] = winnew_ref[...]
        wnew = pad_ref[...]
        wb = winbuf_ref.shape[2]
        nw = wb + PAGE
        widx = _iota((1, nw), 1)
        wpos = jnp.where(widx < wb, past - wb + widx, past + widx - wb)
        dist = (tpos - wpos).astype(F32)
        valid = (dist >= 0) & (dist < WINDOW) & (wpos >= 0)
        s = jnp.concatenate([_dot(qm_ref[...], winbuf_ref[0].astype(BF16)),
                             _nt(qm_ref[...], wnew[:, 0:LANES].astype(BF16))], axis=1) - slope * dist
        p = _softmax_rows(s, valid).astype(BF16)
        o_win = _nt(p[:, 0:wb], winbuf_ref[1].astype(BF16)) + _dot(p[:, wb:], wnew[:, LANES:2 * LANES].astype(BF16))

        gates = jax.nn.sigmoid(gate_ref[...])
        o_cmp = ocmp_ref[...]
        for g in range(NSA_KV):
            gc, gs, gw = _gate_cols(gates, g, rows)
            sl = slice(g * NSA_REP * rows, (g + 1) * NSA_REP * rows)
            chunks = _place_group(gc * o_cmp[sl] + gs * o_sel[sl] + gw * o_win[sl], g, rows)
            for kk in range(2):
                o_ref[:, g * 2 * LANES + kk * LANES:g * 2 * LANES + (kk + 1) * LANES] = chunks[kk]


def _nsa_selwin_sample(u_s, pool_sel_t, layer, page_table, selm, o_cmp, win_buf_t, past, npg=16):
    nseq, n_pages = page_table.shape
    npg = min(npg, n_pages)
    blk_pad = selm.shape[2]
    nrow = NSA_KV * NSA_REP * SAMPLE_ROWS
    wb = win_buf_t.shape[4]

    def page_spec(k):
        return pl.BlockSpec((None, None, 2, LANES, PAGE), lambda b, j, pt: (layer, pt[b, j * npg + k], 0, 0, 0))

    grid_spec = pltpu.PrefetchScalarGridSpec(
        num_scalar_prefetch=1, grid=(nseq, n_pages // npg),
        in_specs=[page_spec(k) for k in range(npg)] + [
            pl.BlockSpec((SAMPLE_ROWS, 512), lambda b, j, pt: (b, C_NQ // 512)),
            pl.BlockSpec((SAMPLE_ROWS, LANES), lambda b, j, pt: (b, C_NG // LANES)),
            pl.BlockSpec((None, NSA_KV * SAMPLE_ROWS, blk_pad), lambda b, j, pt: (b, 0, 0)),
            pl.BlockSpec((None, nrow, LANES), lambda b, j, pt: (b, 0, 0)),
            pl.BlockSpec((None, None, 2, LANES, wb), lambda b, j, pt: (layer, b, 0, 0, 0)),
            pl.BlockSpec((SAMPLE_ROWS, 256), lambda b, j, pt: (b, C_SEL // 256)),
            pl.BlockSpec((SAMPLE_ROWS, 256), lambda b, j, pt: (b, C_WIN // 256))],
        out_specs=pl.BlockSpec((SAMPLE_ROWS, 512), lambda b, j, pt: (b, 0)),
        scratch_shapes=[pltpu.VMEM((nrow, LANES), BF16), pltpu.VMEM((nrow, blk_pad), BF16),
                        pltpu.VMEM((nrow, 1), F32), pltpu.VMEM((nrow, 1), F32),
                        pltpu.VMEM((nrow, LANES), F32), pltpu.VMEM((PAGE, 256), F32),
                        pltpu.VMEM((LANES, npg * PAGE), BF16), pltpu.VMEM((LANES, npg * PAGE), BF16)])
    return pl.pallas_call(
        functools.partial(_nsa_selwin_sample_kernel, npg=npg, past=past),
        out_shape=jax.ShapeDtypeStruct((nseq * SAMPLE_ROWS, 512), F32),
        grid_spec=grid_spec,
        compiler_params=_params(("arbitrary", "arbitrary")),
        name="nsa_selwin_sample",
    )(page_table, *([pool_sel_t] * npg), u_s, u_s, selm, o_cmp, win_buf_t, u_s, u_s)


def _fox_sample_kernel(pt_ref, *refs, npg):
    pages = refs[:npg]
    (q_ref, cpast_ref, clast_ref, knew_ref, vnew_ref, ff_ref, bf_ref, o_ref, logf_ref,
     qbd_ref, m_ref, l_ref, acc_ref, kpad_ref, vpad_ref, fpad_ref, kt_ref, vt_ref) = refs[npg:]
    j = pl.program_id(1)
    rows = SAMPLE_ROWS
    width = FOX_HEADS * HEAD_DIM
    nrow = FOX_HEADS * rows

    @pl.when(j == 0)
    def _():
        q = q_ref[...] * (HEAD_DIM ** -0.5)
        head = _div(_iota((rows, width), 1), HEAD_DIM)
        qbd_ref[...] = jnp.concatenate(
            [jnp.where(head == h, q, 0.0) for h in range(FOX_HEADS)], axis=0).astype(BF16)
        m_ref[...] = jnp.full_like(m_ref, NEG_INF)
        l_ref[...] = jnp.zeros_like(l_ref)
        acc_ref[...] = jnp.zeros_like(acc_ref)
        kpad_ref[...] = jnp.zeros_like(kpad_ref)
        vpad_ref[...] = jnp.zeros_like(vpad_ref)
        fpad_ref[...] = jnp.zeros_like(fpad_ref)

    def head_rows(x):
        return jnp.concatenate([jnp.broadcast_to(x[h:h + 1, :], (rows, x.shape[1])) for h in range(FOX_HEADS)], axis=0)

    c_last = jnp.max(jnp.where(_iota(clast_ref.shape, 1) == clast_ref.shape[1] - 1, clast_ref[...], -jnp.inf),
                     axis=1, keepdims=True)
    for k in range(npg):
        kt_ref[:, k * PAGE:(k + 1) * PAGE] = pages[k][0].astype(BF16)
        vt_ref[:, k * PAGE:(k + 1) * PAGE] = pages[k][1].astype(BF16)
    s = _dot(qbd_ref[...], kt_ref[...]) + head_rows(c_last - cpast_ref[...])
    _online_update(s, vt_ref[...], m_ref, l_ref, acc_ref, True)

    @pl.when(j == pl.num_programs(1) - 1)
    def _():
        lf = _log_sigmoid(ff_ref[...] + bf_ref[...])
        logf_ref[...] = lf
        fpad_ref[0:rows, :] = lf
        c_new, _ = _head_major_cumsum(fpad_ref[...], 0.0)
        kpad_ref[0:rows, :] = knew_ref[...]
        vpad_ref[0:rows, :] = vnew_ref[...]
        s = _nt(qbd_ref[...], kpad_ref[...].astype(BF16)) - head_rows(c_new)
        causal = _iota((nrow, PAGE), 1) <= _mod(_iota((nrow, PAGE), 0), rows)
        _online_update(jnp.where(causal, s, -jnp.inf), vpad_ref[...].astype(BF16), m_ref, l_ref, acc_ref)
        o = acc_ref[...] / jnp.maximum(l_ref[...], 1e-30)
        head = _div(_iota((rows, width), 1), HEAD_DIM)
        out = jnp.zeros((rows, width), F32)
        for h in range(FOX_HEADS):
            out = jnp.where(head == h, o[h * rows:(h + 1) * rows], out)
        o_ref[...] = out


def _fox_sample(u_s, pool_kv_t, layer, page_table, c_past, bf_pad, npg=8):
    nseq, n_pages = page_table.shape
    npg = min(npg, n_pages)
    width = FOX_HEADS * HEAD_DIM
    nrow = FOX_HEADS * SAMPLE_ROWS
    past = n_pages * PAGE
    tk = npg * PAGE

    def page_spec(k):
        return pl.BlockSpec((None, None, 2, width, PAGE), lambda b, j, pt: (layer, pt[b, j * npg + k], 0, 0, 0))

    grid_spec = pltpu.PrefetchScalarGridSpec(
        num_scalar_prefetch=1, grid=(nseq, n_pages // npg),
        in_specs=[page_spec(k) for k in range(npg)] + [
            pl.BlockSpec((SAMPLE_ROWS, width), lambda b, j, pt: (b, C_FQ // width)),
            pl.BlockSpec((None, FOX_HEADS, tk), lambda b, j, pt: (b, 0, j)),
            pl.BlockSpec((None, FOX_HEADS, LANES), lambda b, j, pt: (b, 0, past // LANES - 1)),
            pl.BlockSpec((SAMPLE_ROWS, width), lambda b, j, pt: (b, C_FKV // width)),
            pl.BlockSpec((SAMPLE_ROWS, width), lambda b, j, pt: (b, C_FKV // width + 1)),
            pl.BlockSpec((SAMPLE_ROWS, LANES), lambda b, j, pt: (b, C_FF // LANES)),
            pl.BlockSpec((1, LANES), lambda b, j, pt: (0, 0))],
        out_specs=(pl.BlockSpec((SAMPLE_ROWS, width), lambda b, j, pt: (b, 0)),
                   pl.BlockSpec((SAMPLE_ROWS, LANES), lambda b, j, pt: (b, 0))),
        scratch_shapes=[pltpu.VMEM((nrow, width), BF16), pltpu.VMEM((nrow, 1), F32), pltpu.VMEM((nrow, 1), F32),
                        pltpu.VMEM((nrow, width), F32), pltpu.VMEM((PAGE, width), F32),
                        pltpu.VMEM((PAGE, width), F32), pltpu.VMEM((PAGE, LANES), F32),
                        pltpu.VMEM((width, tk), BF16), pltpu.VMEM((width, tk), BF16)])
    return pl.pallas_call(
        functools.partial(_fox_sample_kernel, npg=npg),
        out_shape=(jax.ShapeDtypeStruct((nseq * SAMPLE_ROWS, width), F32),
                   jax.ShapeDtypeStruct((nseq * SAMPLE_ROWS, LANES), F32)),
        grid_spec=grid_spec,
        compiler_params=_params(("arbitrary", "arbitrary")),
        name="fox_sample",
    )(page_table, *([pool_kv_t] * npg), u_s, c_past, c_past, u_s, u_s, u_s, bf_pad)


def _outproj_kernel(on_ref, of_ref, or_ref, nz_ref, fz_ref, rz_ref, g0_ref, g1_ref, g2_ref, x_ref, gate_ref,
                    wb_ref, wo_ref, fg_ref, *out_refs, final):
    merged = None
    for n, (o_ref, z_ref, g_ref) in enumerate(((on_ref, nz_ref, g0_ref), (of_ref, fz_ref, g1_ref),
                                               (or_ref, rz_ref, g2_ref))):
        br = (o_ref[...] * _silu(z_ref[...])).astype(BF16)
        term = jax.nn.sigmoid(g_ref[...]) * _dot(br, wb_ref[n])
        merged = term if merged is None else merged + term
    y = _dot(merged.astype(BF16), wo_ref[...])
    x = x_ref[...] + gate_ref[...] * y
    out_refs[0][...] = x
    if final:
        ms = jnp.mean(x * x, axis=-1, keepdims=True)
        out_refs[1][...] = x * lax.rsqrt(ms + EPS) * fg_ref[...]


def _outproj(u, o_nsa, o_fox, o_ret, x2d, gate, wb, wo, final_g, rows_per_mod, final):
    n, d = x2d.shape
    tm = min(256, rows_per_mod, n)
    bw = 512

    def row(i):
        return (i, 0)

    out_shape = [jax.ShapeDtypeStruct((n, d), F32)] * (2 if final else 1)
    out_specs = [pl.BlockSpec((tm, d), row)] * (2 if final else 1)
    return pl.pallas_call(
        functools.partial(_outproj_kernel, final=final),
        out_shape=tuple(out_shape),
        grid=(n // tm,),
        in_specs=[pl.BlockSpec((tm, bw), row)] * 3
                 + [pl.BlockSpec((tm, bw), lambda i, c=c: (i, c // bw)) for c in (C_NZ, C_FZ, C_RZ)]
                 + [pl.BlockSpec((tm, d), lambda i, c=c: (i, C_MG // d + c)) for c in range(3)]
                 + [pl.BlockSpec((tm, d), row),
                    _mod_spec(gate, tm, rows_per_mod, d, 1),
                    pl.BlockSpec(wb.shape, lambda i: (0, 0, 0)),
                    pl.BlockSpec(wo.shape, lambda i: (0, 0)),
                    pl.BlockSpec((1, d), lambda i: (0, 0))],
        out_specs=tuple(out_specs),
        compiler_params=_params(("arbitrary",)),
        name="outproj_final" if final else "outproj",
    )(o_nsa, o_fox, o_ret, u, u, u, u, u, u, x2d, gate, wb, wo, final_g.reshape(1, d))


def _pad_cols(w, width):
    return jnp.pad(w, ((0, 0), (0, width - w.shape[1])))


def _pack_w_in(w):
    o = 0
    seg = {}
    for name, width in (("nq", 512), ("nkv", 768), ("ng", 24), ("nz", 512), ("fq", 512), ("fk", 512),
                        ("fv", 512), ("ff", 8), ("fz", 512), ("rq", 512), ("rk", 512), ("rv", 512),
                        ("rz", 512), ("mg", 3 * D_MODEL)):
        seg[name] = w[:, o:o + width]
        o += width
    order = [seg["nq"], seg["fq"], seg["rq"], seg["rk"], seg["rv"], seg["nz"], seg["fz"], seg["rz"],
             seg["fk"], seg["fv"], seg["mg"], seg["nkv"], _pad_cols(seg["ng"], LANES), _pad_cols(seg["ff"], LANES)]
    w_pad = jnp.concatenate(order, axis=1).astype(BF16)
    nkv = seg["nkv"]
    t_order = [seg["nq"], seg["fq"], seg["fv"], nkv[:, 256 + LANES:512], nkv[:, 512 + LANES:768],
               _pad_cols(seg["ng"], LANES)]
    wt_sel = jnp.concatenate(t_order, axis=1).T.astype(BF16)
    return w_pad, wt_sel


def _pack_cmp(w1k, w1v, posk, posv):
    def parts(w):
        return jnp.stack([w[:CMP_STRIDE], w[CMP_STRIDE:]], axis=2)

    w4 = jnp.stack([parts(w1k), parts(w1k), parts(w1v), parts(w1v)])
    big = jnp.einsum("cC,csdph->scdCph", jnp.eye(4, dtype=w4.dtype), w4)
    wt = big.reshape(CMP_FEAT, 4 * 2 * CMP_HIDDEN).T.astype(BF16)

    def pos_row(part):
        sl = slice(part * CMP_STRIDE, (part + 1) * CMP_STRIDE)
        return jnp.stack([posk[sl], posk[sl], posv[sl], posv[sl]], axis=1).reshape(CMP_FEAT)

    p_pad = jnp.zeros((LANES, CMP_FEAT), F32).at[0].set(pos_row(0)).at[1].set(pos_row(1))
    return wt, p_pad


def kernel(x_prompt, x_sample, cache_nsa_cmp_kv, cache_nsa_sel_kv, cache_nsa_win_kv, cache_fox_kv, cache_fox_logf, state_ret, page_table, c_prompt, c_sample, norm_g, w_ada, b_ada, w_in, b_forget, w_cmp_k1, w_cmp_k2, pos_cmp_k, w_cmp_v1, w_cmp_v2, pos_cmp_v, w_branch, w_out, final_g):
    batch, seq, d = x_prompt.shape
    nseq, dec = x_sample.shape[:2]
    depth = norm_g.shape[0]
    n_pool = cache_nsa_cmp_kv.shape[1]
    n_pages = page_table.shape[1]
    past = n_pages * PAGE
    wb_len = cache_nsa_win_kv.shape[2]
    assert d == D_MODEL and dec <= SAMPLE_ROWS and past >= wb_len and seq % 512 == 0
    n_sel = -(-(past + dec) // SEL_BLOCK)
    blk_pad = -(-n_sel // LANES) * LANES
    srows = nseq * SAMPLE_ROWS

    hp = x_prompt.reshape(batch * seq, d)
    hs = jnp.pad(x_sample, ((0, 0), (0, SAMPLE_ROWS - dec), (0, 0))).reshape(srows, d)
    c_rows = 8
    c_all = jnp.concatenate([jnp.pad(c_prompt, ((0, c_rows - batch), (0, 0))), c_sample], axis=0)
    lg = jnp.log1p(-jnp.exp2(-5.0 - jnp.arange(RET_HEADS, dtype=F32)))
    lg_lanes = jnp.repeat(lg, HEAD_DIM).reshape(RET_HEADS // 2, 1, LANES)
    zero_state = jnp.zeros((batch, RET_HEADS, HEAD_DIM, HEAD_DIM), F32)
    pool_cmp = jnp.transpose(cache_nsa_cmp_kv, (0, 1, 3, 4, 5, 2)).reshape(depth, n_pool, 256, PAGE)
    pool_sel = jnp.transpose(cache_nsa_sel_kv, (0, 1, 3, 4, 5, 2)).reshape(depth, n_pool, 2, LANES, PAGE)
    pool_fox = jnp.transpose(cache_fox_kv, (0, 1, 3, 4, 5, 2)).reshape(depth, n_pool, 2, FOX_HEADS * HEAD_DIM, PAGE)
    pool_logf = jnp.transpose(cache_fox_logf, (0, 1, 3, 2))
    win_buf = jnp.transpose(cache_nsa_win_kv, (0, 1, 3, 4, 5, 2)).reshape(depth, nseq, 2, LANES, wb_len)

    outs_p, outs_s = [], []
    for l in range(depth):
        w_pad, wt_sel = _pack_w_in(w_in[l])
        wt, p_pad = _pack_cmp(w_cmp_k1[l], w_cmp_v1[l], pos_cmp_k[l], pos_cmp_v[l])
        w2t = jnp.stack([w_cmp_k2[l].T, w_cmp_k2[l].T, w_cmp_v2[l].T, w_cmp_v2[l].T]).astype(BF16)
        wb_bf = w_branch[l].astype(BF16)
        wo_bf = w_out[l].astype(BF16)
        bf_pad = _pad_cols(b_forget[l].reshape(1, FOX_HEADS), LANES)
        final = l == depth - 1

        mod = _adaln(c_all, w_ada[l].astype(BF16), b_ada[l])
        shift_p, scale_p, gate_p = [m.reshape(batch, 1, d) for m in jnp.split(mod[:batch], 3, axis=1)]
        shift_s, scale_s, gate_s = [jnp.repeat(m, SAMPLE_ROWS, axis=0) for m in jnp.split(mod[c_rows:], 3, axis=1)]
        cmp_bias = _cmp_bias(wt, p_pad)

        u = _inproj(hp, norm_g[l], scale_p, shift_p, w_pad, seq)
        ut = _inproj_t(hp, norm_g[l], scale_p, shift_p, wt_sel, seq)
        cmp_rows = u[:, C_CMP:C_CMP + 256]
        logf, c3 = _logf_prompt(u, bf_pad, batch, seq)
        o_fox = _fox_prompt(u, ut, c3, batch, seq)
        chunk = min(2 * RET_CHUNK, seq)
        o_ret, ret_state = _retention(u, zero_state, lg_lanes, batch, seq, chunk, chunk)
        pre_t = _cmp_pre_prompt(cmp_rows.reshape(batch, seq // CMP_STRIDE, CMP_FEAT), wt)
        kvt, kc = _cmp_post(pre_t, cmp_bias, w2t)
        o_nsa = _nsa_prompt(u, ut, kc, kvt, batch, seq)
        res = _outproj(u, o_nsa, o_fox, o_ret, hp, gate_p, wb_bf, wo_bf, final_g, seq, final)
        hp = res[0]
        y_prompt = res[-1]
        win_rows = u[:, C_WIN:C_WIN + 256].reshape(batch, seq, 2, NSA_KV, HEAD_DIM)
        win_state = jnp.concatenate(
            [jnp.zeros((batch, wb_len) + win_rows.shape[2:], F32), win_rows], axis=1)[:, -wb_len:]
        outs_p.append((cmp_rows.reshape(batch, seq, 2, NSA_KV, HEAD_DIM),
                       u[:, C_SEL:C_SEL + 256].reshape(batch, seq, 2, NSA_KV, HEAD_DIM),
                       win_state,
                       u[:, C_FKV:C_FKV + 1024].reshape(batch, seq, 2, FOX_HEADS, HEAD_DIM),
                       logf, ret_state))

        us = _inproj(hs, norm_g[l], scale_s, shift_s, w_pad, srows)
        kvt_s, _ = _cmp_post(_cmp_pre_paged(pool_cmp, l, page_table, wt), cmp_bias, w2t)
        o_cmp_s, selm = _nsa_cmp_sample(us, kvt_s, nseq, past, n_sel, blk_pad)
        o_nsa_s = _nsa_selwin_sample(us, pool_sel, l, page_table, selm, o_cmp_s, win_buf, past)
        c_past = _logf_past(pool_logf, l, page_table)
        o_fox_s, logf_s = _fox_sample(us, pool_fox, l, page_table, c_past, bf_pad)
        o_ret_s, ret_state_s = _retention(us, state_ret[l], lg_lanes, nseq, SAMPLE_ROWS, SAMPLE_ROWS, dec)
        res = _outproj(us, o_nsa_s, o_fox_s, o_ret_s, hs, gate_s, wb_bf, wo_bf, final_g, srows, final)
        hs = res[0]
        y_sample = res[-1]

        def new_rows(c0, width, shape):
            return us[:, c0:c0 + width].reshape(nseq, SAMPLE_ROWS, width)[:, :dec].reshape((nseq, dec) + shape)

        win_new = new_rows(C_WIN, 256, (2, NSA_KV, HEAD_DIM))
        win_all = jnp.concatenate([cache_nsa_win_kv[l], win_new], axis=1)
        outs_s.append((new_rows(C_CMP, 256, (2, NSA_KV, HEAD_DIM)),
                       new_rows(C_SEL, 256, (2, NSA_KV, HEAD_DIM)),
                       win_all[:, dec:],
                       new_rows(C_FKV, 1024, (2, FOX_HEADS, HEAD_DIM)),
                       logf_s.reshape(nseq, SAMPLE_ROWS, LANES)[:, :dec, :FOX_HEADS],
                       ret_state_s))

    def stacked(rows, i):
        return jnp.stack([r[i] for r in rows], axis=0)

    y_prompt = y_prompt.reshape(batch, seq, d)
    y_sample = y_sample.reshape(nseq, SAMPLE_ROWS, d)[:, :dec]
    return (y_prompt, y_sample,
            stacked(outs_p, 0), stacked(outs_s, 0), stacked(outs_p, 1), stacked(outs_s, 1),
            stacked(outs_p, 2), stacked(outs_s, 2), stacked(outs_p, 3), stacked(outs_s, 3),
            stacked(outs_p, 4), stacked(outs_s, 4), stacked(outs_p, 5), stacked(outs_s, 5))
```

```python
import functools

import jax
import jax.numpy as jnp
from jax import lax
from jax.experimental import pallas as pl
from jax.experimental.pallas import tpu as pltpu

F32 = jnp.float32
BF16 = jnp.bfloat16

D_MODEL = 1024
HEAD_DIM = 64
NSA_KV = 2
NSA_REP = 4
CMP_LEN = 32
CMP_STRIDE = 16
CMP_HIDDEN = 128
SEL_BLOCK = 64
SEL_TOPK = 16
WINDOW = 512
FOX_HEADS = 8
RET_HEADS = 8
RET_CHUNK = 128
EPS = 1e-6
NEG_INF = -1e30
FORCE = 1e9
LANES = 128
PAGE = 128
SAMPLE_ROWS = 8
VMEM_LIMIT = 48 * 1024 * 1024

C_NQ, C_FQ, C_RQ, C_RK, C_RV = 0, 512, 1024, 1536, 2048
C_NZ, C_FZ, C_RZ = 2560, 3072, 3584
C_FKV = 4096
C_MG = 5120
C_CMP, C_SEL, C_WIN = 8192, 8448, 8704
C_NG, C_FF = 8960, 9088
D_PAD = 9216
T_NQ, T_FQ, T_FV, T_SELV, T_WINV, T_NG = 0, 512, 1024, 1536, 1664, 1792
T_ROWS = 1920
T_TILE = 384
CMP_FEAT = CMP_STRIDE * 4 * HEAD_DIM


def _params(sem, vmem=VMEM_LIMIT):
    return pltpu.CompilerParams(dimension_semantics=sem, vmem_limit_bytes=vmem)


def _nt(a, b):
    return lax.dot_general(a, b, (((1,), (1,)), ((), ())), preferred_element_type=F32)


def _tn(a, b):
    return lax.dot_general(a, b, (((0,), (0,)), ((), ())), preferred_element_type=F32)


def _dot(a, b):
    return jnp.dot(a, b, preferred_element_type=F32)


def _split3(x):
    hi = x.astype(BF16)
    r1 = x - hi.astype(F32)
    mid = r1.astype(BF16)
    lo = (r1 - mid.astype(F32)).astype(BF16)
    return hi, mid, lo


def _silu(x):
    return x * jax.nn.sigmoid(x)


def _log_sigmoid(x):
    return jnp.minimum(x, 0.0) - jnp.log(1.0 + jnp.exp(-jnp.abs(x)))


def _iota(shape, dim):
    return lax.broadcasted_iota(jnp.int32, shape, dim)


def _div(x, n):
    return lax.shift_right_arithmetic(x, jnp.int32(n.bit_length() - 1))


def _mod(x, n):
    return x & (n - 1)


def _lane_tile(x, n):
    return x if n == 1 else jnp.concatenate([x] * n, axis=1)


def _adaln_kernel(c_ref, w_ref, b_ref, o_ref):
    c = c_ref[...]
    o_ref[...] = _dot(_silu(c).astype(BF16), w_ref[...]) + b_ref[...]


def _adaln(c_all, w_bf, b):
    rows, d = c_all.shape
    n = w_bf.shape[1]
    tn = 1024
    return pl.pallas_call(
        _adaln_kernel,
        out_shape=jax.ShapeDtypeStruct((rows, n), F32),
        grid=(n // tn,),
        in_specs=[pl.BlockSpec((rows, d), lambda j: (0, 0)),
                  pl.BlockSpec((d, tn), lambda j: (0, j)),
                  pl.BlockSpec((1, tn), lambda j: (0, j))],
        out_specs=pl.BlockSpec((rows, tn), lambda j: (0, j)),
        compiler_params=_params(("arbitrary",)),
        name="adaln",
    )(c_all, w_bf, b.reshape(1, n))


def _inproj_kernel(x_ref, g_ref, sc_ref, sh_ref, w_ref, o_ref, h_ref, *, transposed):
    @pl.when(pl.program_id(1) == 0)
    def _():
        x = x_ref[...]
        ms = jnp.mean(x * x, axis=-1, keepdims=True)
        y = x * lax.rsqrt(ms + EPS) * g_ref[...]
        h_ref[...] = (y * (1.0 + sc_ref[...]) + sh_ref[...]).astype(BF16)

    o_ref[...] = _nt(w_ref[...], h_ref[...]) if transposed else _dot(h_ref[...], w_ref[...])


def _mod_spec(mod, tm, rows_per_mod, d, nargs):
    if mod.ndim == 2:
        return pl.BlockSpec((tm, d), (lambda i, j: (i, 0)) if nargs == 2 else (lambda i: (i, 0)))
    per = rows_per_mod // tm
    return pl.BlockSpec((None, 1, d), (lambda i, j: (i // per, 0, 0)) if nargs == 2 else (lambda i: (i // per, 0, 0)))


def _inproj(x2d, g, scale, shift, w_pad, rows_per_mod):
    n, d = x2d.shape
    tm = min(1024, rows_per_mod, n)
    tn = 512
    return pl.pallas_call(
        functools.partial(_inproj_kernel, transposed=False),
        out_shape=jax.ShapeDtypeStruct((n, D_PAD), F32),
        grid=(n // tm, D_PAD // tn),
        in_specs=[pl.BlockSpec((tm, d), lambda i, j: (i, 0)),
                  pl.BlockSpec((1, d), lambda i, j: (0, 0)),
                  _mod_spec(scale, tm, rows_per_mod, d, 2),
                  _mod_spec(shift, tm, rows_per_mod, d, 2),
                  pl.BlockSpec((d, tn), lambda i, j: (0, j))],
        out_specs=pl.BlockSpec((tm, tn), lambda i, j: (i, j)),
        scratch_shapes=[pltpu.VMEM((tm, d), BF16)],
        compiler_params=_params(("arbitrary", "arbitrary")),
        name="inproj",
    )(x2d, g.reshape(1, d), scale, shift, w_pad)


def _inproj_t(x2d, g, scale, shift, wt_sel, rows_per_mod):
    n, d = x2d.shape
    tm = min(1024, rows_per_mod, n)
    tn = T_TILE
    return pl.pallas_call(
        functools.partial(_inproj_kernel, transposed=True),
        out_shape=jax.ShapeDtypeStruct((T_ROWS, n), F32),
        grid=(n // tm, T_ROWS // tn),
        in_specs=[pl.BlockSpec((tm, d), lambda i, j: (i, 0)),
                  pl.BlockSpec((1, d), lambda i, j: (0, 0)),
                  _mod_spec(scale, tm, rows_per_mod, d, 2),
                  _mod_spec(shift, tm, rows_per_mod, d, 2),
                  pl.BlockSpec((tn, d), lambda i, j: (j, 0))],
        out_specs=pl.BlockSpec((tn, tm), lambda i, j: (j, i)),
        scratch_shapes=[pltpu.VMEM((tm, d), BF16)],
        compiler_params=_params(("arbitrary", "arbitrary")),
        name="inproj_t",
    )(x2d, g.reshape(1, d), scale, shift, wt_sel)


def _lane_cumsum(lft, carry):
    t = lft.shape[1]
    upper = (_iota((t, t), 0) <= _iota((t, t), 1)).astype(BF16)
    c = sum(_dot(p, upper) for p in _split3(lft)) + carry
    return c, carry + jnp.sum(lft, axis=1, keepdims=True)


def _head_major_cumsum(lf, carry):
    eye = (_iota((FOX_HEADS, LANES), 0) == _iota((FOX_HEADS, LANES), 1)).astype(BF16)
    lft = sum(_nt(eye, p) for p in _split3(lf))
    return _lane_cumsum(lft, carry)


def _logf_prompt_kernel(ff_ref, bf_ref, logf_ref, c_ref, carry_ref):
    @pl.when(pl.program_id(1) == 0)
    def _():
        carry_ref[...] = jnp.zeros_like(carry_ref)

    lf = _log_sigmoid(ff_ref[...] + bf_ref[...])
    logf_ref[...] = lf[:, :FOX_HEADS]
    c, carry = _head_major_cumsum(lf, carry_ref[...])
    carry_ref[...] = carry
    pieces = [p.astype(F32) for p in _split3(c)]
    tb = c.shape[1]
    c3t = jnp.concatenate(pieces + [jnp.zeros((LANES - 3 * FOX_HEADS, tb), F32)], axis=0)
    c_ref[...] = c3t.T.astype(BF16)


def _logf_prompt(u, bf_pad, batch, seq):
    tb = min(512, seq)
    nt = seq // tb
    return pl.pallas_call(
        _logf_prompt_kernel,
        out_shape=(jax.ShapeDtypeStruct((batch, seq, FOX_HEADS), F32),
                   jax.ShapeDtypeStruct((batch * seq, LANES), BF16)),
        grid=(batch, nt),
        in_specs=[pl.BlockSpec((tb, LANES), lambda b, i: (b * nt + i, C_FF // LANES)),
                  pl.BlockSpec((1, LANES), lambda b, i: (0, 0))],
        out_specs=(pl.BlockSpec((None, tb, FOX_HEADS), lambda b, i: (b, i, 0)),
                   pl.BlockSpec((tb, LANES), lambda b, i: (b * nt + i, 0))),
        scratch_shapes=[pltpu.VMEM((FOX_HEADS, 1), F32)],
        compiler_params=_params(("arbitrary", "arbitrary")),
        name="logf_prompt",
    )(u, bf_pad)


def _logf_past_kernel(pt_ref, *refs, npg):
    pages, (c_ref, carry_ref) = refs[:npg], refs[npg:]

    @pl.when(pl.program_id(1) == 0)
    def _():
        carry_ref[...] = jnp.zeros_like(carry_ref)

    x = jnp.concatenate([pages[k][...] for k in range(npg)], axis=0)
    rows = npg * FOX_HEADS
    local, _ = _lane_cumsum(x, 0.0)
    tot = jnp.sum(x, axis=1, keepdims=True)
    r, c = _iota((rows, rows), 0), _iota((rows, rows), 1)
    earlier = ((_mod(c, FOX_HEADS) == _mod(r, FOX_HEADS)) & (c < r)).astype(BF16)
    tot_l = jnp.broadcast_to(tot, (rows, PAGE))
    offs = sum(_dot(earlier, p) for p in _split3(tot_l))
    carry = carry_ref[...]
    cum = local + offs + jnp.concatenate([carry] * npg, axis=0)
    for k in range(npg):
        c_ref[:, k * PAGE:(k + 1) * PAGE] = cum[k * FOX_HEADS:(k + 1) * FOX_HEADS, :]
    carry_ref[...] = carry + sum(tot[k * FOX_HEADS:(k + 1) * FOX_HEADS, :] for k in range(npg))


def _logf_past(pool_logf_t, layer, page_table, npg=16):
    nseq, n_pages = page_table.shape
    npg = min(npg, n_pages)
    steps = n_pages // npg

    def page_spec(k):
        return pl.BlockSpec((None, None, FOX_HEADS, PAGE), lambda b, j, pt: (layer, pt[b, j * npg + k], 0, 0))

    grid_spec = pltpu.PrefetchScalarGridSpec(
        num_scalar_prefetch=1, grid=(nseq, steps),
        in_specs=[page_spec(k) for k in range(npg)],
        out_specs=pl.BlockSpec((None, FOX_HEADS, npg * PAGE), lambda b, j, pt: (b, 0, j)),
        scratch_shapes=[pltpu.VMEM((FOX_HEADS, 1), F32)])
    return pl.pallas_call(
        functools.partial(_logf_past_kernel, npg=npg),
        out_shape=jax.ShapeDtypeStruct((nseq, FOX_HEADS, n_pages * PAGE), F32),
        grid_spec=grid_spec,
        compiler_params=_params(("arbitrary", "arbitrary")),
        name="logf_past",
    )(page_table, *([pool_logf_t] * npg))


def _fox_prompt_kernel(ii_ref, jj_ref, qt_ref, k_ref, c3_ref, vt_ref, o_ref, qaug_ref, m_ref, l_ref, acc_ref):
    p = pl.program_id(2)
    i, j = ii_ref[p], jj_ref[p]
    tq, tk = qt_ref.shape[1], k_ref.shape[0]
    hp = pl.program_id(1)

    @pl.when(j == 0)
    def _():
        qt = qt_ref[...] * (HEAD_DIM ** -0.5)
        row = _iota(qt.shape, 0)
        for h in range(2):
            head = 2 * hp + h
            piece_row = (row == head) | (row == head + FOX_HEADS) | (row == head + 2 * FOX_HEADS)
            qaug_ref[h] = jnp.concatenate(
                [jnp.where((row < HEAD_DIM) == (h == 0), qt, 0.0),
                 jnp.where(piece_row, -1.0, 0.0)], axis=0).astype(BF16)
        m_ref[...] = jnp.full_like(m_ref, NEG_INF)
        l_ref[...] = jnp.zeros_like(l_ref)
        acc_ref[...] = jnp.zeros_like(acc_ref)

    def step(masked):
        kaug = jnp.concatenate([k_ref[...].astype(BF16), c3_ref[...]], axis=1)
        vt = vt_ref[...].astype(BF16)
        if masked:
            keep = _iota((tk, tq), 0) <= _iota((tk, tq), 1)
        for h in range(2):
            st = _dot(kaug, qaug_ref[h])
            if masked:
                st = jnp.where(keep, st, -jnp.inf)
            m_prev = m_ref[h]
            m_new = jnp.maximum(m_prev, jnp.max(st, axis=0, keepdims=True))
            alpha = jnp.exp(m_prev - m_new)
            pt = jnp.exp(st - m_new)
            l_ref[h] = alpha * l_ref[h] + jnp.sum(pt, axis=0, keepdims=True)
            acc_ref[h] = alpha * acc_ref[h] + _dot(vt[h * HEAD_DIM:(h + 1) * HEAD_DIM, :], pt.astype(BF16))
            m_ref[h] = m_new

    pl.when(j < i)(lambda: step(False))

    @pl.when(j == i)
    def _():
        step(True)
        ot = jnp.concatenate([acc_ref[h] / jnp.maximum(l_ref[h], 1e-30) for h in range(2)], axis=0)
        o_ref[...] = ot.T


def _fox_prompt(u, ut, c3, batch, seq):
    tq = min(512, seq)
    nq = seq // tq
    pairs = FOX_HEADS // 2
    ii = jnp.asarray([i for i in range(nq) for _ in range(i + 1)], jnp.int32)
    jj = jnp.asarray([j for i in range(nq) for j in range(i + 1)], jnp.int32)
    grid_spec = pltpu.PrefetchScalarGridSpec(
        num_scalar_prefetch=2, grid=(batch, pairs, ii.shape[0]),
        in_specs=[pl.BlockSpec((LANES, tq), lambda b, hp, p, ii, jj: (T_FQ // LANES + hp, b * nq + ii[p])),
                  pl.BlockSpec((tq, LANES), lambda b, hp, p, ii, jj: (b * nq + jj[p], C_FKV // LANES + hp)),
                  pl.BlockSpec((tq, LANES), lambda b, hp, p, ii, jj: (b * nq + jj[p], 0)),
                  pl.BlockSpec((LANES, tq), lambda b, hp, p, ii, jj: (T_FV // LANES + hp, b * nq + jj[p]))],
        out_specs=pl.BlockSpec((tq, LANES), lambda b, hp, p, ii, jj: (b * nq + ii[p], hp)),
        scratch_shapes=[pltpu.VMEM((2, 2 * LANES, tq), BF16), pltpu.VMEM((2, 1, tq), F32),
                        pltpu.VMEM((2, 1, tq), F32), pltpu.VMEM((2, HEAD_DIM, tq), F32)])
    return pl.pallas_call(
        _fox_prompt_kernel,
        out_shape=jax.ShapeDtypeStruct((batch * seq, FOX_HEADS * HEAD_DIM), F32),
        grid_spec=grid_spec,
        compiler_params=_params(("arbitrary",) * 3),
        name="fox_prompt",
    )(ii, jj, ut, u, c3, ut)


def _retention_kernel(q_ref, k_ref, v_ref, lg_ref, s0_ref, o_ref, s_ref, sbd_ref, *, c_true):
    ci = pl.program_id(2)
    c = q_ref.shape[0]
    low_row = _iota((LANES, LANES), 0) < HEAD_DIM
    low_col = _iota((LANES, LANES), 1) < HEAD_DIM

    @pl.when(ci == 0)
    def _():
        sbd_ref[...] = s0_ref[...]

    lg = lg_ref[...]
    low = _iota((c, LANES), 1) < HEAD_DIM
    pos = _iota((c, LANES), 0).astype(F32)
    q = q_ref[...]
    k = k_ref[...] * (HEAD_DIM ** -0.5)
    v = v_ref[...].astype(BF16)
    kb = k.astype(BF16)
    diff = (_iota((c, c), 0) - _iota((c, c), 1)).astype(F32)
    inner = []
    for h in range(2):
        lgh = jnp.max(jnp.where(low[:1] == (h == 0), lg, -jnp.inf), axis=1, keepdims=True)
        decay = jnp.where(diff >= 0, jnp.exp(jnp.maximum(diff, 0.0) * lgh), 0.0)
        qm = jnp.where(low == (h == 0), q, 0.0).astype(BF16)
        scores = _nt(qm, kb) * decay
        inner.append(_dot(scores.astype(BF16), v))
    sbd = sbd_ref[...]
    cross = _dot((q * jnp.exp((pos + 1.0) * lg)).astype(BF16), sbd.astype(BF16))
    o = jnp.where(low, inner[0], inner[1]) + cross

    kd = jnp.where(pos < c_true, k * jnp.exp((c_true - 1.0 - pos) * lg), 0.0).astype(BF16)
    upd = jnp.where(low_row == low_col, _tn(kd, v), 0.0)
    sbd_new = jnp.exp(c_true * lg) * sbd + upd
    sbd_ref[...] = sbd_new

    inv = 1.0 / HEAD_DIM
    s_lo = jnp.sum(jnp.where(low, o, 0.0), axis=1, keepdims=True)
    s_hi = jnp.sum(jnp.where(low, 0.0, o), axis=1, keepdims=True)
    d = o - jnp.where(low, s_lo, s_hi) * inv
    d2 = d * d
    v_lo = jnp.sum(jnp.where(low, d2, 0.0), axis=1, keepdims=True)
    v_hi = jnp.sum(jnp.where(low, 0.0, d2), axis=1, keepdims=True)
    o_ref[...] = d * lax.rsqrt(jnp.where(low, v_lo, v_hi) * inv + EPS)

    @pl.when(ci == pl.num_programs(2) - 1)
    def _():
        s_ref[...] = sbd_new


def _to_block_diag(s):
    b = s.shape[0]
    s = s.reshape(b, RET_HEADS // 2, 2, HEAD_DIM, HEAD_DIM)
    z = jnp.zeros_like(s[:, :, 0])
    return jnp.concatenate([jnp.concatenate([s[:, :, 0], z], axis=-1),
                            jnp.concatenate([z, s[:, :, 1]], axis=-1)], axis=-2)


def _from_block_diag(sbd):
    b = sbd.shape[0]
    return jnp.stack([sbd[:, :, :HEAD_DIM, :HEAD_DIM], sbd[:, :, HEAD_DIM:, HEAD_DIM:]],
                     axis=2).reshape(b, RET_HEADS, HEAD_DIM, HEAD_DIM)


def _retention(u, state0, lg_lanes, batch, rows_per_seq, chunk, c_true):
    nc = rows_per_seq // chunk
    pairs = RET_HEADS // 2
    qcol, kcol, vcol = C_RQ // LANES, C_RK // LANES, C_RV // LANES
    o, sbd = pl.pallas_call(
        functools.partial(_retention_kernel, c_true=c_true),
        out_shape=(jax.ShapeDtypeStruct((batch * rows_per_seq, RET_HEADS * HEAD_DIM), F32),
                   jax.ShapeDtypeStruct((batch, pairs, LANES, LANES), F32)),
        grid=(batch, pairs, nc),
        in_specs=[pl.BlockSpec((chunk, LANES), lambda b, hp, ci: (b * nc + ci, qcol + hp)),
                  pl.BlockSpec((chunk, LANES), lambda b, hp, ci: (b * nc + ci, kcol + hp)),
                  pl.BlockSpec((chunk, LANES), lambda b, hp, ci: (b * nc + ci, vcol + hp)),
                  pl.BlockSpec((None, 1, LANES), lambda b, hp, ci: (hp, 0, 0)),
                  pl.BlockSpec((None, None, LANES, LANES), lambda b, hp, ci: (b, hp, 0, 0))],
        out_specs=(pl.BlockSpec((chunk, LANES), lambda b, hp, ci: (b * nc + ci, hp)),
                   pl.BlockSpec((None, None, LANES, LANES), lambda b, hp, ci: (b, hp, 0, 0))),
        scratch_shapes=[pltpu.VMEM((LANES, LANES), F32)],
        compiler_params=_params(("arbitrary",) * 3),
        name="retention",
    )(u, u, u, lg_lanes, _to_block_diag(state0))
    return o, _from_block_diag(sbd)


def _cmp_bias_kernel(wt_ref, p_ref, o_ref):
    r = _nt(wt_ref[...], p_ref[...].astype(BF16))
    part = _mod(_div(_iota(r.shape, 0), CMP_HIDDEN), 2)
    col = jnp.sum(jnp.where(_iota(r.shape, 1) == part, r, 0.0), axis=1, keepdims=True)
    o_ref[...] = jnp.broadcast_to(col, o_ref.shape)


def _cmp_bias(wt, p_pad):
    return pl.pallas_call(
        _cmp_bias_kernel,
        out_shape=jax.ShapeDtypeStruct((wt.shape[0], LANES), F32),
        compiler_params=_params(None),
        name="cmp_bias",
    )(wt, p_pad)


def _cmp_pre_kernel(x_ref, wt_ref, o_ref):
    o_ref[...] = _nt(wt_ref[...], x_ref[...].astype(BF16))


def _cmp_pre_paged_kernel(pt_ref, *refs, npg):
    pages, (wt_ref, o_ref, stage_ref) = refs[:npg], refs[npg:]
    spp = PAGE // CMP_STRIDE
    feat = 4 * HEAD_DIM
    j = _iota((PAGE, PAGE), 0)
    perm = (_iota((PAGE, PAGE), 1) == _mod(j, spp) * CMP_STRIDE + _div(j, spp)).astype(BF16)
    for k in range(npg):
        xp = _nt(perm, pages[k][...].astype(BF16))
        for s in range(CMP_STRIDE):
            stage_ref[s, k * spp:(k + 1) * spp, :] = xp[s * spp:(s + 1) * spp, :]
    acc = None
    for s in range(CMP_STRIDE):
        term = _nt(wt_ref[:, s * feat:(s + 1) * feat], stage_ref[s].astype(BF16))
        acc = term if acc is None else acc + term
    o_ref[...] = acc


def _cmp_pre_prompt(x_seg, wt):
    batch, n_seg, feat = x_seg.shape
    ts = min(128, n_seg)
    return pl.pallas_call(
        _cmp_pre_kernel,
        out_shape=jax.ShapeDtypeStruct((batch, wt.shape[0], n_seg), F32),
        grid=(batch, n_seg // ts),
        in_specs=[pl.BlockSpec((None, ts, feat), lambda b, j: (b, j, 0)),
                  pl.BlockSpec(wt.shape, lambda b, j: (0, 0))],
        out_specs=pl.BlockSpec((None, wt.shape[0], ts), lambda b, j: (b, 0, j)),
        compiler_params=_params(("arbitrary", "arbitrary")),
        name="cmp_pre_prompt",
    )(x_seg, wt)


def _cmp_pre_paged(pool_t, layer, page_table, wt, npg=32):
    nseq, n_pages = page_table.shape
    npg = min(npg, n_pages)
    spp = PAGE // CMP_STRIDE
    feat = pool_t.shape[2]

    def page_spec(k):
        return pl.BlockSpec((None, None, feat, PAGE), lambda b, j, pt: (layer, pt[b, j * npg + k], 0, 0))

    grid_spec = pltpu.PrefetchScalarGridSpec(
        num_scalar_prefetch=1, grid=(nseq, n_pages // npg),
        in_specs=[page_spec(k) for k in range(npg)] + [pl.BlockSpec(wt.shape, lambda b, j, pt: (0, 0))],
        out_specs=pl.BlockSpec((None, wt.shape[0], npg * spp), lambda b, j, pt: (b, 0, j)),
        scratch_shapes=[pltpu.VMEM((CMP_STRIDE, npg * spp, feat), F32)])
    return pl.pallas_call(
        functools.partial(_cmp_pre_paged_kernel, npg=npg),
        out_shape=jax.ShapeDtypeStruct((nseq, wt.shape[0], n_pages * spp), F32),
        grid_spec=grid_spec,
        compiler_params=_params(("arbitrary", "arbitrary")),
        name="cmp_pre_paged",
    )(page_table, *([pool_t] * npg), wt)


def _cmp_post_kernel(pre_ref, bias_ref, w2t_ref, o_ref, k_ref):
    n_seg = pre_ref.shape[1]
    reps = n_seg // LANES
    for c in range(4):
        base = c * 2 * CMP_HIDDEN
        lo = pre_ref[base:base + CMP_HIDDEN, :] + _lane_tile(bias_ref[base:base + CMP_HIDDEN, :], reps)
        hi = (pre_ref[base + CMP_HIDDEN:base + 2 * CMP_HIDDEN, :]
              + _lane_tile(bias_ref[base + CMP_HIDDEN:base + 2 * CMP_HIDDEN, :], reps))
        hid = _silu(lo + pltpu.roll(hi, n_seg - 1, 1))
        o_ref[c * HEAD_DIM:(c + 1) * HEAD_DIM, :] = _dot(w2t_ref[c], hid.astype(BF16))
    k_ref[...] = o_ref[0:2 * HEAD_DIM, :].T


def _cmp_post(pre_t, bias, w2t):
    batch, rows, n_seg = pre_t.shape
    return pl.pallas_call(
        _cmp_post_kernel,
        out_shape=(jax.ShapeDtypeStruct((batch, 4 * HEAD_DIM, n_seg), F32),
                   jax.ShapeDtypeStruct((batch, n_seg, 2 * HEAD_DIM), F32)),
        grid=(batch,),
        in_specs=[pl.BlockSpec((None, rows, n_seg), lambda b: (b, 0, 0)),
                  pl.BlockSpec(bias.shape, lambda b: (0, 0)),
                  pl.BlockSpec(w2t.shape, lambda b: (0, 0, 0))],
        out_specs=(pl.BlockSpec((None, 4 * HEAD_DIM, n_seg), lambda b: (b, 0, 0)),
                   pl.BlockSpec((None, n_seg, 2 * HEAD_DIM), lambda b: (b, 0, 0))),
        compiler_params=_params(("arbitrary",)),
        name="cmp_post",
    )(pre_t, bias, w2t)


def _group_queries(q, g, rows):
    low = _iota((rows, LANES), 1) < HEAD_DIM
    out = []
    for r in range(NSA_REP):
        chunk = 2 * g + r // 2
        x = q[:, chunk * LANES:(chunk + 1) * LANES] * (HEAD_DIM ** -0.5)
        x = jnp.where(low == (r % 2 == 0), x, 0.0)
        if r % 2 != g:
            x = pltpu.roll(x, HEAD_DIM, 1)
        out.append(x)
    return jnp.concatenate(out, axis=0).astype(BF16)


def _slope_col(g, rows):
    r = _div(_iota((NSA_REP * rows, 1), 0), rows)
    return jnp.exp2(-(r + (NSA_REP * g + 1)).astype(F32))


def _softmax_rows(s, valid):
    s = jnp.where(valid, s, NEG_INF)
    m = jnp.max(s, axis=1, keepdims=True)
    p = jnp.where(valid, jnp.exp(s - m), 0.0)
    return p / jnp.maximum(jnp.sum(p, axis=1, keepdims=True), 1e-30)


def _tile_rows(x, n):
    return jnp.concatenate([x] * n, axis=0)


def _cover(n_cmp_pad, n_blk_pad):
    n = _iota((n_cmp_pad, n_blk_pad), 0) * CMP_STRIDE
    j = _iota((n_cmp_pad, n_blk_pad), 1) * SEL_BLOCK
    return ((n < j + SEL_BLOCK) & (n + CMP_LEN - 1 >= j)).astype(BF16)


def _select_blocks(imp, tpos, n_sel, axis=1):
    blk = _iota(imp.shape, axis)
    cur = _div(tpos, SEL_BLOCK)
    forced = (blk == 0) | (blk == cur) | (blk == cur - 1)
    imp = jnp.where(forced, FORCE, imp)
    imp = jnp.where(blk * SEL_BLOCK <= tpos, imp, NEG_INF)
    imp = jnp.where(blk < n_sel, imp, -jnp.inf)
    blkf = blk.astype(F32)

    def body(_, carry):
        imp, sel = carry
        m = jnp.max(imp, axis=axis, keepdims=True)
        idx = jnp.min(jnp.where(imp == m, blkf, 1e9), axis=axis, keepdims=True)
        hit = blkf == idx
        return jnp.where(hit, -jnp.inf, imp), jnp.where(hit, 1.0, sel)

    _, sel = lax.fori_loop(0, min(SEL_TOPK, n_sel), body, (imp, jnp.zeros_like(imp)))
    return sel


def _cmp_branch(qm, kct, vct, dist, valid, slope, rows):
    s = _dot(qm, kct) - slope * _tile_rows(dist, NSA_REP)
    p = _softmax_rows(s, _tile_rows(valid, NSA_REP))
    o = _nt(p.astype(BF16), vct)
    psum = p[0:rows] + p[rows:2 * rows] + p[2 * rows:3 * rows] + p[3 * rows:4 * rows]
    return o, psum


def _importance(psum, cover):
    hi = psum.astype(BF16)
    lo = (psum - hi.astype(F32)).astype(BF16)
    return _dot(hi, cover) + _dot(lo, cover)


def _gate_cols(gates, g, rows):
    lane = _iota(gates.shape, 1)
    cols = []
    for c in range(3):
        per_head = [jnp.sum(jnp.where(lane == (g * NSA_REP + r) * 3 + c, gates, 0.0), axis=1, keepdims=True)
                    for r in range(NSA_REP)]
        cols.append(jnp.concatenate(per_head, axis=0))
    return cols


def _place_group(o, g, rows):
    low = _iota((rows, LANES), 1) < HEAD_DIM
    chunks = []
    for kk in range(2):
        a = o[(2 * kk) * rows:(2 * kk + 1) * rows]
        b = o[(2 * kk + 1) * rows:(2 * kk + 2) * rows]
        if g == 1:
            a = pltpu.roll(a, HEAD_DIM, 1)
        else:
            b = pltpu.roll(b, HEAD_DIM, 1)
        chunks.append(jnp.where(low, a, b))
    return chunks


def _online_update(s, v, m_ref, l_ref, acc_ref, v_transposed=False):
    m_prev = m_ref[...]
    m_new = jnp.maximum(m_prev, jnp.max(s, axis=1, keepdims=True))
    alpha = jnp.exp(m_prev - m_new)
    p = jnp.exp(s - m_new)
    l_ref[...] = alpha * l_ref[...] + jnp.sum(p, axis=1, keepdims=True)
    pb = p.astype(BF16)
    acc_ref[...] = alpha * acc_ref[...] + (_nt(pb, v) if v_transposed else _dot(pb, v))
    m_ref[...] = m_new


def _softmax_cols(st, valid):
    st = jnp.where(valid, st, NEG_INF)
    m = jnp.max(st, axis=0, keepdims=True)
    p = jnp.where(valid, jnp.exp(st - m), 0.0)
    return p / jnp.maximum(jnp.sum(p, axis=0, keepdims=True), 1e-30)


def _online_update_t(st, vt, m_ref, l_ref, acc_ref):
    m_prev = m_ref[...]
    m_new = jnp.maximum(m_prev, jnp.max(st, axis=0, keepdims=True))
    alpha = jnp.exp(m_prev - m_new)
    pt = jnp.exp(st - m_new)
    l_ref[...] = alpha * l_ref[...] + jnp.sum(pt, axis=0, keepdims=True)
    acc_ref[...] = alpha * acc_ref[...] + _dot(vt, pt.astype(BF16))
    m_ref[...] = m_new


def _nsa_prompt_kernel(qt_ref, gt_ref, kc_ref, kvt_ref, selk_ref, selvt_ref, *rest, n_cmp, tk):
    nwin = WINDOW // LANES + 1
    wk, wvt = rest[:nwin], rest[nwin:2 * nwin]
    o_ref, kaug_ref, selvt_bf, qaug_ref, selt_ref, m_ref, l_ref, acc_ref = rest[2 * nwin:]
    i = pl.program_id(1)
    tq = qt_ref.shape[1]
    seq = selk_ref.shape[0]
    n_seg = kc_ref.shape[0]
    n_blk = seq // SEL_BLOCK
    blk_pad = max(LANES, n_blk)
    bpt = tk // SEL_BLOCK
    t0 = i * tq

    @pl.when(i == 0)
    def _():
        local = _iota((tk, LANES), 0)
        lane = _iota((tk, LANES), 1)
        ext = jnp.where(lane < bpt, (_div(local, SEL_BLOCK) == lane).astype(F32),
                        jnp.where(lane == bpt, (local - _mod(local, 2)).astype(F32),
                                  jnp.where(lane == bpt + 1, _mod(local, 2).astype(F32),
                                            jnp.where(lane == bpt + 2, 1.0, 0.0)))).astype(BF16)
        for c in range(seq // tk):
            kaug_ref[c * tk:(c + 1) * tk, :] = jnp.concatenate(
                [selk_ref[c * tk:(c + 1) * tk, :].astype(BF16), ext], axis=1)
            selvt_bf[c] = selvt_ref[:, c * tk:(c + 1) * tk].astype(BF16)

    tpos = t0 + _iota((1, tq), 1)
    kc = kc_ref[...].astype(BF16)
    vct = kvt_ref[2 * HEAD_DIM:4 * HEAD_DIM, :].astype(BF16)
    n_col = _iota((n_seg, 1), 0)
    dist_c = (tpos - (n_col * CMP_STRIDE + CMP_LEN - 1)).astype(F32)
    dist_c = _lane_tile(dist_c, NSA_REP)
    valid_c = (dist_c >= 0) & (n_col < n_cmp)
    blk = _iota((blk_pad, n_seg), 0) * SEL_BLOCK
    seg = _iota((blk_pad, n_seg), 1) * CMP_STRIDE
    cover_t = ((seg < blk + SEL_BLOCK) & (seg + CMP_LEN - 1 >= blk)).astype(BF16)
    gates_t = jax.nn.sigmoid(gt_ref[...])
    qt = qt_ref[...] * (HEAD_DIM ** -0.5)

    kw = jnp.concatenate([w[...] for w in wk], axis=0).astype(BF16)
    vwt = jnp.concatenate([w[...] for w in wvt], axis=1).astype(BF16)
    wpos = t0 - WINDOW + _iota((kw.shape[0], 1), 0)
    dist_w = (tpos - wpos).astype(F32)
    dist_w = _lane_tile(dist_w, NSA_REP)
    valid_w = (dist_w >= 0) & (dist_w < WINDOW) & (wpos >= 0)

    n_tiles = _div(t0 + tq + tk - 1, tk)
    zero = jnp.zeros((HEAD_DIM, tq), F32)
    out_rows = []
    for g in range(NSA_KV):
        heads = [qt[(g * NSA_REP + r) * HEAD_DIM:(g * NSA_REP + r + 1) * HEAD_DIM, :] for r in range(NSA_REP)]
        qmt = jnp.concatenate([jnp.concatenate([h, zero] if g == 0 else [zero, h], axis=0) for h in heads],
                              axis=1).astype(BF16)
        slope = jnp.concatenate([jnp.full((1, tq), 2.0 ** -(NSA_REP * g + r + 1), F32) for r in range(NSA_REP)],
                                axis=1)

        pt = _softmax_cols(_dot(kc, qmt) - slope * dist_c, valid_c)
        o_cmp = _dot(vct, pt.astype(BF16))
        psum = pt[:, 0:tq] + pt[:, tq:2 * tq] + pt[:, 2 * tq:3 * tq] + pt[:, 3 * tq:4 * tq]
        hi = psum.astype(BF16)
        lo = (psum - hi.astype(F32)).astype(BF16)
        sel_t = _select_blocks(_dot(cover_t, hi) + _dot(cover_t, lo), tpos, n_blk, axis=0)
        selt_ref[...] = sel_t
        qaug_ref[0:LANES, :] = qmt
        qaug_ref[LANES + 16:2 * LANES, :] = jnp.zeros((LANES - 16, NSA_REP * tq), BF16)

        m_ref[...] = jnp.full_like(m_ref, -1e29)
        l_ref[...] = jnp.zeros_like(l_ref)
        acc_ref[...] = jnp.zeros_like(acc_ref)
        row8 = _iota((8, NSA_REP * tq), 0)
        tpos4 = _lane_tile(tpos, NSA_REP)

        def tile(j, _):
            picked = selt_ref[pl.ds(pl.multiple_of(j * bpt, bpt), bpt), :]

            def attend(diagonal):
                shift = (j * tk - t0).astype(F32)
                mask_rows = (_lane_tile(picked, NSA_REP) - 1.0) * 1e30
                bias_rows = jnp.where(row8 < 2, slope, jnp.where(row8 == 2, slope * shift, 0.0))
                qaug_ref[LANES:LANES + 16, :] = jnp.concatenate([mask_rows, bias_rows], axis=0).astype(BF16)
                r = pl.multiple_of(j * tk, tk)
                st = _dot(kaug_ref[pl.ds(r, tk), :], qaug_ref[...])
                if diagonal:
                    st = jnp.where(j * tk + _iota((tk, 1), 0) <= tpos4, st, -jnp.inf)
                _online_update_t(st, selvt_bf[j], m_ref, l_ref, acc_ref)

            active = jnp.max(picked) > 0.5
            pl.when(active & (j < n_tiles - 1))(lambda: attend(False))
            pl.when(active & (j == n_tiles - 1))(lambda: attend(True))
            return 0

        lax.fori_loop(0, n_tiles, tile, 0)
        o_sel = acc_ref[...] / jnp.maximum(l_ref[...], 1e-30)

        o_win = _dot(vwt, _softmax_cols(_dot(kw, qmt) - slope * dist_w, valid_w).astype(BF16))

        def gate_row(c):
            return jnp.concatenate([gates_t[(g * NSA_REP + r) * 3 + c:(g * NSA_REP + r) * 3 + c + 1, :]
                                    for r in range(NSA_REP)], axis=1)

        o = gate_row(0) * o_cmp + gate_row(1) * o_sel + gate_row(2) * o_win
        out_rows += [o[g * HEAD_DIM:(g + 1) * HEAD_DIM, r * tq:(r + 1) * tq] for r in range(NSA_REP)]
    o_ref[...] = jnp.concatenate(out_rows, axis=0).T


def _nsa_prompt(u, ut, kc, kvt, batch, seq):
    tq = LANES
    nq = seq // tq
    tk = min(512, seq)
    n_seg = kvt.shape[2]
    n_cmp = seq // CMP_STRIDE - 1
    nwin = WINDOW // tq + 1

    def win_row(b, i, k):
        return b * nq + jnp.maximum(i - WINDOW // tq + k, 0)

    return pl.pallas_call(
        functools.partial(_nsa_prompt_kernel, n_cmp=n_cmp, tk=tk),
        out_shape=jax.ShapeDtypeStruct((batch * seq, 512), F32),
        grid=(batch, nq),
        in_specs=[pl.BlockSpec((512, tq), lambda b, i: (T_NQ // 512, b * nq + i)),
                  pl.BlockSpec((LANES, tq), lambda b, i: (T_NG // LANES, b * nq + i)),
                  pl.BlockSpec((None, n_seg, LANES), lambda b, i: (b, 0, 0)),
                  pl.BlockSpec((None, 4 * HEAD_DIM, n_seg), lambda b, i: (b, 0, 0)),
                  pl.BlockSpec((seq, LANES), lambda b, i: (b, C_SEL // LANES)),
                  pl.BlockSpec((LANES, seq), lambda b, i: (T_SELV // LANES, b))]
                 + [pl.BlockSpec((tq, LANES), lambda b, i, k=k: (win_row(b, i, k), C_WIN // LANES))
                    for k in range(nwin)]
                 + [pl.BlockSpec((LANES, tq), lambda b, i, k=k: (T_WINV // LANES, win_row(b, i, k)))
                    for k in range(nwin)],
        out_specs=pl.BlockSpec((tq, 512), lambda b, i: (b * nq + i, 0)),
        scratch_shapes=[pltpu.VMEM((seq, 2 * LANES), BF16), pltpu.VMEM((seq // tk, LANES, tk), BF16),
                        pltpu.VMEM((2 * LANES, NSA_REP * tq), BF16),
                        pltpu.VMEM((max(LANES, seq // SEL_BLOCK), tq), F32),
                        pltpu.VMEM((1, NSA_REP * tq), F32), pltpu.VMEM((1, NSA_REP * tq), F32),
                        pltpu.VMEM((LANES, NSA_REP * tq), F32)],
        compiler_params=_params(("arbitrary", "arbitrary")),
        name="nsa_prompt",
    )(ut, ut, kc, kvt, u, ut, *([u] * nwin), *([ut] * nwin))


def _nsa_cmp_sample_kernel(q_ref, kvt_ref, o_ref, sel_ref, *, past, n_cmp, n_sel):
    rows = SAMPLE_ROWS
    n_seg = kvt_ref.shape[1]
    blk_pad = sel_ref.shape[1]
    tpos = past + _iota((rows, 1), 0)
    kct = kvt_ref[0:2 * HEAD_DIM, :].astype(BF16)
    vct = kvt_ref[2 * HEAD_DIM:4 * HEAD_DIM, :].astype(BF16)
    n_idx = _iota((rows, n_seg), 1)
    dist = (tpos - (n_idx * CMP_STRIDE + CMP_LEN - 1)).astype(F32)
    valid = (dist >= 0) & (n_idx < n_cmp)
    cover = _cover(n_seg, blk_pad)
    q = q_ref[...]
    for g in range(NSA_KV):
        qm = _group_queries(q, g, rows)
        o, psum = _cmp_branch(qm, kct, vct, dist, valid, _slope_col(g, rows), rows)
        o_ref[g * NSA_REP * rows:(g + 1) * NSA_REP * rows, :] = o
        sel_ref[g * rows:(g + 1) * rows, :] = _select_blocks(_importance(psum, cover), tpos, n_sel)


def _nsa_cmp_sample(u_s, kvt, nseq, past, n_sel, blk_pad):
    n_seg = kvt.shape[2]
    return pl.pallas_call(
        functools.partial(_nsa_cmp_sample_kernel, past=past, n_cmp=n_seg - 1, n_sel=n_sel),
        out_shape=(jax.ShapeDtypeStruct((nseq, NSA_KV * NSA_REP * SAMPLE_ROWS, LANES), F32),
                   jax.ShapeDtypeStruct((nseq, NSA_KV * SAMPLE_ROWS, blk_pad), F32)),
        grid=(nseq,),
        in_specs=[pl.BlockSpec((SAMPLE_ROWS, 512), lambda b: (b, C_NQ // 512)),
                  pl.BlockSpec((None, 4 * HEAD_DIM, n_seg), lambda b: (b, 0, 0))],
        out_specs=(pl.BlockSpec((None, NSA_KV * NSA_REP * SAMPLE_ROWS, LANES), lambda b: (b, 0, 0)),
                   pl.BlockSpec((None, NSA_KV * SAMPLE_ROWS, blk_pad), lambda b: (b, 0, 0))),
        compiler_params=_params(("arbitrary",)),
        name="nsa_cmp_sample",
    )(u_s, kvt)


def _stack_groups(fn):
    return jnp.concatenate([fn(g) for g in range(NSA_KV)], axis=0)


def _nsa_selwin_sample_kernel(pt_ref, src_ref, act_ref, *refs, npg, past):
    pages = refs[:npg]
    (q_ref, gate_ref, selm_ref, ocmp_ref, winbuf_ref, selnew_ref, winnew_ref, o_ref,
     qm_ref, selrows_ref, m_ref, l_ref, acc_ref, pad_ref, kt_ref, vt_ref) = refs[npg:]
    j = pl.program_id(1)
    rows = SAMPLE_ROWS
    nrow = NSA_KV * NSA_REP * rows
    blk_pad = selm_ref.shape[1]
    tk = npg * PAGE
    tpos = past + _mod(_iota((nrow, 1), 0), rows)
    slope = _stack_groups(lambda g: _slope_col(g, rows))

    @pl.when(j == 0)
    def _():
        q = q_ref[...]
        qm_ref[...] = _stack_groups(lambda g: _group_queries(q, g, rows))
        selrows_ref[...] = _stack_groups(
            lambda g: _tile_rows(selm_ref[g * rows:(g + 1) * rows, :], NSA_REP)).astype(BF16)
        m_ref[...] = jnp.full_like(m_ref, NEG_INF)
        l_ref[...] = jnp.zeros_like(l_ref)
        acc_ref[...] = jnp.zeros_like(acc_ref)
        pad_ref[...] = jnp.zeros_like(pad_ref)

    def attend(score_fn, v_fn, v_transposed, kpos):
        n = kpos.shape[1]
        expand = (_iota((blk_pad, n), 0) == _div(jnp.broadcast_to(kpos, (blk_pad, n)), SEL_BLOCK)).astype(BF16)
        chosen = _dot(selrows_ref[...], expand)
        dist = (tpos - kpos).astype(F32)
        ok = (chosen > 0.5) & (dist >= 0)
        s = jnp.where(ok, score_fn() - slope * dist, -jnp.inf)
        _online_update(s, v_fn(), m_ref, l_ref, acc_ref, v_transposed)

    @pl.when(act_ref[pl.program_id(0), j] > 0)
    def _():
        for k in range(npg):
            kt_ref[:, k * PAGE:(k + 1) * PAGE] = pages[k][0].astype(BF16)
            vt_ref[:, k * PAGE:(k + 1) * PAGE] = pages[k][1].astype(BF16)
        attend(lambda: _dot(qm_ref[...], kt_ref[...]), lambda: vt_ref[...], True, j * tk + _iota((1, tk), 1))

    @pl.when(j == pl.num_programs(1) - 1)
    def _():
        pad_ref[0:rows, :] = selnew_ref[...]
        new = pad_ref[...]
        attend(lambda: _nt(qm_ref[...], new[:, 0:LANES].astype(BF16)),
               lambda: new[:, LANES:2 * LANES].astype(BF16), False, past + _iota((1, PAGE), 1))
        o_sel = acc_ref[...] / jnp.maximum(l_ref[...], 1e-30)

        pad_ref[0:rows, :] = winnew_ref[...]
        wnew = pad_ref[...]
        wb = winbuf_ref.shape[2]
        nw = wb + PAGE
        widx = _iota((1, nw), 1)
        wpos = jnp.where(widx < wb, past - wb + widx, past + widx - wb)
        dist = (tpos - wpos).astype(F32)
        valid = (dist >= 0) & (dist < WINDOW) & (wpos >= 0)
        s = jnp.concatenate([_dot(qm_ref[...], winbuf_ref[0].astype(BF16)),
                             _nt(qm_ref[...], wnew[:, 0:LANES].astype(BF16))], axis=1) - slope * dist
        p = _softmax_rows(s, valid).astype(BF16)
        o_win = _nt(p[:, 0:wb], winbuf_ref[1].astype(BF16)) + _dot(p[:, wb:], wnew[:, LANES:2 * LANES].astype(BF16))

        gates = jax.nn.sigmoid(gate_ref[...])
        o_cmp = ocmp_ref[...]
        for g in range(NSA_KV):
            gc, gs, gw = _gate_cols(gates, g, rows)
            sl = slice(g * NSA_REP * rows, (g + 1) * NSA_REP * rows)
            chunks = _place_group(gc * o_cmp[sl] + gs * o_sel[sl] + gw * o_win[sl], g, rows)
            for kk in range(2):
                o_ref[:, g * 2 * LANES + kk * LANES:g * 2 * LANES + (kk + 1) * LANES] = chunks[kk]


def _nsa_selwin_sample(u_s, pool_sel_t, layer, page_table, selm, o_cmp, win_buf_t, past, npg=16):
    nseq, n_pages = page_table.shape
    npg = min(npg, n_pages)
    blk_pad = selm.shape[2]
    nrow = NSA_KV * NSA_REP * SAMPLE_ROWS
    wb = win_buf_t.shape[4]

    steps = n_pages // npg
    bps = npg * PAGE // SEL_BLOCK
    picked = selm[:, :, :steps * bps].reshape(nseq, selm.shape[1], steps, bps)
    active = jnp.max(picked, axis=(1, 3)) > 0.5
    idx = jnp.arange(steps, dtype=jnp.int32)[None, :]
    src = jnp.maximum(lax.cummax(jnp.where(active, idx, -1), axis=1), 0).astype(jnp.int32)
    act = active.astype(jnp.int32)

    def page_spec(k):
        return pl.BlockSpec((None, None, 2, LANES, PAGE),
                            lambda b, j, pt, src, act: (layer, pt[b, src[b, j] * npg + k], 0, 0, 0))

    grid_spec = pltpu.PrefetchScalarGridSpec(
        num_scalar_prefetch=3, grid=(nseq, steps),
        in_specs=[page_spec(k) for k in range(npg)] + [
            pl.BlockSpec((SAMPLE_ROWS, 512), lambda b, j, *_: (b, C_NQ // 512)),
            pl.BlockSpec((SAMPLE_ROWS, LANES), lambda b, j, *_: (b, C_NG // LANES)),
            pl.BlockSpec((None, NSA_KV * SAMPLE_ROWS, blk_pad), lambda b, j, *_: (b, 0, 0)),
            pl.BlockSpec((None, nrow, LANES), lambda b, j, *_: (b, 0, 0)),
            pl.BlockSpec((None, None, 2, LANES, wb), lambda b, j, *_: (layer, b, 0, 0, 0)),
            pl.BlockSpec((SAMPLE_ROWS, 256), lambda b, j, *_: (b, C_SEL // 256)),
            pl.BlockSpec((SAMPLE_ROWS, 256), lambda b, j, *_: (b, C_WIN // 256))],
        out_specs=pl.BlockSpec((SAMPLE_ROWS, 512), lambda b, j, *_: (b, 0)),
        scratch_shapes=[pltpu.VMEM((nrow, LANES), BF16), pltpu.VMEM((nrow, blk_pad), BF16),
                        pltpu.VMEM((nrow, 1), F32), pltpu.VMEM((nrow, 1), F32),
                        pltpu.VMEM((nrow, LANES), F32), pltpu.VMEM((PAGE, 256), F32),
                        pltpu.VMEM((LANES, npg * PAGE), BF16), pltpu.VMEM((LANES, npg * PAGE), BF16)])
    return pl.pallas_call(
        functools.partial(_nsa_selwin_sample_kernel, npg=npg, past=past),
        out_shape=jax.ShapeDtypeStruct((nseq * SAMPLE_ROWS, 512), F32),
        grid_spec=grid_spec,
        compiler_params=_params(("arbitrary", "arbitrary")),
        name="nsa_selwin_sample",
    )(page_table, src, act, *([pool_sel_t] * npg), u_s, u_s, selm, o_cmp, win_buf_t, u_s, u_s)


def _fox_sample_kernel(pt_ref, *refs, npg):
    pages = refs[:npg]
    (q_ref, cpast_ref, clast_ref, knew_ref, vnew_ref, ff_ref, bf_ref, o_ref, logf_ref,
     qbd_ref, m_ref, l_ref, acc_ref, kpad_ref, vpad_ref, fpad_ref) = refs[npg:]
    j = pl.program_id(1)
    rows = SAMPLE_ROWS
    width = FOX_HEADS * HEAD_DIM
    nrow = FOX_HEADS * rows

    @pl.when(j == 0)
    def _():
        q = q_ref[...] * (HEAD_DIM ** -0.5)
        head = _div(_iota((rows, width), 1), HEAD_DIM)
        qbd_ref[...] = jnp.concatenate(
            [jnp.where(head == h, q, 0.0) for h in range(FOX_HEADS)], axis=0).astype(BF16)
        m_ref[...] = jnp.full_like(m_ref, NEG_INF)
        l_ref[...] = jnp.zeros_like(l_ref)
        acc_ref[...] = jnp.zeros_like(acc_ref)
        kpad_ref[...] = jnp.zeros_like(kpad_ref)
        vpad_ref[...] = jnp.zeros_like(vpad_ref)
        fpad_ref[...] = jnp.zeros_like(fpad_ref)

    def head_rows(x):
        return jnp.concatenate([jnp.broadcast_to(x[h:h + 1, :], (rows, x.shape[1])) for h in range(FOX_HEADS)], axis=0)

    c_last = jnp.max(jnp.where(_iota(clast_ref.shape, 1) == clast_ref.shape[1] - 1, clast_ref[...], -jnp.inf),
                     axis=1, keepdims=True)
    qbd = qbd_ref[...]
    s = jnp.concatenate([_dot(qbd, pages[k][0].astype(BF16)) for k in range(npg)], axis=1)
    s = s + head_rows(c_last - cpast_ref[...])
    m_prev = m_ref[...]
    m_new = jnp.maximum(m_prev, jnp.max(s, axis=1, keepdims=True))
    alpha = jnp.exp(m_prev - m_new)
    p = jnp.exp(s - m_new)
    l_ref[...] = alpha * l_ref[...] + jnp.sum(p, axis=1, keepdims=True)
    pb = p.astype(BF16)
    acc = alpha * acc_ref[...]
    for k in range(npg):
        acc = acc + _nt(pb[:, k * PAGE:(k + 1) * PAGE], pages[k][1].astype(BF16))
    acc_ref[...] = acc
    m_ref[...] = m_new

    @pl.when(j == pl.num_programs(1) - 1)
    def _():
        lf = _log_sigmoid(ff_ref[...] + bf_ref[...])
        logf_ref[...] = lf
        fpad_ref[0:rows, :] = lf
        c_new, _ = _head_major_cumsum(fpad_ref[...], 0.0)
        kpad_ref[0:rows, :] = knew_ref[...]
        vpad_ref[0:rows, :] = vnew_ref[...]
        s = _nt(qbd_ref[...], kpad_ref[...].astype(BF16)) - head_rows(c_new)
        causal = _iota((nrow, PAGE), 1) <= _mod(_iota((nrow, PAGE), 0), rows)
        _online_update(jnp.where(causal, s, -jnp.inf), vpad_ref[...].astype(BF16), m_ref, l_ref, acc_ref)
        o = acc_ref[...] / jnp.maximum(l_ref[...], 1e-30)
        head = _div(_iota((rows, width), 1), HEAD_DIM)
        out = jnp.zeros((rows, width), F32)
        for h in range(FOX_HEADS):
            out = jnp.where(head == h, o[h * rows:(h + 1) * rows], out)
        o_ref[...] = out


def _fox_sample(u_s, pool_kv_t, layer, page_table, c_past, bf_pad, npg=16):
    nseq, n_pages = page_table.shape
    npg = min(npg, n_pages)
    width = FOX_HEADS * HEAD_DIM
    nrow = FOX_HEADS * SAMPLE_ROWS
    past = n_pages * PAGE
    tk = npg * PAGE

    def page_spec(k):
        return pl.BlockSpec((None, None, 2, width, PAGE), lambda b, j, pt: (layer, pt[b, j * npg + k], 0, 0, 0))

    grid_spec = pltpu.PrefetchScalarGridSpec(
        num_scalar_prefetch=1, grid=(nseq, n_pages // npg),
        in_specs=[page_spec(k) for k in range(npg)] + [
            pl.BlockSpec((SAMPLE_ROWS, width), lambda b, j, pt: (b, C_FQ // width)),
            pl.BlockSpec((None, FOX_HEADS, tk), lambda b, j, pt: (b, 0, j)),
            pl.BlockSpec((None, FOX_HEADS, LANES), lambda b, j, pt: (b, 0, past // LANES - 1)),
            pl.BlockSpec((SAMPLE_ROWS, width), lambda b, j, pt: (b, C_FKV // width)),
            pl.BlockSpec((SAMPLE_ROWS, width), lambda b, j, pt: (b, C_FKV // width + 1)),
            pl.BlockSpec((SAMPLE_ROWS, LANES), lambda b, j, pt: (b, C_FF // LANES)),
            pl.BlockSpec((1, LANES), lambda b, j, pt: (0, 0))],
        out_specs=(pl.BlockSpec((SAMPLE_ROWS, width), lambda b, j, pt: (b, 0)),
                   pl.BlockSpec((SAMPLE_ROWS, LANES), lambda b, j, pt: (b, 0))),
        scratch_shapes=[pltpu.VMEM((nrow, width), BF16), pltpu.VMEM((nrow, 1), F32), pltpu.VMEM((nrow, 1), F32),
                        pltpu.VMEM((nrow, width), F32), pltpu.VMEM((PAGE, width), F32),
                        pltpu.VMEM((PAGE, width), F32), pltpu.VMEM((PAGE, LANES), F32)])
    return pl.pallas_call(
        functools.partial(_fox_sample_kernel, npg=npg),
        out_shape=(jax.ShapeDtypeStruct((nseq * SAMPLE_ROWS, width), F32),
                   jax.ShapeDtypeStruct((nseq * SAMPLE_ROWS, LANES), F32)),
        grid_spec=grid_spec,
        compiler_params=_params(("arbitrary", "arbitrary")),
        name="fox_sample",
    )(page_table, *([pool_kv_t] * npg), u_s, c_past, c_past, u_s, u_s, u_s, bf_pad)


def _outproj_kernel(on_ref, of_ref, or_ref, nz_ref, fz_ref, rz_ref, g0_ref, g1_ref, g2_ref, x_ref, gate_ref,
                    wb_ref, wo_ref, fg_ref, *out_refs, final):
    merged = None
    for n, (o_ref, z_ref, g_ref) in enumerate(((on_ref, nz_ref, g0_ref), (of_ref, fz_ref, g1_ref),
                                               (or_ref, rz_ref, g2_ref))):
        br = (o_ref[...] * _silu(z_ref[...])).astype(BF16)
        term = jax.nn.sigmoid(g_ref[...]) * _dot(br, wb_ref[n])
        merged = term if merged is None else merged + term
    y = _dot(merged.astype(BF16), wo_ref[...])
    x = x_ref[...] + gate_ref[...] * y
    out_refs[0][...] = x
    if final:
        ms = jnp.mean(x * x, axis=-1, keepdims=True)
        out_refs[1][...] = x * lax.rsqrt(ms + EPS) * fg_ref[...]


def _outproj(u, o_nsa, o_fox, o_ret, x2d, gate, wb, wo, final_g, rows_per_mod, final):
    n, d = x2d.shape
    tm = min(256, rows_per_mod, n)
    bw = 512

    def row(i):
        return (i, 0)

    out_shape = [jax.ShapeDtypeStruct((n, d), F32)] * (2 if final else 1)
    out_specs = [pl.BlockSpec((tm, d), row)] * (2 if final else 1)
    return pl.pallas_call(
        functools.partial(_outproj_kernel, final=final),
        out_shape=tuple(out_shape),
        grid=(n // tm,),
        in_specs=[pl.BlockSpec((tm, bw), row)] * 3
                 + [pl.BlockSpec((tm, bw), lambda i, c=c: (i, c // bw)) for c in (C_NZ, C_FZ, C_RZ)]
                 + [pl.BlockSpec((tm, d), lambda i, c=c: (i, C_MG // d + c)) for c in range(3)]
                 + [pl.BlockSpec((tm, d), row),
                    _mod_spec(gate, tm, rows_per_mod, d, 1),
                    pl.BlockSpec(wb.shape, lambda i: (0, 0, 0)),
                    pl.BlockSpec(wo.shape, lambda i: (0, 0)),
                    pl.BlockSpec((1, d), lambda i: (0, 0))],
        out_specs=tuple(out_specs),
        compiler_params=_params(("arbitrary",)),
        name="outproj_final" if final else "outproj",
    )(o_nsa, o_fox, o_ret, u, u, u, u, u, u, x2d, gate, wb, wo, final_g.reshape(1, d))


def _pad_cols(w, width):
    return jnp.pad(w, ((0, 0), (0, width - w.shape[1])))


def _pack_w_in(w):
    o = 0
    seg = {}
    for name, width in (("nq", 512), ("nkv", 768), ("ng", 24), ("nz", 512), ("fq", 512), ("fk", 512),
                        ("fv", 512), ("ff", 8), ("fz", 512), ("rq", 512), ("rk", 512), ("rv", 512),
                        ("rz", 512), ("mg", 3 * D_MODEL)):
        seg[name] = w[:, o:o + width]
        o += width
    order = [seg["nq"], seg["fq"], seg["rq"], seg["rk"], seg["rv"], seg["nz"], seg["fz"], seg["rz"],
             seg["fk"], seg["fv"], seg["mg"], seg["nkv"], _pad_cols(seg["ng"], LANES), _pad_cols(seg["ff"], LANES)]
    w_pad = jnp.concatenate(order, axis=1).astype(BF16)
    nkv = seg["nkv"]
    t_order = [seg["nq"], seg["fq"], seg["fv"], nkv[:, 256 + LANES:512], nkv[:, 512 + LANES:768],
               _pad_cols(seg["ng"], LANES)]
    wt_sel = jnp.concatenate(t_order, axis=1).T.astype(BF16)
    return w_pad, wt_sel


def _pack_cmp(w1k, w1v, posk, posv):
    def parts(w):
        return jnp.stack([w[:CMP_STRIDE], w[CMP_STRIDE:]], axis=2)

    w4 = jnp.stack([parts(w1k), parts(w1k), parts(w1v), parts(w1v)])
    big = jnp.einsum("cC,csdph->scdCph", jnp.eye(4, dtype=w4.dtype), w4)
    wt = big.reshape(CMP_FEAT, 4 * 2 * CMP_HIDDEN).T.astype(BF16)

    def pos_row(part):
        sl = slice(part * CMP_STRIDE, (part + 1) * CMP_STRIDE)
        return jnp.stack([posk[sl], posk[sl], posv[sl], posv[sl]], axis=1).reshape(CMP_FEAT)

    p_pad = jnp.zeros((LANES, CMP_FEAT), F32).at[0].set(pos_row(0)).at[1].set(pos_row(1))
    return wt, p_pad


def kernel(x_prompt, x_sample, cache_nsa_cmp_kv, cache_nsa_sel_kv, cache_nsa_win_kv, cache_fox_kv, cache_fox_logf, state_ret, page_table, c_prompt, c_sample, norm_g, w_ada, b_ada, w_in, b_forget, w_cmp_k1, w_cmp_k2, pos_cmp_k, w_cmp_v1, w_cmp_v2, pos_cmp_v, w_branch, w_out, final_g):
    batch, seq, d = x_prompt.shape
    nseq, dec = x_sample.shape[:2]
    depth = norm_g.shape[0]
    n_pool = cache_nsa_cmp_kv.shape[1]
    n_pages = page_table.shape[1]
    past = n_pages * PAGE
    wb_len = cache_nsa_win_kv.shape[2]
    assert d == D_MODEL and dec <= SAMPLE_ROWS and past >= wb_len and seq % 512 == 0
    n_sel = -(-(past + dec) // SEL_BLOCK)
    blk_pad = -(-n_sel // LANES) * LANES
    srows = nseq * SAMPLE_ROWS

    hp = x_prompt.reshape(batch * seq, d)
    hs = jnp.pad(x_sample, ((0, 0), (0, SAMPLE_ROWS - dec), (0, 0))).reshape(srows, d)
    c_rows = 8
    c_all = jnp.concatenate([jnp.pad(c_prompt, ((0, c_rows - batch), (0, 0))), c_sample], axis=0)
    lg = jnp.log1p(-jnp.exp2(-5.0 - jnp.arange(RET_HEADS, dtype=F32)))
    lg_lanes = jnp.repeat(lg, HEAD_DIM).reshape(RET_HEADS // 2, 1, LANES)
    zero_state = jnp.zeros((batch, RET_HEADS, HEAD_DIM, HEAD_DIM), F32)
    pool_cmp = jnp.transpose(cache_nsa_cmp_kv, (0, 1, 3, 4, 5, 2)).reshape(depth, n_pool, 256, PAGE)
    pool_sel = jnp.transpose(cache_nsa_sel_kv, (0, 1, 3, 4, 5, 2)).reshape(depth, n_pool, 2, LANES, PAGE)
    pool_fox = jnp.transpose(cache_fox_kv, (0, 1, 3, 4, 5, 2)).reshape(depth, n_pool, 2, FOX_HEADS * HEAD_DIM, PAGE)
    pool_logf = jnp.transpose(cache_fox_logf, (0, 1, 3, 2))
    win_buf = jnp.transpose(cache_nsa_win_kv, (0, 1, 3, 4, 5, 2)).reshape(depth, nseq, 2, LANES, wb_len)

    outs_p, outs_s = [], []
    for l in range(depth):
        w_pad, wt_sel = _pack_w_in(w_in[l])
        wt, p_pad = _pack_cmp(w_cmp_k1[l], w_cmp_v1[l], pos_cmp_k[l], pos_cmp_v[l])
        w2t = jnp.stack([w_cmp_k2[l].T, w_cmp_k2[l].T, w_cmp_v2[l].T, w_cmp_v2[l].T]).astype(BF16)
        wb_bf = w_branch[l].astype(BF16)
        wo_bf = w_out[l].astype(BF16)
        bf_pad = _pad_cols(b_forget[l].reshape(1, FOX_HEADS), LANES)
        final = l == depth - 1

        mod = _adaln(c_all, w_ada[l].astype(BF16), b_ada[l])
        shift_p, scale_p, gate_p = [m.reshape(batch, 1, d) for m in jnp.split(mod[:batch], 3, axis=1)]
        shift_s, scale_s, gate_s = [jnp.repeat(m, SAMPLE_ROWS, axis=0) for m in jnp.split(mod[c_rows:], 3, axis=1)]
        cmp_bias = _cmp_bias(wt, p_pad)

        u = _inproj(hp, norm_g[l], scale_p, shift_p, w_pad, seq)
        ut = _inproj_t(hp, norm_g[l], scale_p, shift_p, wt_sel, seq)
        cmp_rows = u[:, C_CMP:C_CMP + 256]
        logf, c3 = _logf_prompt(u, bf_pad, batch, seq)
        o_fox = _fox_prompt(u, ut, c3, batch, seq)
        chunk = min(2 * RET_CHUNK, seq)
        o_ret, ret_state = _retention(u, zero_state, lg_lanes, batch, seq, chunk, chunk)
        pre_t = _cmp_pre_prompt(cmp_rows.reshape(batch, seq // CMP_STRIDE, CMP_FEAT), wt)
        kvt, kc = _cmp_post(pre_t, cmp_bias, w2t)
        o_nsa = _nsa_prompt(u, ut, kc, kvt, batch, seq)
        res = _outproj(u, o_nsa, o_fox, o_ret, hp, gate_p, wb_bf, wo_bf, final_g, seq, final)
        hp = res[0]
        y_prompt = res[-1]
        win_rows = u[:, C_WIN:C_WIN + 256].reshape(batch, seq, 2, NSA_KV, HEAD_DIM)
        win_state = jnp.concatenate(
            [jnp.zeros((batch, wb_len) + win_rows.shape[2:], F32), win_rows], axis=1)[:, -wb_len:]
        outs_p.append((cmp_rows.reshape(batch, seq, 2, NSA_KV, HEAD_DIM),
                       u[:, C_SEL:C_SEL + 256].reshape(batch, seq, 2, NSA_KV, HEAD_DIM),
                       win_state,
                       u[:, C_FKV:C_FKV + 1024].reshape(batch, seq, 2, FOX_HEADS, HEAD_DIM),
                       logf, ret_state))

        us = _inproj(hs, norm_g[l], scale_s, shift_s, w_pad, srows)
        kvt_s, _ = _cmp_post(_cmp_pre_paged(pool_cmp, l, page_table, wt), cmp_bias, w2t)
        o_cmp_s, selm = _nsa_cmp_sample(us, kvt_s, nseq, past, n_sel, blk_pad)
        o_nsa_s = _nsa_selwin_sample(us, pool_sel, l, page_table, selm, o_cmp_s, win_buf, past)
        c_past = _logf_past(pool_logf, l, page_table)
        o_fox_s, logf_s = _fox_sample(us, pool_fox, l, page_table, c_past, bf_pad)
        o_ret_s, ret_state_s = _retention(us, state_ret[l], lg_lanes, nseq, SAMPLE_ROWS, SAMPLE_ROWS, dec)
        res = _outproj(us, o_nsa_s, o_fox_s, o_ret_s, hs, gate_s, wb_bf, wo_bf, final_g, srows, final)
        hs = res[0]
        y_sample = res[-1]

        def new_rows(c0, width, shape):
            return us[:, c0:c0 + width].reshape(nseq, SAMPLE_ROWS, width)[:, :dec].reshape((nseq, dec) + shape)

        win_new = new_rows(C_WIN, 256, (2, NSA_KV, HEAD_DIM))
        win_all = jnp.concatenate([cache_nsa_win_kv[l], win_new], axis=1)
        outs_s.append((new_rows(C_CMP, 256, (2, NSA_KV, HEAD_DIM)),
                       new_rows(C_SEL, 256, (2, NSA_KV, HEAD_DIM)),
                       win_all[:, dec:],
                       new_rows(C_FKV, 1024, (2, FOX_HEADS, HEAD_DIM)),
                       logf_s.reshape(nseq, SAMPLE_ROWS, LANES)[:, :dec, :FOX_HEADS],
                       ret_state_s))

    def stacked(rows, i):
        return jnp.stack([r[i] for r in rows], axis=0)

    y_prompt = y_prompt.reshape(batch, seq, d)
    y_sample = y_sample.reshape(nseq, SAMPLE_ROWS, d)[:, :dec]
    return (y_prompt, y_sample,
            stacked(outs_p, 0), stacked(outs_s, 0), stacked(outs_p, 1), stacked(outs_s, 1),
            stacked(outs_p, 2), stacked(outs_s, 2), stacked(outs_p, 3), stacked(outs_s, 3),
            stacked(outs_p, 4), stacked(outs_s, 4), stacked(outs_p, 5), stacked(outs_s, 5))
```

```python
import functools

import jax
import jax.numpy as jnp
from jax import lax
from jax.experimental import pallas as pl
from jax.experimental.pallas import tpu as pltpu

F32 = jnp.float32
BF16 = jnp.bfloat16

D_MODEL = 1024
HEAD_DIM = 64
NSA_KV = 2
NSA_REP = 4
CMP_LEN = 32
CMP_STRIDE = 16
CMP_HIDDEN = 128
SEL_BLOCK = 64
SEL_TOPK = 16
WINDOW = 512
FOX_HEADS = 8
RET_HEADS = 8
RET_CHUNK = 128
EPS = 1e-6
NEG_INF = -1e30
FORCE = 1e9
LANES = 128
PAGE = 128
SAMPLE_ROWS = 8
VMEM_LIMIT = 48 * 1024 * 1024

C_NQ, C_FQ, C_RQ, C_RK, C_RV = 0, 512, 1024, 1536, 2048
C_NZ, C_FZ, C_RZ = 2560, 3072, 3584
C_FKV = 4096
C_MG = 5120
C_CMP, C_SEL, C_WIN = 8192, 8448, 8704
C_NG, C_FF = 8960, 9088
D_PAD = 9216
T_NQ, T_FQ, T_FV, T_SELV, T_WINV, T_NG = 0, 512, 1024, 1536, 1664, 1792
T_ROWS = 1920
T_TILE = 384
CMP_FEAT = CMP_STRIDE * 4 * HEAD_DIM


def _params(sem, vmem=VMEM_LIMIT):
    return pltpu.CompilerParams(dimension_semantics=sem, vmem_limit_bytes=vmem)


def _nt(a, b):
    return lax.dot_general(a, b, (((1,), (1,)), ((), ())), preferred_element_type=F32)


def _tn(a, b):
    return lax.dot_general(a, b, (((0,), (0,)), ((), ())), preferred_element_type=F32)


def _dot(a, b):
    return jnp.dot(a, b, preferred_element_type=F32)


def _split3(x):
    hi = x.astype(BF16)
    r1 = x - hi.astype(F32)
    mid = r1.astype(BF16)
    lo = (r1 - mid.astype(F32)).astype(BF16)
    return hi, mid, lo


def _silu(x):
    return x * jax.nn.sigmoid(x)


def _log_sigmoid(x):
    return jnp.minimum(x, 0.0) - jnp.log(1.0 + jnp.exp(-jnp.abs(x)))


def _iota(shape, dim):
    return lax.broadcasted_iota(jnp.int32, shape, dim)


def _div(x, n):
    return lax.shift_right_arithmetic(x, jnp.int32(n.bit_length() - 1))


def _mod(x, n):
    return x & (n - 1)


def _lane_tile(x, n):
    return x if n == 1 else jnp.concatenate([x] * n, axis=1)


def _adaln_kernel(c_ref, w_ref, b_ref, o_ref):
    c = c_ref[...]
    o_ref[...] = _dot(_silu(c).astype(BF16), w_ref[...]) + b_ref[...]


def _adaln(c_all, w_bf, b):
    rows, d = c_all.shape
    n = w_bf.shape[1]
    tn = 1024
    return pl.pallas_call(
        _adaln_kernel,
        out_shape=jax.ShapeDtypeStruct((rows, n), F32),
        grid=(n // tn,),
        in_specs=[pl.BlockSpec((rows, d), lambda j: (0, 0)),
                  pl.BlockSpec((d, tn), lambda j: (0, j)),
                  pl.BlockSpec((1, tn), lambda j: (0, j))],
        out_specs=pl.BlockSpec((rows, tn), lambda j: (0, j)),
        compiler_params=_params(("arbitrary",)),
        name="adaln",
    )(c_all, w_bf, b.reshape(1, n))


def _inproj_kernel(x_ref, g_ref, sc_ref, sh_ref, w_ref, o_ref, h_ref, *, transposed):
    @pl.when(pl.program_id(1) == 0)
    def _():
        x = x_ref[...]
        ms = jnp.mean(x * x, axis=-1, keepdims=True)
        y = x * lax.rsqrt(ms + EPS) * g_ref[...]
        h_ref[...] = (y * (1.0 + sc_ref[...]) + sh_ref[...]).astype(BF16)

    o_ref[...] = _nt(w_ref[...], h_ref[...]) if transposed else _dot(h_ref[...], w_ref[...])


def _mod_spec(mod, tm, rows_per_mod, d, nargs):
    if mod.ndim == 2:
        return pl.BlockSpec((tm, d), (lambda i, j: (i, 0)) if nargs == 2 else (lambda i: (i, 0)))
    per = rows_per_mod // tm
    return pl.BlockSpec((None, 1, d), (lambda i, j: (i // per, 0, 0)) if nargs == 2 else (lambda i: (i // per, 0, 0)))


def _inproj(x2d, g, scale, shift, w_pad, rows_per_mod):
    n, d = x2d.shape
    tm = min(1024, rows_per_mod, n)
    tn = 512
    return pl.pallas_call(
        functools.partial(_inproj_kernel, transposed=False),
        out_shape=jax.ShapeDtypeStruct((n, D_PAD), F32),
        grid=(n // tm, D_PAD // tn),
        in_specs=[pl.BlockSpec((tm, d), lambda i, j: (i, 0)),
                  pl.BlockSpec((1, d), lambda i, j: (0, 0)),
                  _mod_spec(scale, tm, rows_per_mod, d, 2),
                  _mod_spec(shift, tm, rows_per_mod, d, 2),
                  pl.BlockSpec((d, tn), lambda i, j: (0, j))],
        out_specs=pl.BlockSpec((tm, tn), lambda i, j: (i, j)),
        scratch_shapes=[pltpu.VMEM((tm, d), BF16)],
        compiler_params=_params(("arbitrary", "arbitrary")),
        name="inproj",
    )(x2d, g.reshape(1, d), scale, shift, w_pad)


def _inproj_t(x2d, g, scale, shift, wt_sel, rows_per_mod, tn=T_TILE, per_batch=False):
    n, d = x2d.shape
    rows = wt_sel.shape[0]
    tm = min(1024, rows_per_mod, n)
    per = rows_per_mod // tm
    if per_batch:
        out_shape = jax.ShapeDtypeStruct((n // rows_per_mod, rows, rows_per_mod), F32)
        out_spec = pl.BlockSpec((None, tn, tm), lambda i, j: (i // per, j, i % per))
    else:
        out_shape = jax.ShapeDtypeStruct((rows, n), F32)
        out_spec = pl.BlockSpec((tn, tm), lambda i, j: (j, i))
    return pl.pallas_call(
        functools.partial(_inproj_kernel, transposed=True),
        out_shape=out_shape,
        grid=(n // tm, rows // tn),
        in_specs=[pl.BlockSpec((tm, d), lambda i, j: (i, 0)),
                  pl.BlockSpec((1, d), lambda i, j: (0, 0)),
                  _mod_spec(scale, tm, rows_per_mod, d, 2),
                  _mod_spec(shift, tm, rows_per_mod, d, 2),
                  pl.BlockSpec((tn, d), lambda i, j: (j, 0))],
        out_specs=out_spec,
        scratch_shapes=[pltpu.VMEM((tm, d), BF16)],
        compiler_params=_params(("arbitrary", "arbitrary")),
        name="inproj_rows" if per_batch else "inproj_t",
    )(x2d, g.reshape(1, d), scale, shift, wt_sel)


def _lane_cumsum(lft, carry):
    t = lft.shape[1]
    upper = (_iota((t, t), 0) <= _iota((t, t), 1)).astype(BF16)
    c = sum(_dot(p, upper) for p in _split3(lft)) + carry
    return c, carry + jnp.sum(lft, axis=1, keepdims=True)


def _head_major_cumsum(lf, carry):
    eye = (_iota((FOX_HEADS, LANES), 0) == _iota((FOX_HEADS, LANES), 1)).astype(BF16)
    lft = sum(_nt(eye, p) for p in _split3(lf))
    return _lane_cumsum(lft, carry)


def _logf_prompt_kernel(ff_ref, bf_ref, logf_ref, c_ref, carry_ref):
    @pl.when(pl.program_id(1) == 0)
    def _():
        carry_ref[...] = jnp.zeros_like(carry_ref)

    lf = _log_sigmoid(ff_ref[...] + bf_ref[...])
    logf_ref[...] = lf[:, :FOX_HEADS]
    c, carry = _head_major_cumsum(lf, carry_ref[...])
    carry_ref[...] = carry
    pieces = [p.astype(F32) for p in _split3(c)]
    tb = c.shape[1]
    c3t = jnp.concatenate(pieces + [jnp.zeros((LANES - 3 * FOX_HEADS, tb), F32)], axis=0)
    c_ref[...] = c3t.T.astype(BF16)


def _logf_prompt(u, bf_pad, batch, seq):
    tb = min(512, seq)
    nt = seq // tb
    return pl.pallas_call(
        _logf_prompt_kernel,
        out_shape=(jax.ShapeDtypeStruct((batch, seq, FOX_HEADS), F32),
                   jax.ShapeDtypeStruct((batch * seq, LANES), BF16)),
        grid=(batch, nt),
        in_specs=[pl.BlockSpec((tb, LANES), lambda b, i: (b * nt + i, C_FF // LANES)),
                  pl.BlockSpec((1, LANES), lambda b, i: (0, 0))],
        out_specs=(pl.BlockSpec((None, tb, FOX_HEADS), lambda b, i: (b, i, 0)),
                   pl.BlockSpec((tb, LANES), lambda b, i: (b * nt + i, 0))),
        scratch_shapes=[pltpu.VMEM((FOX_HEADS, 1), F32)],
        compiler_params=_params(("arbitrary", "arbitrary")),
        name="logf_prompt",
    )(u, bf_pad)


def _logf_past_kernel(pt_ref, *refs, npg):
    pages, (c_ref, carry_ref) = refs[:npg], refs[npg:]

    @pl.when(pl.program_id(1) == 0)
    def _():
        carry_ref[...] = jnp.zeros_like(carry_ref)

    x = jnp.concatenate([pages[k][...] for k in range(npg)], axis=0)
    rows = npg * FOX_HEADS
    local, _ = _lane_cumsum(x, 0.0)
    tot = jnp.sum(x, axis=1, keepdims=True)
    r, c = _iota((rows, rows), 0), _iota((rows, rows), 1)
    earlier = ((_mod(c, FOX_HEADS) == _mod(r, FOX_HEADS)) & (c < r)).astype(BF16)
    tot_l = jnp.broadcast_to(tot, (rows, PAGE))
    offs = sum(_dot(earlier, p) for p in _split3(tot_l))
    carry = carry_ref[...]
    cum = local + offs + jnp.concatenate([carry] * npg, axis=0)
    for k in range(npg):
        c_ref[:, k * PAGE:(k + 1) * PAGE] = cum[k * FOX_HEADS:(k + 1) * FOX_HEADS, :]
    carry_ref[...] = carry + sum(tot[k * FOX_HEADS:(k + 1) * FOX_HEADS, :] for k in range(npg))


def _logf_past(pool_logf_t, layer, page_table, npg=16):
    nseq, n_pages = page_table.shape
    npg = min(npg, n_pages)
    steps = n_pages // npg

    def page_spec(k):
        return pl.BlockSpec((None, None, FOX_HEADS, PAGE), lambda b, j, pt: (layer, pt[b, j * npg + k], 0, 0))

    grid_spec = pltpu.PrefetchScalarGridSpec(
        num_scalar_prefetch=1, grid=(nseq, steps),
        in_specs=[page_spec(k) for k in range(npg)],
        out_specs=pl.BlockSpec((None, FOX_HEADS, npg * PAGE), lambda b, j, pt: (b, 0, j)),
        scratch_shapes=[pltpu.VMEM((FOX_HEADS, 1), F32)])
    return pl.pallas_call(
        functools.partial(_logf_past_kernel, npg=npg),
        out_shape=jax.ShapeDtypeStruct((nseq, FOX_HEADS, n_pages * PAGE), F32),
        grid_spec=grid_spec,
        compiler_params=_params(("arbitrary", "arbitrary")),
        name="logf_past",
    )(page_table, *([pool_logf_t] * npg))


def _fox_prompt_kernel(ii_ref, jj_ref, qt_ref, k_ref, c3_ref, vt_ref, o_ref, qaug_ref, m_ref, l_ref, acc_ref):
    p = pl.program_id(2)
    i, j = ii_ref[p], jj_ref[p]
    tq, tk = qt_ref.shape[1], k_ref.shape[0]
    hp = pl.program_id(1)

    @pl.when(j == 0)
    def _():
        qt = qt_ref[...] * (HEAD_DIM ** -0.5)
        row = _iota(qt.shape, 0)
        for h in range(2):
            head = 2 * hp + h
            piece_row = (row == head) | (row == head + FOX_HEADS) | (row == head + 2 * FOX_HEADS)
            qaug_ref[h] = jnp.concatenate(
                [jnp.where((row < HEAD_DIM) == (h == 0), qt, 0.0),
                 jnp.where(piece_row, -1.0, 0.0)], axis=0).astype(BF16)
        m_ref[...] = jnp.full_like(m_ref, NEG_INF)
        l_ref[...] = jnp.zeros_like(l_ref)
        acc_ref[...] = jnp.zeros_like(acc_ref)

    def step(masked):
        kaug = jnp.concatenate([k_ref[...].astype(BF16), c3_ref[...]], axis=1)
        vt = vt_ref[...].astype(BF16)
        if masked:
            keep = _iota((tk, tq), 0) <= _iota((tk, tq), 1)
        for h in range(2):
            st = _dot(kaug, qaug_ref[h])
            if masked:
                st = jnp.where(keep, st, -jnp.inf)
            m_prev = m_ref[h]
            m_new = jnp.maximum(m_prev, jnp.max(st, axis=0, keepdims=True))
            alpha = jnp.exp(m_prev - m_new)
            pt = jnp.exp(st - m_new)
            l_ref[h] = alpha * l_ref[h] + jnp.sum(pt, axis=0, keepdims=True)
            acc_ref[h] = alpha * acc_ref[h] + _dot(vt[h * HEAD_DIM:(h + 1) * HEAD_DIM, :], pt.astype(BF16))
            m_ref[h] = m_new

    pl.when(j < i)(lambda: step(False))

    @pl.when(j == i)
    def _():
        step(True)
        ot = jnp.concatenate([acc_ref[h] / jnp.maximum(l_ref[h], 1e-30) for h in range(2)], axis=0)
        o_ref[...] = ot.T


def _fox_prompt(u, ut, c3, batch, seq):
    tq = min(512, seq)
    nq = seq // tq
    pairs = FOX_HEADS // 2
    ii = jnp.asarray([i for i in range(nq) for _ in range(i + 1)], jnp.int32)
    jj = jnp.asarray([j for i in range(nq) for j in range(i + 1)], jnp.int32)
    grid_spec = pltpu.PrefetchScalarGridSpec(
        num_scalar_prefetch=2, grid=(batch, pairs, ii.shape[0]),
        in_specs=[pl.BlockSpec((LANES, tq), lambda b, hp, p, ii, jj: (T_FQ // LANES + hp, b * nq + ii[p])),
                  pl.BlockSpec((tq, LANES), lambda b, hp, p, ii, jj: (b * nq + jj[p], C_FKV // LANES + hp)),
                  pl.BlockSpec((tq, LANES), lambda b, hp, p, ii, jj: (b * nq + jj[p], 0)),
                  pl.BlockSpec((LANES, tq), lambda b, hp, p, ii, jj: (T_FV // LANES + hp, b * nq + jj[p]))],
        out_specs=pl.BlockSpec((tq, LANES), lambda b, hp, p, ii, jj: (b * nq + ii[p], hp)),
        scratch_shapes=[pltpu.VMEM((2, 2 * LANES, tq), BF16), pltpu.VMEM((2, 1, tq), F32),
                        pltpu.VMEM((2, 1, tq), F32), pltpu.VMEM((2, HEAD_DIM, tq), F32)])
    return pl.pallas_call(
        _fox_prompt_kernel,
        out_shape=jax.ShapeDtypeStruct((batch * seq, FOX_HEADS * HEAD_DIM), F32),
        grid_spec=grid_spec,
        compiler_params=_params(("arbitrary",) * 3),
        name="fox_prompt",
    )(ii, jj, ut, u, c3, ut)


def _retention_kernel(q_ref, k_ref, v_ref, lg_ref, s0_ref, o_ref, s_ref, sbd_ref, *, c_true):
    ci = pl.program_id(2)
    c = q_ref.shape[0]
    low_row = _iota((LANES, LANES), 0) < HEAD_DIM
    low_col = _iota((LANES, LANES), 1) < HEAD_DIM

    @pl.when(ci == 0)
    def _():
        sbd_ref[...] = s0_ref[...]

    lg = lg_ref[...]
    low = _iota((c, LANES), 1) < HEAD_DIM
    pos = _iota((c, LANES), 0).astype(F32)
    q = q_ref[...]
    k = k_ref[...] * (HEAD_DIM ** -0.5)
    v = v_ref[...].astype(BF16)
    kb = k.astype(BF16)
    diff = (_iota((c, c), 0) - _iota((c, c), 1)).astype(F32)
    inner = []
    for h in range(2):
        lgh = jnp.max(jnp.where(low[:1] == (h == 0), lg, -jnp.inf), axis=1, keepdims=True)
        decay = jnp.where(diff >= 0, jnp.exp(jnp.maximum(diff, 0.0) * lgh), 0.0)
        qm = jnp.where(low == (h == 0), q, 0.0).astype(BF16)
        scores = _nt(qm, kb) * decay
        inner.append(_dot(scores.astype(BF16), v))
    sbd = sbd_ref[...]
    cross = _dot((q * jnp.exp((pos + 1.0) * lg)).astype(BF16), sbd.astype(BF16))
    o = jnp.where(low, inner[0], inner[1]) + cross

    kd = jnp.where(pos < c_true, k * jnp.exp((c_true - 1.0 - pos) * lg), 0.0).astype(BF16)
    upd = jnp.where(low_row == low_col, _tn(kd, v), 0.0)
    sbd_new = jnp.exp(c_true * lg) * sbd + upd
    sbd_ref[...] = sbd_new

    inv = 1.0 / HEAD_DIM
    s_lo = jnp.sum(jnp.where(low, o, 0.0), axis=1, keepdims=True)
    s_hi = jnp.sum(jnp.where(low, 0.0, o), axis=1, keepdims=True)
    d = o - jnp.where(low, s_lo, s_hi) * inv
    d2 = d * d
    v_lo = jnp.sum(jnp.where(low, d2, 0.0), axis=1, keepdims=True)
    v_hi = jnp.sum(jnp.where(low, 0.0, d2), axis=1, keepdims=True)
    o_ref[...] = d * lax.rsqrt(jnp.where(low, v_lo, v_hi) * inv + EPS)

    @pl.when(ci == pl.num_programs(2) - 1)
    def _():
        s_ref[...] = sbd_new


def _to_block_diag(s):
    b = s.shape[0]
    s = s.reshape(b, RET_HEADS // 2, 2, HEAD_DIM, HEAD_DIM)
    z = jnp.zeros_like(s[:, :, 0])
    return jnp.concatenate([jnp.concatenate([s[:, :, 0], z], axis=-1),
                            jnp.concatenate([z, s[:, :, 1]], axis=-1)], axis=-2)


def _from_block_diag(sbd):
    b = sbd.shape[0]
    return jnp.stack([sbd[:, :, :HEAD_DIM, :HEAD_DIM], sbd[:, :, HEAD_DIM:, HEAD_DIM:]],
                     axis=2).reshape(b, RET_HEADS, HEAD_DIM, HEAD_DIM)


def _retention(u, state0, lg_lanes, batch, rows_per_seq, chunk, c_true):
    nc = rows_per_seq // chunk
    pairs = RET_HEADS // 2
    qcol, kcol, vcol = C_RQ // LANES, C_RK // LANES, C_RV // LANES
    o, sbd = pl.pallas_call(
        functools.partial(_retention_kernel, c_true=c_true),
        out_shape=(jax.ShapeDtypeStruct((batch * rows_per_seq, RET_HEADS * HEAD_DIM), F32),
                   jax.ShapeDtypeStruct((batch, pairs, LANES, LANES), F32)),
        grid=(batch, pairs, nc),
        in_specs=[pl.BlockSpec((chunk, LANES), lambda b, hp, ci: (b * nc + ci, qcol + hp)),
                  pl.BlockSpec((chunk, LANES), lambda b, hp, ci: (b * nc + ci, kcol + hp)),
                  pl.BlockSpec((chunk, LANES), lambda b, hp, ci: (b * nc + ci, vcol + hp)),
                  pl.BlockSpec((None, 1, LANES), lambda b, hp, ci: (hp, 0, 0)),
                  pl.BlockSpec((None, None, LANES, LANES), lambda b, hp, ci: (b, hp, 0, 0))],
        out_specs=(pl.BlockSpec((chunk, LANES), lambda b, hp, ci: (b * nc + ci, hp)),
                   pl.BlockSpec((None, None, LANES, LANES), lambda b, hp, ci: (b, hp, 0, 0))),
        scratch_shapes=[pltpu.VMEM((LANES, LANES), F32)],
        compiler_params=_params(("arbitrary",) * 3),
        name="retention",
    )(u, u, u, lg_lanes, _to_block_diag(state0))
    return o, _from_block_diag(sbd)


def _cmp_bias_kernel(wt_ref, p_ref, o_ref):
    r = _nt(wt_ref[...], p_ref[...].astype(BF16))
    part = _mod(_div(_iota(r.shape, 0), CMP_HIDDEN), 2)
    col = jnp.sum(jnp.where(_iota(r.shape, 1) == part, r, 0.0), axis=1, keepdims=True)
    o_ref[...] = jnp.broadcast_to(col, o_ref.shape)


def _cmp_bias(wt, p_pad):
    return pl.pallas_call(
        _cmp_bias_kernel,
        out_shape=jax.ShapeDtypeStruct((wt.shape[0], LANES), F32),
        compiler_params=_params(None),
        name="cmp_bias",
    )(wt, p_pad)


def _cmp_pre_kernel(x_ref, wt_ref, o_ref):
    o_ref[...] = _nt(wt_ref[...], x_ref[...].astype(BF16))


def _cmp_pre_paged_kernel(pt_ref, *refs, npg):
    pages, (wt_ref, o_ref, stage_ref) = refs[:npg], refs[npg:]
    spp = PAGE // CMP_STRIDE
    feat = 4 * HEAD_DIM
    j = _iota((PAGE, PAGE), 0)
    perm = (_iota((PAGE, PAGE), 1) == _mod(j, spp) * CMP_STRIDE + _div(j, spp)).astype(BF16)
    for k in range(npg):
        xp = _nt(perm, pages[k][...].astype(BF16))
        for s in range(CMP_STRIDE):
            stage_ref[s, k * spp:(k + 1) * spp, :] = xp[s * spp:(s + 1) * spp, :]
    acc = None
    for s in range(CMP_STRIDE):
        term = _nt(wt_ref[:, s * feat:(s + 1) * feat], stage_ref[s].astype(BF16))
        acc = term if acc is None else acc + term
    o_ref[...] = acc


def _cmp_pre_prompt(x_seg, wt):
    batch, n_seg, feat = x_seg.shape
    ts = min(128, n_seg)
    return pl.pallas_call(
        _cmp_pre_kernel,
        out_shape=jax.ShapeDtypeStruct((batch, wt.shape[0], n_seg), F32),
        grid=(batch, n_seg // ts),
        in_specs=[pl.BlockSpec((None, ts, feat), lambda b, j: (b, j, 0)),
                  pl.BlockSpec(wt.shape, lambda b, j: (0, 0))],
        out_specs=pl.BlockSpec((None, wt.shape[0], ts), lambda b, j: (b, 0, j)),
        compiler_params=_params(("arbitrary", "arbitrary")),
        name="cmp_pre_prompt",
    )(x_seg, wt)


def _cmp_pre_paged(pool_t, layer, page_table, wt, npg=32):
    nseq, n_pages = page_table.shape
    npg = min(npg, n_pages)
    spp = PAGE // CMP_STRIDE
    feat = pool_t.shape[2]

    def page_spec(k):
        return pl.BlockSpec((None, None, feat, PAGE), lambda b, j, pt: (layer, pt[b, j * npg + k], 0, 0))

    grid_spec = pltpu.PrefetchScalarGridSpec(
        num_scalar_prefetch=1, grid=(nseq, n_pages // npg),
        in_specs=[page_spec(k) for k in range(npg)] + [pl.BlockSpec(wt.shape, lambda b, j, pt: (0, 0))],
        out_specs=pl.BlockSpec((None, wt.shape[0], npg * spp), lambda b, j, pt: (b, 0, j)),
        scratch_shapes=[pltpu.VMEM((CMP_STRIDE, npg * spp, feat), F32)])
    return pl.pallas_call(
        functools.partial(_cmp_pre_paged_kernel, npg=npg),
        out_shape=jax.ShapeDtypeStruct((nseq, wt.shape[0], n_pages * spp), F32),
        grid_spec=grid_spec,
        compiler_params=_params(("arbitrary", "arbitrary")),
        name="cmp_pre_paged",
    )(page_table, *([pool_t] * npg), wt)


def _cmp_post_kernel(pre_ref, bias_ref, w2t_ref, o_ref, k_ref):
    n_seg = pre_ref.shape[1]
    reps = n_seg // LANES
    for c in range(4):
        base = c * 2 * CMP_HIDDEN
        lo = pre_ref[base:base + CMP_HIDDEN, :] + _lane_tile(bias_ref[base:base + CMP_HIDDEN, :], reps)
        hi = (pre_ref[base + CMP_HIDDEN:base + 2 * CMP_HIDDEN, :]
              + _lane_tile(bias_ref[base + CMP_HIDDEN:base + 2 * CMP_HIDDEN, :], reps))
        hid = _silu(lo + pltpu.roll(hi, n_seg - 1, 1))
        o_ref[c * HEAD_DIM:(c + 1) * HEAD_DIM, :] = _dot(w2t_ref[c], hid.astype(BF16))
    k_ref[...] = o_ref[0:2 * HEAD_DIM, :].T


def _cmp_post(pre_t, bias, w2t):
    batch, rows, n_seg = pre_t.shape
    return pl.pallas_call(
        _cmp_post_kernel,
        out_shape=(jax.ShapeDtypeStruct((batch, 4 * HEAD_DIM, n_seg), F32),
                   jax.ShapeDtypeStruct((batch, n_seg, 2 * HEAD_DIM), F32)),
        grid=(batch,),
        in_specs=[pl.BlockSpec((None, rows, n_seg), lambda b: (b, 0, 0)),
                  pl.BlockSpec(bias.shape, lambda b: (0, 0)),
                  pl.BlockSpec(w2t.shape, lambda b: (0, 0, 0))],
        out_specs=(pl.BlockSpec((None, 4 * HEAD_DIM, n_seg), lambda b: (b, 0, 0)),
                   pl.BlockSpec((None, n_seg, 2 * HEAD_DIM), lambda b: (b, 0, 0))),
        compiler_params=_params(("arbitrary",)),
        name="cmp_post",
    )(pre_t, bias, w2t)


def _group_queries(q, g, rows):
    low = _iota((rows, LANES), 1) < HEAD_DIM
    out = []
    for r in range(NSA_REP):
        chunk = 2 * g + r // 2
        x = q[:, chunk * LANES:(chunk + 1) * LANES] * (HEAD_DIM ** -0.5)
        x = jnp.where(low == (r % 2 == 0), x, 0.0)
        if r % 2 != g:
            x = pltpu.roll(x, HEAD_DIM, 1)
        out.append(x)
    return jnp.concatenate(out, axis=0).astype(BF16)


def _slope_col(g, rows):
    r = _div(_iota((NSA_REP * rows, 1), 0), rows)
    return jnp.exp2(-(r + (NSA_REP * g + 1)).astype(F32))


def _softmax_rows(s, valid):
    s = jnp.where(valid, s, NEG_INF)
    m = jnp.max(s, axis=1, keepdims=True)
    p = jnp.where(valid, jnp.exp(s - m), 0.0)
    return p / jnp.maximum(jnp.sum(p, axis=1, keepdims=True), 1e-30)


def _tile_rows(x, n):
    return jnp.concatenate([x] * n, axis=0)


def _cover(n_cmp_pad, n_blk_pad):
    n = _iota((n_cmp_pad, n_blk_pad), 0) * CMP_STRIDE
    j = _iota((n_cmp_pad, n_blk_pad), 1) * SEL_BLOCK
    return ((n < j + SEL_BLOCK) & (n + CMP_LEN - 1 >= j)).astype(BF16)


def _select_blocks(imp, tpos, n_sel, axis=1):
    blk = _iota(imp.shape, axis)
    cur = _div(tpos, SEL_BLOCK)
    forced = (blk == 0) | (blk == cur) | (blk == cur - 1)
    imp = jnp.where(forced, FORCE, imp)
    imp = jnp.where(blk * SEL_BLOCK <= tpos, imp, NEG_INF)
    imp = jnp.where(blk < n_sel, imp, -jnp.inf)
    blkf = blk.astype(F32)

    def body(_, carry):
        imp, sel = carry
        m = jnp.max(imp, axis=axis, keepdims=True)
        idx = jnp.min(jnp.where(imp == m, blkf, 1e9), axis=axis, keepdims=True)
        hit = blkf == idx
        return jnp.where(hit, -jnp.inf, imp), jnp.where(hit, 1.0, sel)

    _, sel = lax.fori_loop(0, min(SEL_TOPK, n_sel), body, (imp, jnp.zeros_like(imp)))
    return sel


def _cmp_branch(qm, kct, vct, dist, valid, slope, rows):
    s = _dot(qm, kct) - slope * _tile_rows(dist, NSA_REP)
    p = _softmax_rows(s, _tile_rows(valid, NSA_REP))
    o = _nt(p.astype(BF16), vct)
    psum = p[0:rows] + p[rows:2 * rows] + p[2 * rows:3 * rows] + p[3 * rows:4 * rows]
    return o, psum


def _importance(psum, cover):
    hi = psum.astype(BF16)
    lo = (psum - hi.astype(F32)).astype(BF16)
    return _dot(hi, cover) + _dot(lo, cover)


def _gate_cols(gates, g, rows):
    lane = _iota(gates.shape, 1)
    cols = []
    for c in range(3):
        per_head = [jnp.sum(jnp.where(lane == (g * NSA_REP + r) * 3 + c, gates, 0.0), axis=1, keepdims=True)
                    for r in range(NSA_REP)]
        cols.append(jnp.concatenate(per_head, axis=0))
    return cols


def _place_group(o, g, rows):
    low = _iota((rows, LANES), 1) < HEAD_DIM
    chunks = []
    for kk in range(2):
        a = o[(2 * kk) * rows:(2 * kk + 1) * rows]
        b = o[(2 * kk + 1) * rows:(2 * kk + 2) * rows]
        if g == 1:
            a = pltpu.roll(a, HEAD_DIM, 1)
        else:
            b = pltpu.roll(b, HEAD_DIM, 1)
        chunks.append(jnp.where(low, a, b))
    return chunks


def _online_update(s, v, m_ref, l_ref, acc_ref, v_transposed=False):
    m_prev = m_ref[...]
    m_new = jnp.maximum(m_prev, jnp.max(s, axis=1, keepdims=True))
    alpha = jnp.exp(m_prev - m_new)
    p = jnp.exp(s - m_new)
    l_ref[...] = alpha * l_ref[...] + jnp.sum(p, axis=1, keepdims=True)
    pb = p.astype(BF16)
    acc_ref[...] = alpha * acc_ref[...] + (_nt(pb, v) if v_transposed else _dot(pb, v))
    m_ref[...] = m_new


def _softmax_cols(st, valid):
    st = jnp.where(valid, st, NEG_INF)
    m = jnp.max(st, axis=0, keepdims=True)
    p = jnp.where(valid, jnp.exp(st - m), 0.0)
    return p / jnp.maximum(jnp.sum(p, axis=0, keepdims=True), 1e-30)


def _online_update_t(st, vt, m_ref, l_ref, acc_ref):
    m_prev = m_ref[...]
    m_new = jnp.maximum(m_prev, jnp.max(st, axis=0, keepdims=True))
    alpha = jnp.exp(m_prev - m_new)
    pt = jnp.exp(st - m_new)
    l_ref[...] = alpha * l_ref[...] + jnp.sum(pt, axis=0, keepdims=True)
    acc_ref[...] = alpha * acc_ref[...] + _dot(vt, pt.astype(BF16))
    m_ref[...] = m_new


def _nsa_prompt_kernel(qt_ref, gt_ref, kc_ref, kvt_ref, selk_ref, selvt_ref, *rest, n_cmp, tk):
    nwin = WINDOW // LANES + 1
    wk, wvt = rest[:nwin], rest[nwin:2 * nwin]
    o_ref, kaug_ref, selvt_bf, qaug_ref, selt_ref, m_ref, l_ref, acc_ref = rest[2 * nwin:]
    i = pl.program_id(1)
    tq = qt_ref.shape[1]
    seq = selk_ref.shape[0]
    n_seg = kc_ref.shape[0]
    n_blk = seq // SEL_BLOCK
    blk_pad = max(LANES, n_blk)
    bpt = tk // SEL_BLOCK
    t0 = i * tq

    @pl.when(i == 0)
    def _():
        local = _iota((tk, LANES), 0)
        lane = _iota((tk, LANES), 1)
        ext = jnp.where(lane < bpt, (_div(local, SEL_BLOCK) == lane).astype(F32),
                        jnp.where(lane == bpt, (local - _mod(local, 2)).astype(F32),
                                  jnp.where(lane == bpt + 1, _mod(local, 2).astype(F32),
                                            jnp.where(lane == bpt + 2, 1.0, 0.0)))).astype(BF16)
        for c in range(seq // tk):
            kaug_ref[c * tk:(c + 1) * tk, :] = jnp.concatenate(
                [selk_ref[c * tk:(c + 1) * tk, :].astype(BF16), ext], axis=1)
            selvt_bf[c] = selvt_ref[:, c * tk:(c + 1) * tk].astype(BF16)

    tpos = t0 + _iota((1, tq), 1)
    kc = kc_ref[...].astype(BF16)
    vct = kvt_ref[2 * HEAD_DIM:4 * HEAD_DIM, :].astype(BF16)
    n_col = _iota((n_seg, 1), 0)
    dist_c = (tpos - (n_col * CMP_STRIDE + CMP_LEN - 1)).astype(F32)
    dist_c = _lane_tile(dist_c, NSA_REP)
    valid_c = (dist_c >= 0) & (n_col < n_cmp)
    blk = _iota((blk_pad, n_seg), 0) * SEL_BLOCK
    seg = _iota((blk_pad, n_seg), 1) * CMP_STRIDE
    cover_t = ((seg < blk + SEL_BLOCK) & (seg + CMP_LEN - 1 >= blk)).astype(BF16)
    gates_t = jax.nn.sigmoid(gt_ref[...])
    qt = qt_ref[...] * (HEAD_DIM ** -0.5)

    kw = jnp.concatenate([w[...] for w in wk], axis=0).astype(BF16)
    vwt = jnp.concatenate([w[...] for w in wvt], axis=1).astype(BF16)
    wpos = t0 - WINDOW + _iota((kw.shape[0], 1), 0)
    dist_w = (tpos - wpos).astype(F32)
    dist_w = _lane_tile(dist_w, NSA_REP)
    valid_w = (dist_w >= 0) & (dist_w < WINDOW) & (wpos >= 0)

    n_tiles = _div(t0 + tq + tk - 1, tk)
    zero = jnp.zeros((HEAD_DIM, tq), F32)
    out_rows = []
    for g in range(NSA_KV):
        heads = [qt[(g * NSA_REP + r) * HEAD_DIM:(g * NSA_REP + r + 1) * HEAD_DIM, :] for r in range(NSA_REP)]
        qmt = jnp.concatenate([jnp.concatenate([h, zero] if g == 0 else [zero, h], axis=0) for h in heads],
                              axis=1).astype(BF16)
        slope = jnp.concatenate([jnp.full((1, tq), 2.0 ** -(NSA_REP * g + r + 1), F32) for r in range(NSA_REP)],
                                axis=1)

        pt = _softmax_cols(_dot(kc, qmt) - slope * dist_c, valid_c)
        o_cmp = _dot(vct, pt.astype(BF16))
        psum = pt[:, 0:tq] + pt[:, tq:2 * tq] + pt[:, 2 * tq:3 * tq] + pt[:, 3 * tq:4 * tq]
        hi = psum.astype(BF16)
        lo = (psum - hi.astype(F32)).astype(BF16)
        sel_t = _select_blocks(_dot(cover_t, hi) + _dot(cover_t, lo), tpos, n_blk, axis=0)
        selt_ref[...] = sel_t
        qaug_ref[0:LANES, :] = qmt
        qaug_ref[LANES + 16:2 * LANES, :] = jnp.zeros((LANES - 16, NSA_REP * tq), BF16)

        m_ref[...] = jnp.full_like(m_ref, -1e29)
        l_ref[...] = jnp.zeros_like(l_ref)
        acc_ref[...] = jnp.zeros_like(acc_ref)
        row8 = _iota((8, NSA_REP * tq), 0)
        tpos4 = _lane_tile(tpos, NSA_REP)

        def tile(j, _):
            picked = selt_ref[pl.ds(pl.multiple_of(j * bpt, bpt), bpt), :]

            def attend(diagonal):
                shift = (j * tk - t0).astype(F32)
                mask_rows = (_lane_tile(picked, NSA_REP) - 1.0) * 1e30
                bias_rows = jnp.where(row8 < 2, slope, jnp.where(row8 == 2, slope * shift, 0.0))
                qaug_ref[LANES:LANES + 16, :] = jnp.concatenate([mask_rows, bias_rows], axis=0).astype(BF16)
                r = pl.multiple_of(j * tk, tk)
                st = _dot(kaug_ref[pl.ds(r, tk), :], qaug_ref[...])
                if diagonal:
                    st = jnp.where(j * tk + _iota((tk, 1), 0) <= tpos4, st, -jnp.inf)
                _online_update_t(st, selvt_bf[j], m_ref, l_ref, acc_ref)

            active = jnp.max(picked) > 0.5
            pl.when(active & (j < n_tiles - 1))(lambda: attend(False))
            pl.when(active & (j == n_tiles - 1))(lambda: attend(True))
            return 0

        lax.fori_loop(0, n_tiles, tile, 0)
        o_sel = acc_ref[...] / jnp.maximum(l_ref[...], 1e-30)

        o_win = _dot(vwt, _softmax_cols(_dot(kw, qmt) - slope * dist_w, valid_w).astype(BF16))

        def gate_row(c):
            return jnp.concatenate([gates_t[(g * NSA_REP + r) * 3 + c:(g * NSA_REP + r) * 3 + c + 1, :]
                                    for r in range(NSA_REP)], axis=1)

        o = gate_row(0) * o_cmp + gate_row(1) * o_sel + gate_row(2) * o_win
        out_rows += [o[g * HEAD_DIM:(g + 1) * HEAD_DIM, r * tq:(r + 1) * tq] for r in range(NSA_REP)]
    o_ref[...] = jnp.concatenate(out_rows, axis=0).T


def _nsa_prompt(u, ut, kc, kvt, batch, seq):
    tq = LANES
    nq = seq // tq
    tk = min(512, seq)
    n_seg = kvt.shape[2]
    n_cmp = seq // CMP_STRIDE - 1
    nwin = WINDOW // tq + 1

    def win_row(b, i, k):
        return b * nq + jnp.maximum(i - WINDOW // tq + k, 0)

    return pl.pallas_call(
        functools.partial(_nsa_prompt_kernel, n_cmp=n_cmp, tk=tk),
        out_shape=jax.ShapeDtypeStruct((batch * seq, 512), F32),
        grid=(batch, nq),
        in_specs=[pl.BlockSpec((512, tq), lambda b, i: (T_NQ // 512, b * nq + i)),
                  pl.BlockSpec((LANES, tq), lambda b, i: (T_NG // LANES, b * nq + i)),
                  pl.BlockSpec((None, n_seg, LANES), lambda b, i: (b, 0, 0)),
                  pl.BlockSpec((None, 4 * HEAD_DIM, n_seg), lambda b, i: (b, 0, 0)),
                  pl.BlockSpec((seq, LANES), lambda b, i: (b, C_SEL // LANES)),
                  pl.BlockSpec((LANES, seq), lambda b, i: (T_SELV // LANES, b))]
                 + [pl.BlockSpec((tq, LANES), lambda b, i, k=k: (win_row(b, i, k), C_WIN // LANES))
                    for k in range(nwin)]
                 + [pl.BlockSpec((LANES, tq), lambda b, i, k=k: (T_WINV // LANES, win_row(b, i, k)))
                    for k in range(nwin)],
        out_specs=pl.BlockSpec((tq, 512), lambda b, i: (b * nq + i, 0)),
        scratch_shapes=[pltpu.VMEM((seq, 2 * LANES), BF16), pltpu.VMEM((seq // tk, LANES, tk), BF16),
                        pltpu.VMEM((2 * LANES, NSA_REP * tq), BF16),
                        pltpu.VMEM((max(LANES, seq // SEL_BLOCK), tq), F32),
                        pltpu.VMEM((1, NSA_REP * tq), F32), pltpu.VMEM((1, NSA_REP * tq), F32),
                        pltpu.VMEM((LANES, NSA_REP * tq), F32)],
        compiler_params=_params(("arbitrary", "arbitrary")),
        name="nsa_prompt",
    )(ut, ut, kc, kvt, u, ut, *([u] * nwin), *([ut] * nwin))


def _nsa_cmp_sample_kernel(q_ref, kvt_ref, o_ref, sel_ref, *, past, n_cmp, n_sel):
    rows = SAMPLE_ROWS
    n_seg = kvt_ref.shape[1]
    blk_pad = sel_ref.shape[1]
    tpos = past + _iota((rows, 1), 0)
    kct = kvt_ref[0:2 * HEAD_DIM, :].astype(BF16)
    vct = kvt_ref[2 * HEAD_DIM:4 * HEAD_DIM, :].astype(BF16)
    n_idx = _iota((rows, n_seg), 1)
    dist = (tpos - (n_idx * CMP_STRIDE + CMP_LEN - 1)).astype(F32)
    valid = (dist >= 0) & (n_idx < n_cmp)
    cover = _cover(n_seg, blk_pad)
    q = q_ref[...]
    for g in range(NSA_KV):
        qm = _group_queries(q, g, rows)
        o, psum = _cmp_branch(qm, kct, vct, dist, valid, _slope_col(g, rows), rows)
        o_ref[g * NSA_REP * rows:(g + 1) * NSA_REP * rows, :] = o
        sel_ref[g * rows:(g + 1) * rows, :] = _select_blocks(_importance(psum, cover), tpos, n_sel)


def _nsa_cmp_sample(u_s, kvt, nseq, past, n_sel, blk_pad):
    n_seg = kvt.shape[2]
    return pl.pallas_call(
        functools.partial(_nsa_cmp_sample_kernel, past=past, n_cmp=n_seg - 1, n_sel=n_sel),
        out_shape=(jax.ShapeDtypeStruct((nseq, NSA_KV * NSA_REP * SAMPLE_ROWS, LANES), F32),
                   jax.ShapeDtypeStruct((nseq, NSA_KV * SAMPLE_ROWS, blk_pad), F32)),
        grid=(nseq,),
        in_specs=[pl.BlockSpec((SAMPLE_ROWS, 512), lambda b: (b, C_NQ // 512)),
                  pl.BlockSpec((None, 4 * HEAD_DIM, n_seg), lambda b: (b, 0, 0))],
        out_specs=(pl.BlockSpec((None, NSA_KV * NSA_REP * SAMPLE_ROWS, LANES), lambda b: (b, 0, 0)),
                   pl.BlockSpec((None, NSA_KV * SAMPLE_ROWS, blk_pad), lambda b: (b, 0, 0))),
        compiler_params=_params(("arbitrary",)),
        name="nsa_cmp_sample",
    )(u_s, kvt)


def _stack_groups(fn):
    return jnp.concatenate([fn(g) for g in range(NSA_KV)], axis=0)


def _nsa_selwin_sample_kernel(pt_ref, src_ref, act_ref, *refs, npg, past):
    pages = refs[:npg]
    (q_ref, gate_ref, selm_ref, ocmp_ref, winbuf_ref, selnew_ref, winnew_ref, o_ref,
     qm_ref, selrows_ref, m_ref, l_ref, acc_ref, pad_ref, kt_ref, vt_ref) = refs[npg:]
    j = pl.program_id(1)
    rows = SAMPLE_ROWS
    nrow = NSA_KV * NSA_REP * rows
    blk_pad = selm_ref.shape[1]
    tk = npg * PAGE
    tpos = past + _mod(_iota((nrow, 1), 0), rows)
    slope = _stack_groups(lambda g: _slope_col(g, rows))

    @pl.when(j == 0)
    def _():
        q = q_ref[...]
        qm_ref[...] = _stack_groups(lambda g: _group_queries(q, g, rows))
        selrows_ref[...] = _stack_groups(
            lambda g: _tile_rows(selm_ref[g * rows:(g + 1) * rows, :], NSA_REP)).astype(BF16)
        m_ref[...] = jnp.full_like(m_ref, NEG_INF)
        l_ref[...] = jnp.zeros_like(l_ref)
        acc_ref[...] = jnp.zeros_like(acc_ref)
        pad_ref[...] = jnp.zeros_like(pad_ref)

    def attend(score_fn, v_fn, v_transposed, kpos):
        n = kpos.shape[1]
        expand = (_iota((blk_pad, n), 0) == _div(jnp.broadcast_to(kpos, (blk_pad, n)), SEL_BLOCK)).astype(BF16)
        chosen = _dot(selrows_ref[...], expand)
        dist = (tpos - kpos).astype(F32)
        ok = (chosen > 0.5) & (dist >= 0)
        s = jnp.where(ok, score_fn() - slope * dist, -jnp.inf)
        _online_update(s, v_fn(), m_ref, l_ref, acc_ref, v_transposed)

    @pl.when(act_ref[pl.program_id(0), j] > 0)
    def _():
        for k in range(npg):
            kt_ref[:, k * PAGE:(k + 1) * PAGE] = pages[k][0].astype(BF16)
            vt_ref[:, k * PAGE:(k + 1) * PAGE] = pages[k][1].astype(BF16)
        attend(lambda: _dot(qm_ref[...], kt_ref[...]), lambda: vt_ref[...], True, j * tk + _iota((1, tk), 1))

    @pl.when(j == pl.num_programs(1) - 1)
    def _():
        pad_ref[0:rows, :] = selnew_ref[...]
        new = pad_ref[...]
        attend(lambda: _nt(qm_ref[...], new[:, 0:LANES].astype(BF16)),
               lambda: new[:, LANES:2 * LANES].astype(BF16), False, past + _iota((1, PAGE), 1))
        o_sel = acc_ref[...] / jnp.maximum(l_ref[...], 1e-30)

        pad_ref[0:rows, :] = winnew_ref[...]
        wnew = pad_ref[...]
        wb = winbuf_ref.shape[2]
        nw = wb + PAGE
        widx = _iota((1, nw), 1)
        wpos = jnp.where(widx < wb, past - wb + widx, past + widx - wb)
        dist = (tpos - wpos).astype(F32)
        valid = (dist >= 0) & (dist < WINDOW) & (wpos >= 0)
        s = jnp.concatenate([_dot(qm_ref[...], winbuf_ref[0].astype(BF16)),
                             _nt(qm_ref[...], wnew[:, 0:LANES].astype(BF16))], axis=1) - slope * dist
        p = _softmax_rows(s, valid).astype(BF16)
        o_win = _nt(p[:, 0:wb], winbuf_ref[1].astype(BF16)) + _dot(p[:, wb:], wnew[:, LANES:2 * LANES].astype(BF16))

        gates = jax.nn.sigmoid(gate_ref[...])
        o_cmp = ocmp_ref[...]
        for g in range(NSA_KV):
            gc, gs, gw = _gate_cols(gates, g, rows)
            sl = slice(g * NSA_REP * rows, (g + 1) * NSA_REP * rows)
            chunks = _place_group(gc * o_cmp[sl] + gs * o_sel[sl] + gw * o_win[sl], g, rows)
            for kk in range(2):
                o_ref[:, g * 2 * LANES + kk * LANES:g * 2 * LANES + (kk + 1) * LANES] = chunks[kk]


def _nsa_selwin_sample(u_s, pool_sel_t, layer, page_table, selm, o_cmp, win_buf_t, past, npg=16):
    nseq, n_pages = page_table.shape
    npg = min(npg, n_pages)
    blk_pad = selm.shape[2]
    nrow = NSA_KV * NSA_REP * SAMPLE_ROWS
    wb = win_buf_t.shape[4]

    steps = n_pages // npg
    bps = npg * PAGE // SEL_BLOCK
    picked = selm[:, :, :steps * bps].reshape(nseq, selm.shape[1], steps, bps)
    active = jnp.max(picked, axis=(1, 3)) > 0.5
    idx = jnp.arange(steps, dtype=jnp.int32)[None, :]
    src = jnp.maximum(lax.cummax(jnp.where(active, idx, -1), axis=1), 0).astype(jnp.int32)
    act = active.astype(jnp.int32)

    def page_spec(k):
        return pl.BlockSpec((None, None, 2, LANES, PAGE),
                            lambda b, j, pt, src, act: (layer, pt[b, src[b, j] * npg + k], 0, 0, 0))

    grid_spec = pltpu.PrefetchScalarGridSpec(
        num_scalar_prefetch=3, grid=(nseq, steps),
        in_specs=[page_spec(k) for k in range(npg)] + [
            pl.BlockSpec((SAMPLE_ROWS, 512), lambda b, j, *_: (b, C_NQ // 512)),
            pl.BlockSpec((SAMPLE_ROWS, LANES), lambda b, j, *_: (b, C_NG // LANES)),
            pl.BlockSpec((None, NSA_KV * SAMPLE_ROWS, blk_pad), lambda b, j, *_: (b, 0, 0)),
            pl.BlockSpec((None, nrow, LANES), lambda b, j, *_: (b, 0, 0)),
            pl.BlockSpec((None, None, 2, LANES, wb), lambda b, j, *_: (layer, b, 0, 0, 0)),
            pl.BlockSpec((SAMPLE_ROWS, 256), lambda b, j, *_: (b, C_SEL // 256)),
            pl.BlockSpec((SAMPLE_ROWS, 256), lambda b, j, *_: (b, C_WIN // 256))],
        out_specs=pl.BlockSpec((SAMPLE_ROWS, 512), lambda b, j, *_: (b, 0)),
        scratch_shapes=[pltpu.VMEM((nrow, LANES), BF16), pltpu.VMEM((nrow, blk_pad), BF16),
                        pltpu.VMEM((nrow, 1), F32), pltpu.VMEM((nrow, 1), F32),
                        pltpu.VMEM((nrow, LANES), F32), pltpu.VMEM((PAGE, 256), F32),
                        pltpu.VMEM((LANES, npg * PAGE), BF16), pltpu.VMEM((LANES, npg * PAGE), BF16)])
    return pl.pallas_call(
        functools.partial(_nsa_selwin_sample_kernel, npg=npg, past=past),
        out_shape=jax.ShapeDtypeStruct((nseq * SAMPLE_ROWS, 512), F32),
        grid_spec=grid_spec,
        compiler_params=_params(("arbitrary", "arbitrary")),
        name="nsa_selwin_sample",
    )(page_table, src, act, *([pool_sel_t] * npg), u_s, u_s, selm, o_cmp, win_buf_t, u_s, u_s)


def _fox_sample_kernel(pt_ref, *refs, npg):
    pages = refs[:npg]
    (q_ref, cpast_ref, clast_ref, knew_ref, vnew_ref, ff_ref, bf_ref, o_ref, logf_ref,
     qbd_ref, m_ref, l_ref, acc_ref, kpad_ref, vpad_ref, fpad_ref) = refs[npg:]
    j = pl.program_id(1)
    rows = SAMPLE_ROWS
    width = FOX_HEADS * HEAD_DIM
    nrow = FOX_HEADS * rows

    @pl.when(j == 0)
    def _():
        q = q_ref[...] * (HEAD_DIM ** -0.5)
        head = _div(_iota((rows, width), 1), HEAD_DIM)
        qbd_ref[...] = jnp.concatenate(
            [jnp.where(head == h, q, 0.0) for h in range(FOX_HEADS)], axis=0).astype(BF16)
        m_ref[...] = jnp.full_like(m_ref, NEG_INF)
        l_ref[...] = jnp.zeros_like(l_ref)
        acc_ref[...] = jnp.zeros_like(acc_ref)
        kpad_ref[...] = jnp.zeros_like(kpad_ref)
        vpad_ref[...] = jnp.zeros_like(vpad_ref)
        fpad_ref[...] = jnp.zeros_like(fpad_ref)

    def head_rows(x):
        return jnp.concatenate([jnp.broadcast_to(x[h:h + 1, :], (rows, x.shape[1])) for h in range(FOX_HEADS)], axis=0)

    c_last = jnp.max(jnp.where(_iota(clast_ref.shape, 1) == clast_ref.shape[1] - 1, clast_ref[...], -jnp.inf),
                     axis=1, keepdims=True)
    qbd = qbd_ref[...]
    s = jnp.concatenate([_dot(qbd, pages[k][0].astype(BF16)) for k in range(npg)], axis=1)
    s = s + head_rows(c_last - cpast_ref[...])
    m_prev = m_ref[...]
    m_new = jnp.maximum(m_prev, jnp.max(s, axis=1, keepdims=True))
    alpha = jnp.exp(m_prev - m_new)
    p = jnp.exp(s - m_new)
    l_ref[...] = alpha * l_ref[...] + jnp.sum(p, axis=1, keepdims=True)
    pb = p.astype(BF16)
    acc = alpha * acc_ref[...]
    for k in range(npg):
        acc = acc + _nt(pb[:, k * PAGE:(k + 1) * PAGE], pages[k][1].astype(BF16))
    acc_ref[...] = acc
    m_ref[...] = m_new

    @pl.when(j == pl.num_programs(1) - 1)
    def _():
        lf = _log_sigmoid(ff_ref[...] + bf_ref[...])
        logf_ref[...] = lf
        fpad_ref[0:rows, :] = lf
        c_new, _ = _head_major_cumsum(fpad_ref[...], 0.0)
        kpad_ref[0:rows, :] = knew_ref[...]
        vpad_ref[0:rows, :] = vnew_ref[...]
        s = _nt(qbd_ref[...], kpad_ref[...].astype(BF16)) - head_rows(c_new)
        causal = _iota((nrow, PAGE), 1) <= _mod(_iota((nrow, PAGE), 0), rows)
        _online_update(jnp.where(causal, s, -jnp.inf), vpad_ref[...].astype(BF16), m_ref, l_ref, acc_ref)
        o = acc_ref[...] / jnp.maximum(l_ref[...], 1e-30)
        head = _div(_iota((rows, width), 1), HEAD_DIM)
        out = jnp.zeros((rows, width), F32)
        for h in range(FOX_HEADS):
            out = jnp.where(head == h, o[h * rows:(h + 1) * rows], out)
        o_ref[...] = out


def _fox_sample(u_s, pool_kv_t, layer, page_table, c_past, bf_pad, npg=16):
    nseq, n_pages = page_table.shape
    npg = min(npg, n_pages)
    width = FOX_HEADS * HEAD_DIM
    nrow = FOX_HEADS * SAMPLE_ROWS
    past = n_pages * PAGE
    tk = npg * PAGE

    def page_spec(k):
        return pl.BlockSpec((None, None, 2, width, PAGE), lambda b, j, pt: (layer, pt[b, j * npg + k], 0, 0, 0))

    grid_spec = pltpu.PrefetchScalarGridSpec(
        num_scalar_prefetch=1, grid=(nseq, n_pages // npg),
        in_specs=[page_spec(k) for k in range(npg)] + [
            pl.BlockSpec((SAMPLE_ROWS, width), lambda b, j, pt: (b, C_FQ // width)),
            pl.BlockSpec((None, FOX_HEADS, tk), lambda b, j, pt: (b, 0, j)),
            pl.BlockSpec((None, FOX_HEADS, LANES), lambda b, j, pt: (b, 0, past // LANES - 1)),
            pl.BlockSpec((SAMPLE_ROWS, width), lambda b, j, pt: (b, C_FKV // width)),
            pl.BlockSpec((SAMPLE_ROWS, width), lambda b, j, pt: (b, C_FKV // width + 1)),
            pl.BlockSpec((SAMPLE_ROWS, LANES), lambda b, j, pt: (b, C_FF // LANES)),
            pl.BlockSpec((1, LANES), lambda b, j, pt: (0, 0))],
        out_specs=(pl.BlockSpec((SAMPLE_ROWS, width), lambda b, j, pt: (b, 0)),
                   pl.BlockSpec((SAMPLE_ROWS, LANES), lambda b, j, pt: (b, 0))),
        scratch_shapes=[pltpu.VMEM((nrow, width), BF16), pltpu.VMEM((nrow, 1), F32), pltpu.VMEM((nrow, 1), F32),
                        pltpu.VMEM((nrow, width), F32), pltpu.VMEM((PAGE, width), F32),
                        pltpu.VMEM((PAGE, width), F32), pltpu.VMEM((PAGE, LANES), F32)])
    return pl.pallas_call(
        functools.partial(_fox_sample_kernel, npg=npg),
        out_shape=(jax.ShapeDtypeStruct((nseq * SAMPLE_ROWS, width), F32),
                   jax.ShapeDtypeStruct((nseq * SAMPLE_ROWS, LANES), F32)),
        grid_spec=grid_spec,
        compiler_params=_params(("arbitrary", "arbitrary")),
        name="fox_sample",
    )(page_table, *([pool_kv_t] * npg), u_s, c_past, c_past, u_s, u_s, u_s, bf_pad)


def _outproj_kernel(on_ref, of_ref, or_ref, nz_ref, fz_ref, rz_ref, g0_ref, g1_ref, g2_ref, x_ref, gate_ref,
                    wb_ref, wo_ref, fg_ref, *out_refs, final):
    merged = None
    for n, (o_ref, z_ref, g_ref) in enumerate(((on_ref, nz_ref, g0_ref), (of_ref, fz_ref, g1_ref),
                                               (or_ref, rz_ref, g2_ref))):
        br = (o_ref[...] * _silu(z_ref[...])).astype(BF16)
        term = jax.nn.sigmoid(g_ref[...]) * _dot(br, wb_ref[n])
        merged = term if merged is None else merged + term
    y = _dot(merged.astype(BF16), wo_ref[...])
    x = x_ref[...] + gate_ref[...] * y
    out_refs[0][...] = x
    if final:
        ms = jnp.mean(x * x, axis=-1, keepdims=True)
        out_refs[1][...] = x * lax.rsqrt(ms + EPS) * fg_ref[...]


def _outproj(u, o_nsa, o_fox, o_ret, x2d, gate, wb, wo, final_g, rows_per_mod, final):
    n, d = x2d.shape
    tm = min(256, rows_per_mod, n)
    bw = 512

    def row(i):
        return (i, 0)

    out_shape = [jax.ShapeDtypeStruct((n, d), F32)] * (2 if final else 1)
    out_specs = [pl.BlockSpec((tm, d), row)] * (2 if final else 1)
    return pl.pallas_call(
        functools.partial(_outproj_kernel, final=final),
        out_shape=tuple(out_shape),
        grid=(n // tm,),
        in_specs=[pl.BlockSpec((tm, bw), row)] * 3
                 + [pl.BlockSpec((tm, bw), lambda i, c=c: (i, c // bw)) for c in (C_NZ, C_FZ, C_RZ)]
                 + [pl.BlockSpec((tm, d), lambda i, c=c: (i, C_MG // d + c)) for c in range(3)]
                 + [pl.BlockSpec((tm, d), row),
                    _mod_spec(gate, tm, rows_per_mod, d, 1),
                    pl.BlockSpec(wb.shape, lambda i: (0, 0, 0)),
                    pl.BlockSpec(wo.shape, lambda i: (0, 0)),
                    pl.BlockSpec((1, d), lambda i: (0, 0))],
        out_specs=tuple(out_specs),
        compiler_params=_params(("arbitrary",)),
        name="outproj_final" if final else "outproj",
    )(o_nsa, o_fox, o_ret, u, u, u, u, u, u, x2d, gate, wb, wo, final_g.reshape(1, d))


def _pad_cols(w, width):
    return jnp.pad(w, ((0, 0), (0, width - w.shape[1])))


def _pack_w_in(w):
    o = 0
    seg = {}
    for name, width in (("nq", 512), ("nkv", 768), ("ng", 24), ("nz", 512), ("fq", 512), ("fk", 512),
                        ("fv", 512), ("ff", 8), ("fz", 512), ("rq", 512), ("rk", 512), ("rv", 512),
                        ("rz", 512), ("mg", 3 * D_MODEL)):
        seg[name] = w[:, o:o + width]
        o += width
    order = [seg["nq"], seg["fq"], seg["rq"], seg["rk"], seg["rv"], seg["nz"], seg["fz"], seg["rz"],
             seg["fk"], seg["fv"], seg["mg"], seg["nkv"], _pad_cols(seg["ng"], LANES), _pad_cols(seg["ff"], LANES)]
    w_pad = jnp.concatenate(order, axis=1).astype(BF16)
    nkv = seg["nkv"]
    t_order = [seg["nq"], seg["fq"], seg["fv"], nkv[:, 256 + LANES:512], nkv[:, 512 + LANES:768],
               _pad_cols(seg["ng"], LANES)]
    wt_sel = jnp.concatenate(t_order, axis=1).T.astype(BF16)
    wt_rows = jnp.concatenate([nkv, seg["fk"], seg["fv"]], axis=1).T.astype(BF16)
    return w_pad, wt_sel, wt_rows


def _pack_cmp(w1k, w1v, posk, posv):
    def parts(w):
        return jnp.stack([w[:CMP_STRIDE], w[CMP_STRIDE:]], axis=2)

    w4 = jnp.stack([parts(w1k), parts(w1k), parts(w1v), parts(w1v)])
    big = jnp.einsum("cC,csdph->scdCph", jnp.eye(4, dtype=w4.dtype), w4)
    wt = big.reshape(CMP_FEAT, 4 * 2 * CMP_HIDDEN).T.astype(BF16)

    def pos_row(part):
        sl = slice(part * CMP_STRIDE, (part + 1) * CMP_STRIDE)
        return jnp.stack([posk[sl], posk[sl], posv[sl], posv[sl]], axis=1).reshape(CMP_FEAT)

    p_pad = jnp.zeros((LANES, CMP_FEAT), F32).at[0].set(pos_row(0)).at[1].set(pos_row(1))
    return wt, p_pad


def kernel(x_prompt, x_sample, cache_nsa_cmp_kv, cache_nsa_sel_kv, cache_nsa_win_kv, cache_fox_kv, cache_fox_logf, state_ret, page_table, c_prompt, c_sample, norm_g, w_ada, b_ada, w_in, b_forget, w_cmp_k1, w_cmp_k2, pos_cmp_k, w_cmp_v1, w_cmp_v2, pos_cmp_v, w_branch, w_out, final_g):
    batch, seq, d = x_prompt.shape
    nseq, dec = x_sample.shape[:2]
    depth = norm_g.shape[0]
    n_pool = cache_nsa_cmp_kv.shape[1]
    n_pages = page_table.shape[1]
    past = n_pages * PAGE
    wb_len = cache_nsa_win_kv.shape[2]
    assert d == D_MODEL and dec <= SAMPLE_ROWS and past >= wb_len and seq % 512 == 0
    n_sel = -(-(past + dec) // SEL_BLOCK)
    blk_pad = -(-n_sel // LANES) * LANES
    srows = nseq * SAMPLE_ROWS

    hp = x_prompt.reshape(batch * seq, d)
    hs = jnp.pad(x_sample, ((0, 0), (0, SAMPLE_ROWS - dec), (0, 0))).reshape(srows, d)
    c_rows = 8
    c_all = jnp.concatenate([jnp.pad(c_prompt, ((0, c_rows - batch), (0, 0))), c_sample], axis=0)
    lg = jnp.log1p(-jnp.exp2(-5.0 - jnp.arange(RET_HEADS, dtype=F32)))
    lg_lanes = jnp.repeat(lg, HEAD_DIM).reshape(RET_HEADS // 2, 1, LANES)
    zero_state = jnp.zeros((batch, RET_HEADS, HEAD_DIM, HEAD_DIM), F32)
    pool_cmp = jnp.transpose(cache_nsa_cmp_kv, (0, 1, 3, 4, 5, 2)).reshape(depth, n_pool, 256, PAGE)
    pool_sel = jnp.transpose(cache_nsa_sel_kv, (0, 1, 3, 4, 5, 2)).reshape(depth, n_pool, 2, LANES, PAGE)
    pool_fox = jnp.transpose(cache_fox_kv, (0, 1, 3, 4, 5, 2)).reshape(depth, n_pool, 2, FOX_HEADS * HEAD_DIM, PAGE)
    pool_logf = jnp.transpose(cache_fox_logf, (0, 1, 3, 2))
    win_buf = jnp.transpose(cache_nsa_win_kv, (0, 1, 3, 4, 5, 2)).reshape(depth, nseq, 2, LANES, wb_len)

    outs_p, outs_s = [], []
    for l in range(depth):
        w_pad, wt_sel, wt_rows = _pack_w_in(w_in[l])
        wt, p_pad = _pack_cmp(w_cmp_k1[l], w_cmp_v1[l], pos_cmp_k[l], pos_cmp_v[l])
        w2t = jnp.stack([w_cmp_k2[l].T, w_cmp_k2[l].T, w_cmp_v2[l].T, w_cmp_v2[l].T]).astype(BF16)
        wb_bf = w_branch[l].astype(BF16)
        wo_bf = w_out[l].astype(BF16)
        bf_pad = _pad_cols(b_forget[l].reshape(1, FOX_HEADS), LANES)
        final = l == depth - 1

        mod = _adaln(c_all, w_ada[l].astype(BF16), b_ada[l])
        shift_p, scale_p, gate_p = [m.reshape(batch, 1, d) for m in jnp.split(mod[:batch], 3, axis=1)]
        shift_s, scale_s, gate_s = [jnp.repeat(m, SAMPLE_ROWS, axis=0) for m in jnp.split(mod[c_rows:], 3, axis=1)]
        cmp_bias = _cmp_bias(wt, p_pad)

        u = _inproj(hp, norm_g[l], scale_p, shift_p, w_pad, seq)
        ut = _inproj_t(hp, norm_g[l], scale_p, shift_p, wt_sel, seq)
        nsa_rows_t = _inproj_t(hp, norm_g[l], scale_p, shift_p, wt_rows[:768], seq, tn=256, per_batch=True)
        fox_rows_t = _inproj_t(hp, norm_g[l], scale_p, shift_p, wt_rows[768:], seq, tn=256, per_batch=True)
        cmp_rows = u[:, C_CMP:C_CMP + 256]
        logf, c3 = _logf_prompt(u, bf_pad, batch, seq)
        o_fox = _fox_prompt(u, ut, c3, batch, seq)
        chunk = min(2 * RET_CHUNK, seq)
        o_ret, ret_state = _retention(u, zero_state, lg_lanes, batch, seq, chunk, chunk)
        pre_t = _cmp_pre_prompt(cmp_rows.reshape(batch, seq // CMP_STRIDE, CMP_FEAT), wt)
        kvt, kc = _cmp_post(pre_t, cmp_bias, w2t)
        o_nsa = _nsa_prompt(u, ut, kc, kvt, batch, seq)
        res = _outproj(u, o_nsa, o_fox, o_ret, hp, gate_p, wb_bf, wo_bf, final_g, seq, final)
        hp = res[0]
        y_prompt = res[-1]

        def cache_rows(x, r0, heads):
            x = x[:, r0:r0 + 2 * heads * HEAD_DIM].reshape(batch, 2, heads, HEAD_DIM, seq)
            return jnp.transpose(x, (0, 4, 1, 2, 3))

        win_rows = cache_rows(nsa_rows_t, 512, NSA_KV)
        win_state = jnp.concatenate(
            [jnp.zeros((batch, wb_len) + win_rows.shape[2:], F32), win_rows], axis=1)[:, -wb_len:]
        outs_p.append((cache_rows(nsa_rows_t, 0, NSA_KV), cache_rows(nsa_rows_t, 256, NSA_KV), win_state,
                       cache_rows(fox_rows_t, 0, FOX_HEADS), logf, ret_state))

        us = _inproj(hs, norm_g[l], scale_s, shift_s, w_pad, srows)
        kvt_s, _ = _cmp_post(_cmp_pre_paged(pool_cmp, l, page_table, wt), cmp_bias, w2t)
        o_cmp_s, selm = _nsa_cmp_sample(us, kvt_s, nseq, past, n_sel, blk_pad)
        o_nsa_s = _nsa_selwin_sample(us, pool_sel, l, page_table, selm, o_cmp_s, win_buf, past)
        c_past = _logf_past(pool_logf, l, page_table)
        o_fox_s, logf_s = _fox_sample(us, pool_fox, l, page_table, c_past, bf_pad)
        o_ret_s, ret_state_s = _retention(us, state_ret[l], lg_lanes, nseq, SAMPLE_ROWS, SAMPLE_ROWS, dec)
        res = _outproj(us, o_nsa_s, o_fox_s, o_ret_s, hs, gate_s, wb_bf, wo_bf, final_g, srows, final)
        hs = res[0]
        y_sample = res[-1]

        def new_rows(c0, width, shape):
            return us[:, c0:c0 + width].reshape(nseq, SAMPLE_ROWS, width)[:, :dec].reshape((nseq, dec) + shape)

        win_new = new_rows(C_WIN, 256, (2, NSA_KV, HEAD_DIM))
        win_all = jnp.concatenate([cache_nsa_win_kv[l], win_new], axis=1)
        outs_s.append((new_rows(C_CMP, 256, (2, NSA_KV, HEAD_DIM)),
                       new_rows(C_SEL, 256, (2, NSA_KV, HEAD_DIM)),
                       win_all[:, dec:],
                       new_rows(C_FKV, 1024, (2, FOX_HEADS, HEAD_DIM)),
                       logf_s.reshape(nseq, SAMPLE_ROWS, LANES)[:, :dec, :FOX_HEADS],
                       ret_state_s))

    def stacked(rows, i):
        return jnp.stack([r[i] for r in rows], axis=0)

    y_prompt = y_prompt.reshape(batch, seq, d)
    y_sample = y_sample.reshape(nseq, SAMPLE_ROWS, d)[:, :dec]
    return (y_prompt, y_sample,
            stacked(outs_p, 0), stacked(outs_s, 0), stacked(outs_p, 1), stacked(outs_s, 1),
            stacked(outs_p, 2), stacked(outs_s, 2), stacked(outs_p, 3), stacked(outs_s, 3),
            stacked(outs_p, 4), stacked(outs_s, 4), stacked(outs_p, 5), stacked(outs_s, 5))
```

```python
import functools

import jax
import jax.numpy as jnp
from jax import lax
from jax.experimental import pallas as pl
from jax.experimental.pallas import tpu as pltpu

F32 = jnp.float32
BF16 = jnp.bfloat16

D_MODEL = 1024
HEAD_DIM = 64
NSA_KV = 2
NSA_REP = 4
CMP_LEN = 32
CMP_STRIDE = 16
CMP_HIDDEN = 128
SEL_BLOCK = 64
SEL_TOPK = 16
WINDOW = 512
FOX_HEADS = 8
RET_HEADS = 8
RET_CHUNK = 128
EPS = 1e-6
NEG_INF = -1e30
FORCE = 1e9
LANES = 128
PAGE = 128
SAMPLE_ROWS = 8
VMEM_LIMIT = 48 * 1024 * 1024

C_NQ, C_FQ, C_RQ, C_RK, C_RV = 0, 512, 1024, 1536, 2048
C_NZ, C_FZ, C_RZ = 2560, 3072, 3584
C_FKV = 4096
C_MG = 5120
C_CMP, C_SEL, C_WIN = 8192, 8448, 8704
C_NG, C_FF = 8960, 9088
D_PAD = 9216
T_NQ, T_FQ, T_FV, T_SELV, T_WINV, T_NG = 0, 512, 1024, 1536, 1664, 1792
T_ROWS = 1920
T_TILE = 384
CMP_FEAT = CMP_STRIDE * 4 * HEAD_DIM


def _params(sem, vmem=VMEM_LIMIT):
    return pltpu.CompilerParams(dimension_semantics=sem, vmem_limit_bytes=vmem)


def _nt(a, b):
    return lax.dot_general(a, b, (((1,), (1,)), ((), ())), preferred_element_type=F32)


def _tn(a, b):
    return lax.dot_general(a, b, (((0,), (0,)), ((), ())), preferred_element_type=F32)


def _dot(a, b):
    return jnp.dot(a, b, preferred_element_type=F32)


def _split3(x):
    hi = x.astype(BF16)
    r1 = x - hi.astype(F32)
    mid = r1.astype(BF16)
    lo = (r1 - mid.astype(F32)).astype(BF16)
    return hi, mid, lo


def _silu(x):
    return x * jax.nn.sigmoid(x)


def _log_sigmoid(x):
    return jnp.minimum(x, 0.0) - jnp.log(1.0 + jnp.exp(-jnp.abs(x)))


def _iota(shape, dim):
    return lax.broadcasted_iota(jnp.int32, shape, dim)


def _div(x, n):
    return lax.shift_right_arithmetic(x, jnp.int32(n.bit_length() - 1))


def _mod(x, n):
    return x & (n - 1)


def _lane_tile(x, n):
    return x if n == 1 else jnp.concatenate([x] * n, axis=1)


def _adaln_kernel(c_ref, w_ref, b_ref, o_ref):
    c = c_ref[...]
    o_ref[...] = _dot(_silu(c).astype(BF16), w_ref[...]) + b_ref[...]


def _adaln(c_all, w_bf, b):
    rows, d = c_all.shape
    n = w_bf.shape[1]
    tn = 1024
    return pl.pallas_call(
        _adaln_kernel,
        out_shape=jax.ShapeDtypeStruct((rows, n), F32),
        grid=(n // tn,),
        in_specs=[pl.BlockSpec((rows, d), lambda j: (0, 0)),
                  pl.BlockSpec((d, tn), lambda j: (0, j)),
                  pl.BlockSpec((1, tn), lambda j: (0, j))],
        out_specs=pl.BlockSpec((rows, tn), lambda j: (0, j)),
        compiler_params=_params(("arbitrary",)),
        name="adaln",
    )(c_all, w_bf, b.reshape(1, n))


def _inproj_kernel(x_ref, g_ref, sc_ref, sh_ref, w_ref, o_ref, h_ref, *, transposed):
    @pl.when(pl.program_id(1) == 0)
    def _():
        x = x_ref[...]
        ms = jnp.mean(x * x, axis=-1, keepdims=True)
        y = x * lax.rsqrt(ms + EPS) * g_ref[...]
        h_ref[...] = (y * (1.0 + sc_ref[...]) + sh_ref[...]).astype(BF16)

    o_ref[...] = _nt(w_ref[...], h_ref[...]) if transposed else _dot(h_ref[...], w_ref[...])


def _mod_spec(mod, tm, rows_per_mod, d, nargs):
    if mod.ndim == 2:
        return pl.BlockSpec((tm, d), (lambda i, j: (i, 0)) if nargs == 2 else (lambda i: (i, 0)))
    per = rows_per_mod // tm
    return pl.BlockSpec((None, 1, d), (lambda i, j: (i // per, 0, 0)) if nargs == 2 else (lambda i: (i // per, 0, 0)))


def _inproj(x2d, g, scale, shift, w_pad, rows_per_mod):
    n, d = x2d.shape
    tm = min(1024, rows_per_mod, n)
    tn = 512
    return pl.pallas_call(
        functools.partial(_inproj_kernel, transposed=False),
        out_shape=jax.ShapeDtypeStruct((n, D_PAD), F32),
        grid=(n // tm, D_PAD // tn),
        in_specs=[pl.BlockSpec((tm, d), lambda i, j: (i, 0)),
                  pl.BlockSpec((1, d), lambda i, j: (0, 0)),
                  _mod_spec(scale, tm, rows_per_mod, d, 2),
                  _mod_spec(shift, tm, rows_per_mod, d, 2),
                  pl.BlockSpec((d, tn), lambda i, j: (0, j))],
        out_specs=pl.BlockSpec((tm, tn), lambda i, j: (i, j)),
        scratch_shapes=[pltpu.VMEM((tm, d), BF16)],
        compiler_params=_params(("arbitrary", "arbitrary")),
        name="inproj",
    )(x2d, g.reshape(1, d), scale, shift, w_pad)


def _inproj_t(x2d, g, scale, shift, wt_sel, rows_per_mod, tn=T_TILE, per_batch=False):
    n, d = x2d.shape
    rows = wt_sel.shape[0]
    tm = min(1024, rows_per_mod, n)
    per = rows_per_mod // tm
    if per_batch:
        out_shape = jax.ShapeDtypeStruct((n // rows_per_mod, rows, rows_per_mod), F32)
        out_spec = pl.BlockSpec((None, tn, tm), lambda i, j: (i // per, j, i % per))
    else:
        out_shape = jax.ShapeDtypeStruct((rows, n), F32)
        out_spec = pl.BlockSpec((tn, tm), lambda i, j: (j, i))
    return pl.pallas_call(
        functools.partial(_inproj_kernel, transposed=True),
        out_shape=out_shape,
        grid=(n // tm, rows // tn),
        in_specs=[pl.BlockSpec((tm, d), lambda i, j: (i, 0)),
                  pl.BlockSpec((1, d), lambda i, j: (0, 0)),
                  _mod_spec(scale, tm, rows_per_mod, d, 2),
                  _mod_spec(shift, tm, rows_per_mod, d, 2),
                  pl.BlockSpec((tn, d), lambda i, j: (j, 0))],
        out_specs=out_spec,
        scratch_shapes=[pltpu.VMEM((tm, d), BF16)],
        compiler_params=_params(("arbitrary", "arbitrary")),
        name="inproj_rows" if per_batch else "inproj_t",
    )(x2d, g.reshape(1, d), scale, shift, wt_sel)


def _lane_cumsum(lft, carry):
    t = lft.shape[1]
    upper = (_iota((t, t), 0) <= _iota((t, t), 1)).astype(BF16)
    c = sum(_dot(p, upper) for p in _split3(lft)) + carry
    return c, carry + jnp.sum(lft, axis=1, keepdims=True)


def _head_major_cumsum(lf, carry):
    eye = (_iota((FOX_HEADS, LANES), 0) == _iota((FOX_HEADS, LANES), 1)).astype(BF16)
    lft = sum(_nt(eye, p) for p in _split3(lf))
    return _lane_cumsum(lft, carry)


def _logf_prompt_kernel(ff_ref, bf_ref, logf_ref, c_ref, carry_ref):
    @pl.when(pl.program_id(1) == 0)
    def _():
        carry_ref[...] = jnp.zeros_like(carry_ref)

    lf = _log_sigmoid(ff_ref[...] + bf_ref[...])
    logf_ref[...] = lf[:, :FOX_HEADS]
    c, carry = _head_major_cumsum(lf, carry_ref[...])
    carry_ref[...] = carry
    pieces = [p.astype(F32) for p in _split3(c)]
    tb = c.shape[1]
    c3t = jnp.concatenate(pieces + [jnp.zeros((LANES - 3 * FOX_HEADS, tb), F32)], axis=0)
    c_ref[...] = c3t.T.astype(BF16)


def _logf_prompt(u, bf_pad, batch, seq):
    tb = min(512, seq)
    nt = seq // tb
    return pl.pallas_call(
        _logf_prompt_kernel,
        out_shape=(jax.ShapeDtypeStruct((batch, seq, FOX_HEADS), F32),
                   jax.ShapeDtypeStruct((batch * seq, LANES), BF16)),
        grid=(batch, nt),
        in_specs=[pl.BlockSpec((tb, LANES), lambda b, i: (b * nt + i, C_FF // LANES)),
                  pl.BlockSpec((1, LANES), lambda b, i: (0, 0))],
        out_specs=(pl.BlockSpec((None, tb, FOX_HEADS), lambda b, i: (b, i, 0)),
                   pl.BlockSpec((tb, LANES), lambda b, i: (b * nt + i, 0))),
        scratch_shapes=[pltpu.VMEM((FOX_HEADS, 1), F32)],
        compiler_params=_params(("arbitrary", "arbitrary")),
        name="logf_prompt",
    )(u, bf_pad)


def _logf_past_kernel(pt_ref, *refs, npg):
    pages, (c_ref, carry_ref) = refs[:npg], refs[npg:]

    @pl.when(pl.program_id(1) == 0)
    def _():
        carry_ref[...] = jnp.zeros_like(carry_ref)

    x = jnp.concatenate([pages[k][...] for k in range(npg)], axis=0)
    rows = npg * FOX_HEADS
    local, _ = _lane_cumsum(x, 0.0)
    tot = jnp.sum(x, axis=1, keepdims=True)
    r, c = _iota((rows, rows), 0), _iota((rows, rows), 1)
    earlier = ((_mod(c, FOX_HEADS) == _mod(r, FOX_HEADS)) & (c < r)).astype(BF16)
    tot_l = jnp.broadcast_to(tot, (rows, PAGE))
    offs = sum(_dot(earlier, p) for p in _split3(tot_l))
    carry = carry_ref[...]
    cum = local + offs + jnp.concatenate([carry] * npg, axis=0)
    for k in range(npg):
        c_ref[:, k * PAGE:(k + 1) * PAGE] = cum[k * FOX_HEADS:(k + 1) * FOX_HEADS, :]
    carry_ref[...] = carry + sum(tot[k * FOX_HEADS:(k + 1) * FOX_HEADS, :] for k in range(npg))


def _logf_past(pool_logf_t, layer, page_table, npg=32):
    nseq, n_pages = page_table.shape
    npg = min(npg, n_pages)
    steps = n_pages // npg

    def page_spec(k):
        return pl.BlockSpec((None, None, FOX_HEADS, PAGE), lambda b, j, pt: (layer, pt[b, j * npg + k], 0, 0))

    grid_spec = pltpu.PrefetchScalarGridSpec(
        num_scalar_prefetch=1, grid=(nseq, steps),
        in_specs=[page_spec(k) for k in range(npg)],
        out_specs=pl.BlockSpec((None, FOX_HEADS, npg * PAGE), lambda b, j, pt: (b, 0, j)),
        scratch_shapes=[pltpu.VMEM((FOX_HEADS, 1), F32)])
    return pl.pallas_call(
        functools.partial(_logf_past_kernel, npg=npg),
        out_shape=jax.ShapeDtypeStruct((nseq, FOX_HEADS, n_pages * PAGE), F32),
        grid_spec=grid_spec,
        compiler_params=_params(("arbitrary", "arbitrary")),
        name="logf_past",
    )(page_table, *([pool_logf_t] * npg))


def _fox_prompt_kernel(ii_ref, jj_ref, qt_ref, k_ref, c3_ref, vt_ref, o_ref, qaug_ref, m_ref, l_ref, acc_ref):
    p = pl.program_id(2)
    i, j = ii_ref[p], jj_ref[p]
    tq, tk = qt_ref.shape[1], k_ref.shape[0]
    hp = pl.program_id(1)

    @pl.when(j == 0)
    def _():
        qt = qt_ref[...] * (HEAD_DIM ** -0.5)
        row = _iota(qt.shape, 0)
        for h in range(2):
            head = 2 * hp + h
            piece_row = (row == head) | (row == head + FOX_HEADS) | (row == head + 2 * FOX_HEADS)
            qaug_ref[h] = jnp.concatenate(
                [jnp.where((row < HEAD_DIM) == (h == 0), qt, 0.0),
                 jnp.where(piece_row, -1.0, 0.0)], axis=0).astype(BF16)
        m_ref[...] = jnp.full_like(m_ref, NEG_INF)
        l_ref[...] = jnp.zeros_like(l_ref)
        acc_ref[...] = jnp.zeros_like(acc_ref)

    def step(masked):
        kaug = jnp.concatenate([k_ref[...].astype(BF16), c3_ref[...]], axis=1)
        vt = vt_ref[...].astype(BF16)
        if masked:
            keep = _iota((tk, tq), 0) <= _iota((tk, tq), 1)
        for h in range(2):
            st = _dot(kaug, qaug_ref[h])
            if masked:
                st = jnp.where(keep, st, -jnp.inf)
            m_prev = m_ref[h]
            m_new = jnp.maximum(m_prev, jnp.max(st, axis=0, keepdims=True))
            alpha = jnp.exp(m_prev - m_new)
            pt = jnp.exp(st - m_new)
            l_ref[h] = alpha * l_ref[h] + jnp.sum(pt, axis=0, keepdims=True)
            acc_ref[h] = alpha * acc_ref[h] + _dot(vt[h * HEAD_DIM:(h + 1) * HEAD_DIM, :], pt.astype(BF16))
            m_ref[h] = m_new

    pl.when(j < i)(lambda: step(False))

    @pl.when(j == i)
    def _():
        step(True)
        ot = jnp.concatenate([acc_ref[h] / jnp.maximum(l_ref[h], 1e-30) for h in range(2)], axis=0)
        o_ref[...] = ot.T


def _fox_prompt(u, ut, c3, batch, seq):
    tq = min(512, seq)
    nq = seq // tq
    pairs = FOX_HEADS // 2
    ii = jnp.asarray([i for i in range(nq) for _ in range(i + 1)], jnp.int32)
    jj = jnp.asarray([j for i in range(nq) for j in range(i + 1)], jnp.int32)
    grid_spec = pltpu.PrefetchScalarGridSpec(
        num_scalar_prefetch=2, grid=(batch, pairs, ii.shape[0]),
        in_specs=[pl.BlockSpec((LANES, tq), lambda b, hp, p, ii, jj: (T_FQ // LANES + hp, b * nq + ii[p])),
                  pl.BlockSpec((tq, LANES), lambda b, hp, p, ii, jj: (b * nq + jj[p], C_FKV // LANES + hp)),
                  pl.BlockSpec((tq, LANES), lambda b, hp, p, ii, jj: (b * nq + jj[p], 0)),
                  pl.BlockSpec((LANES, tq), lambda b, hp, p, ii, jj: (T_FV // LANES + hp, b * nq + jj[p]))],
        out_specs=pl.BlockSpec((tq, LANES), lambda b, hp, p, ii, jj: (b * nq + ii[p], hp)),
        scratch_shapes=[pltpu.VMEM((2, 2 * LANES, tq), BF16), pltpu.VMEM((2, 1, tq), F32),
                        pltpu.VMEM((2, 1, tq), F32), pltpu.VMEM((2, HEAD_DIM, tq), F32)])
    return pl.pallas_call(
        _fox_prompt_kernel,
        out_shape=jax.ShapeDtypeStruct((batch * seq, FOX_HEADS * HEAD_DIM), F32),
        grid_spec=grid_spec,
        compiler_params=_params(("arbitrary",) * 3),
        name="fox_prompt",
    )(ii, jj, ut, u, c3, ut)


def _retention_kernel(q_ref, k_ref, v_ref, lg_ref, s0_ref, o_ref, s_ref, sbd_ref, *, c_true):
    ci = pl.program_id(2)
    c = q_ref.shape[0]
    low_row = _iota((LANES, LANES), 0) < HEAD_DIM
    low_col = _iota((LANES, LANES), 1) < HEAD_DIM

    @pl.when(ci == 0)
    def _():
        sbd_ref[...] = s0_ref[...]

    lg = lg_ref[...]
    low = _iota((c, LANES), 1) < HEAD_DIM
    pos = _iota((c, LANES), 0).astype(F32)
    q = q_ref[...]
    k = k_ref[...] * (HEAD_DIM ** -0.5)
    v = v_ref[...].astype(BF16)
    kb = k.astype(BF16)
    diff = (_iota((c, c), 0) - _iota((c, c), 1)).astype(F32)
    inner = []
    for h in range(2):
        lgh = jnp.max(jnp.where(low[:1] == (h == 0), lg, -jnp.inf), axis=1, keepdims=True)
        decay = jnp.where(diff >= 0, jnp.exp(jnp.maximum(diff, 0.0) * lgh), 0.0)
        qm = jnp.where(low == (h == 0), q, 0.0).astype(BF16)
        scores = _nt(qm, kb) * decay
        inner.append(_dot(scores.astype(BF16), v))
    sbd = sbd_ref[...]
    cross = _dot((q * jnp.exp((pos + 1.0) * lg)).astype(BF16), sbd.astype(BF16))
    o = jnp.where(low, inner[0], inner[1]) + cross

    kd = jnp.where(pos < c_true, k * jnp.exp((c_true - 1.0 - pos) * lg), 0.0).astype(BF16)
    upd = jnp.where(low_row == low_col, _tn(kd, v), 0.0)
    sbd_new = jnp.exp(c_true * lg) * sbd + upd
    sbd_ref[...] = sbd_new

    inv = 1.0 / HEAD_DIM
    s_lo = jnp.sum(jnp.where(low, o, 0.0), axis=1, keepdims=True)
    s_hi = jnp.sum(jnp.where(low, 0.0, o), axis=1, keepdims=True)
    d = o - jnp.where(low, s_lo, s_hi) * inv
    d2 = d * d
    v_lo = jnp.sum(jnp.where(low, d2, 0.0), axis=1, keepdims=True)
    v_hi = jnp.sum(jnp.where(low, 0.0, d2), axis=1, keepdims=True)
    o_ref[...] = d * lax.rsqrt(jnp.where(low, v_lo, v_hi) * inv + EPS)

    @pl.when(ci == pl.num_programs(2) - 1)
    def _():
        s_ref[...] = sbd_new


def _to_block_diag(s):
    b = s.shape[0]
    s = s.reshape(b, RET_HEADS // 2, 2, HEAD_DIM, HEAD_DIM)
    z = jnp.zeros_like(s[:, :, 0])
    return jnp.concatenate([jnp.concatenate([s[:, :, 0], z], axis=-1),
                            jnp.concatenate([z, s[:, :, 1]], axis=-1)], axis=-2)


def _from_block_diag(sbd):
    b = sbd.shape[0]
    return jnp.stack([sbd[:, :, :HEAD_DIM, :HEAD_DIM], sbd[:, :, HEAD_DIM:, HEAD_DIM:]],
                     axis=2).reshape(b, RET_HEADS, HEAD_DIM, HEAD_DIM)


def _retention(u, state0, lg_lanes, batch, rows_per_seq, chunk, c_true):
    nc = rows_per_seq // chunk
    pairs = RET_HEADS // 2
    qcol, kcol, vcol = C_RQ // LANES, C_RK // LANES, C_RV // LANES
    o, sbd = pl.pallas_call(
        functools.partial(_retention_kernel, c_true=c_true),
        out_shape=(jax.ShapeDtypeStruct((batch * rows_per_seq, RET_HEADS * HEAD_DIM), F32),
                   jax.ShapeDtypeStruct((batch, pairs, LANES, LANES), F32)),
        grid=(batch, pairs, nc),
        in_specs=[pl.BlockSpec((chunk, LANES), lambda b, hp, ci: (b * nc + ci, qcol + hp)),
                  pl.BlockSpec((chunk, LANES), lambda b, hp, ci: (b * nc + ci, kcol + hp)),
                  pl.BlockSpec((chunk, LANES), lambda b, hp, ci: (b * nc + ci, vcol + hp)),
                  pl.BlockSpec((None, 1, LANES), lambda b, hp, ci: (hp, 0, 0)),
                  pl.BlockSpec((None, None, LANES, LANES), lambda b, hp, ci: (b, hp, 0, 0))],
        out_specs=(pl.BlockSpec((chunk, LANES), lambda b, hp, ci: (b * nc + ci, hp)),
                   pl.BlockSpec((None, None, LANES, LANES), lambda b, hp, ci: (b, hp, 0, 0))),
        scratch_shapes=[pltpu.VMEM((LANES, LANES), F32)],
        compiler_params=_params(("arbitrary",) * 3),
        name="retention",
    )(u, u, u, lg_lanes, _to_block_diag(state0))
    return o, _from_block_diag(sbd)


def _cmp_bias_kernel(wt_ref, p_ref, o_ref):
    r = _nt(wt_ref[...], p_ref[...].astype(BF16))
    part = _mod(_div(_iota(r.shape, 0), CMP_HIDDEN), 2)
    col = jnp.sum(jnp.where(_iota(r.shape, 1) == part, r, 0.0), axis=1, keepdims=True)
    o_ref[...] = jnp.broadcast_to(col, o_ref.shape)


def _cmp_bias(wt, p_pad):
    return pl.pallas_call(
        _cmp_bias_kernel,
        out_shape=jax.ShapeDtypeStruct((wt.shape[0], LANES), F32),
        compiler_params=_params(None),
        name="cmp_bias",
    )(wt, p_pad)


def _cmp_pre_kernel(x_ref, wt_ref, o_ref):
    o_ref[...] = _nt(wt_ref[...], x_ref[...].astype(BF16))


def _cmp_pre_paged_kernel(pt_ref, *refs, npg):
    pages, (wt_ref, o_ref, stage_ref) = refs[:npg], refs[npg:]
    spp = PAGE // CMP_STRIDE
    feat = 4 * HEAD_DIM
    j = _iota((PAGE, PAGE), 0)
    perm = (_iota((PAGE, PAGE), 1) == _mod(j, spp) * CMP_STRIDE + _div(j, spp)).astype(BF16)
    for k in range(npg):
        xp = _nt(perm, pages[k][...].astype(BF16))
        for s in range(CMP_STRIDE):
            stage_ref[s, k * spp:(k + 1) * spp, :] = xp[s * spp:(s + 1) * spp, :]
    acc = None
    for s in range(CMP_STRIDE):
        term = _nt(wt_ref[:, s * feat:(s + 1) * feat], stage_ref[s].astype(BF16))
        acc = term if acc is None else acc + term
    o_ref[...] = acc


def _cmp_pre_prompt(x_seg, wt):
    batch, n_seg, feat = x_seg.shape
    ts = min(128, n_seg)
    return pl.pallas_call(
        _cmp_pre_kernel,
        out_shape=jax.ShapeDtypeStruct((batch, wt.shape[0], n_seg), F32),
        grid=(batch, n_seg // ts),
        in_specs=[pl.BlockSpec((None, ts, feat), lambda b, j: (b, j, 0)),
                  pl.BlockSpec(wt.shape, lambda b, j: (0, 0))],
        out_specs=pl.BlockSpec((None, wt.shape[0], ts), lambda b, j: (b, 0, j)),
        compiler_params=_params(("arbitrary", "arbitrary")),
        name="cmp_pre_prompt",
    )(x_seg, wt)


def _cmp_pre_paged(pool_t, layer, page_table, wt, npg=32):
    nseq, n_pages = page_table.shape
    npg = min(npg, n_pages)
    spp = PAGE // CMP_STRIDE
    feat = pool_t.shape[2]

    def page_spec(k):
        return pl.BlockSpec((None, None, feat, PAGE), lambda b, j, pt: (layer, pt[b, j * npg + k], 0, 0))

    grid_spec = pltpu.PrefetchScalarGridSpec(
        num_scalar_prefetch=1, grid=(nseq, n_pages // npg),
        in_specs=[page_spec(k) for k in range(npg)] + [pl.BlockSpec(wt.shape, lambda b, j, pt: (0, 0))],
        out_specs=pl.BlockSpec((None, wt.shape[0], npg * spp), lambda b, j, pt: (b, 0, j)),
        scratch_shapes=[pltpu.VMEM((CMP_STRIDE, npg * spp, feat), F32)])
    return pl.pallas_call(
        functools.partial(_cmp_pre_paged_kernel, npg=npg),
        out_shape=jax.ShapeDtypeStruct((nseq, wt.shape[0], n_pages * spp), F32),
        grid_spec=grid_spec,
        compiler_params=_params(("arbitrary", "arbitrary")),
        name="cmp_pre_paged",
    )(page_table, *([pool_t] * npg), wt)


def _cmp_post_kernel(pre_ref, bias_ref, w2t_ref, o_ref, k_ref):
    n_seg = pre_ref.shape[1]
    reps = n_seg // LANES
    for c in range(4):
        base = c * 2 * CMP_HIDDEN
        lo = pre_ref[base:base + CMP_HIDDEN, :] + _lane_tile(bias_ref[base:base + CMP_HIDDEN, :], reps)
        hi = (pre_ref[base + CMP_HIDDEN:base + 2 * CMP_HIDDEN, :]
              + _lane_tile(bias_ref[base + CMP_HIDDEN:base + 2 * CMP_HIDDEN, :], reps))
        hid = _silu(lo + pltpu.roll(hi, n_seg - 1, 1))
        o_ref[c * HEAD_DIM:(c + 1) * HEAD_DIM, :] = _dot(w2t_ref[c], hid.astype(BF16))
    k_ref[...] = o_ref[0:2 * HEAD_DIM, :].T


def _cmp_post(pre_t, bias, w2t):
    batch, rows, n_seg = pre_t.shape
    return pl.pallas_call(
        _cmp_post_kernel,
        out_shape=(jax.ShapeDtypeStruct((batch, 4 * HEAD_DIM, n_seg), F32),
                   jax.ShapeDtypeStruct((batch, n_seg, 2 * HEAD_DIM), F32)),
        grid=(batch,),
        in_specs=[pl.BlockSpec((None, rows, n_seg), lambda b: (b, 0, 0)),
                  pl.BlockSpec(bias.shape, lambda b: (0, 0)),
                  pl.BlockSpec(w2t.shape, lambda b: (0, 0, 0))],
        out_specs=(pl.BlockSpec((None, 4 * HEAD_DIM, n_seg), lambda b: (b, 0, 0)),
                   pl.BlockSpec((None, n_seg, 2 * HEAD_DIM), lambda b: (b, 0, 0))),
        compiler_params=_params(("arbitrary",)),
        name="cmp_post",
    )(pre_t, bias, w2t)


def _group_queries(q, g, rows):
    low = _iota((rows, LANES), 1) < HEAD_DIM
    out = []
    for r in range(NSA_REP):
        chunk = 2 * g + r // 2
        x = q[:, chunk * LANES:(chunk + 1) * LANES] * (HEAD_DIM ** -0.5)
        x = jnp.where(low == (r % 2 == 0), x, 0.0)
        if r % 2 != g:
            x = pltpu.roll(x, HEAD_DIM, 1)
        out.append(x)
    return jnp.concatenate(out, axis=0).astype(BF16)


def _slope_col(g, rows):
    r = _div(_iota((NSA_REP * rows, 1), 0), rows)
    return jnp.exp2(-(r + (NSA_REP * g + 1)).astype(F32))


def _softmax_rows(s, valid):
    s = jnp.where(valid, s, NEG_INF)
    m = jnp.max(s, axis=1, keepdims=True)
    p = jnp.where(valid, jnp.exp(s - m), 0.0)
    return p / jnp.maximum(jnp.sum(p, axis=1, keepdims=True), 1e-30)


def _tile_rows(x, n):
    return jnp.concatenate([x] * n, axis=0)


def _cover(n_cmp_pad, n_blk_pad):
    n = _iota((n_cmp_pad, n_blk_pad), 0) * CMP_STRIDE
    j = _iota((n_cmp_pad, n_blk_pad), 1) * SEL_BLOCK
    return ((n < j + SEL_BLOCK) & (n + CMP_LEN - 1 >= j)).astype(BF16)


def _select_blocks(imp, tpos, n_sel, axis=1):
    blk = _iota(imp.shape, axis)
    cur = _div(tpos, SEL_BLOCK)
    forced = (blk == 0) | (blk == cur) | (blk == cur - 1)
    imp = jnp.where(forced, FORCE, imp)
    imp = jnp.where(blk * SEL_BLOCK <= tpos, imp, NEG_INF)
    imp = jnp.where(blk < n_sel, imp, -jnp.inf)
    blkf = blk.astype(F32)

    def body(_, carry):
        imp, sel = carry
        m = jnp.max(imp, axis=axis, keepdims=True)
        idx = jnp.min(jnp.where(imp == m, blkf, 1e9), axis=axis, keepdims=True)
        hit = blkf == idx
        return jnp.where(hit, -jnp.inf, imp), jnp.where(hit, 1.0, sel)

    _, sel = lax.fori_loop(0, min(SEL_TOPK, n_sel), body, (imp, jnp.zeros_like(imp)))
    return sel


def _cmp_branch(qm, kct, vct, dist, valid, slope, rows):
    s = _dot(qm, kct) - slope * _tile_rows(dist, NSA_REP)
    p = _softmax_rows(s, _tile_rows(valid, NSA_REP))
    o = _nt(p.astype(BF16), vct)
    psum = p[0:rows] + p[rows:2 * rows] + p[2 * rows:3 * rows] + p[3 * rows:4 * rows]
    return o, psum


def _importance(psum, cover):
    hi = psum.astype(BF16)
    lo = (psum - hi.astype(F32)).astype(BF16)
    return _dot(hi, cover) + _dot(lo, cover)


def _gate_cols(gates, g, rows):
    lane = _iota(gates.shape, 1)
    cols = []
    for c in range(3):
        per_head = [jnp.sum(jnp.where(lane == (g * NSA_REP + r) * 3 + c, gates, 0.0), axis=1, keepdims=True)
                    for r in range(NSA_REP)]
        cols.append(jnp.concatenate(per_head, axis=0))
    return cols


def _place_group(o, g, rows):
    low = _iota((rows, LANES), 1) < HEAD_DIM
    chunks = []
    for kk in range(2):
        a = o[(2 * kk) * rows:(2 * kk + 1) * rows]
        b = o[(2 * kk + 1) * rows:(2 * kk + 2) * rows]
        if g == 1:
            a = pltpu.roll(a, HEAD_DIM, 1)
        else:
            b = pltpu.roll(b, HEAD_DIM, 1)
        chunks.append(jnp.where(low, a, b))
    return chunks


def _online_update(s, v, m_ref, l_ref, acc_ref, v_transposed=False):
    m_prev = m_ref[...]
    m_new = jnp.maximum(m_prev, jnp.max(s, axis=1, keepdims=True))
    alpha = jnp.exp(m_prev - m_new)
    p = jnp.exp(s - m_new)
    l_ref[...] = alpha * l_ref[...] + jnp.sum(p, axis=1, keepdims=True)
    pb = p.astype(BF16)
    acc_ref[...] = alpha * acc_ref[...] + (_nt(pb, v) if v_transposed else _dot(pb, v))
    m_ref[...] = m_new


def _softmax_cols(st, valid):
    st = jnp.where(valid, st, NEG_INF)
    m = jnp.max(st, axis=0, keepdims=True)
    p = jnp.where(valid, jnp.exp(st - m), 0.0)
    return p / jnp.maximum(jnp.sum(p, axis=0, keepdims=True), 1e-30)


def _online_update_t(st, vt, m_ref, l_ref, acc_ref):
    m_prev = m_ref[...]
    m_new = jnp.maximum(m_prev, jnp.max(st, axis=0, keepdims=True))
    alpha = jnp.exp(m_prev - m_new)
    pt = jnp.exp(st - m_new)
    l_ref[...] = alpha * l_ref[...] + jnp.sum(pt, axis=0, keepdims=True)
    acc_ref[...] = alpha * acc_ref[...] + _dot(vt, pt.astype(BF16))
    m_ref[...] = m_new


def _nsa_prompt_kernel(qt_ref, gt_ref, kc_ref, kvt_ref, selk_ref, selvt_ref, *rest, n_cmp, tk):
    nwin = WINDOW // LANES + 1
    wk, wvt = rest[:nwin], rest[nwin:2 * nwin]
    o_ref, kaug_ref, selvt_bf, qaug_ref, selt_ref, m_ref, l_ref, acc_ref = rest[2 * nwin:]
    i = pl.program_id(1)
    tq = qt_ref.shape[1]
    seq = selk_ref.shape[0]
    n_seg = kc_ref.shape[0]
    n_blk = seq // SEL_BLOCK
    blk_pad = max(LANES, n_blk)
    bpt = tk // SEL_BLOCK
    t0 = i * tq

    @pl.when(i == 0)
    def _():
        local = _iota((tk, LANES), 0)
        lane = _iota((tk, LANES), 1)
        ext = jnp.where(lane < bpt, (_div(local, SEL_BLOCK) == lane).astype(F32),
                        jnp.where(lane == bpt, (local - _mod(local, 2)).astype(F32),
                                  jnp.where(lane == bpt + 1, _mod(local, 2).astype(F32),
                                            jnp.where(lane == bpt + 2, 1.0, 0.0)))).astype(BF16)
        for c in range(seq // tk):
            kaug_ref[c * tk:(c + 1) * tk, :] = jnp.concatenate(
                [selk_ref[c * tk:(c + 1) * tk, :].astype(BF16), ext], axis=1)
            selvt_bf[c] = selvt_ref[:, c * tk:(c + 1) * tk].astype(BF16)

    tpos = t0 + _iota((1, tq), 1)
    kc = kc_ref[...].astype(BF16)
    vct = kvt_ref[2 * HEAD_DIM:4 * HEAD_DIM, :].astype(BF16)
    n_col = _iota((n_seg, 1), 0)
    dist_c = (tpos - (n_col * CMP_STRIDE + CMP_LEN - 1)).astype(F32)
    dist_c = _lane_tile(dist_c, NSA_REP)
    valid_c = (dist_c >= 0) & (n_col < n_cmp)
    blk = _iota((blk_pad, n_seg), 0) * SEL_BLOCK
    seg = _iota((blk_pad, n_seg), 1) * CMP_STRIDE
    cover_t = ((seg < blk + SEL_BLOCK) & (seg + CMP_LEN - 1 >= blk)).astype(BF16)
    gates_t = jax.nn.sigmoid(gt_ref[...])
    qt = qt_ref[...] * (HEAD_DIM ** -0.5)

    kw = jnp.concatenate([w[...] for w in wk], axis=0).astype(BF16)
    vwt = jnp.concatenate([w[...] for w in wvt], axis=1).astype(BF16)
    wpos = t0 - WINDOW + _iota((kw.shape[0], 1), 0)
    dist_w = (tpos - wpos).astype(F32)
    dist_w = _lane_tile(dist_w, NSA_REP)
    valid_w = (dist_w >= 0) & (dist_w < WINDOW) & (wpos >= 0)

    n_tiles = _div(t0 + tq + tk - 1, tk)
    zero = jnp.zeros((HEAD_DIM, tq), F32)
    out_rows = []
    for g in range(NSA_KV):
        heads = [qt[(g * NSA_REP + r) * HEAD_DIM:(g * NSA_REP + r + 1) * HEAD_DIM, :] for r in range(NSA_REP)]
        qmt = jnp.concatenate([jnp.concatenate([h, zero] if g == 0 else [zero, h], axis=0) for h in heads],
                              axis=1).astype(BF16)
        slope = jnp.concatenate([jnp.full((1, tq), 2.0 ** -(NSA_REP * g + r + 1), F32) for r in range(NSA_REP)],
                                axis=1)

        pt = _softmax_cols(_dot(kc, qmt) - slope * dist_c, valid_c)
        o_cmp = _dot(vct, pt.astype(BF16))
        psum = pt[:, 0:tq] + pt[:, tq:2 * tq] + pt[:, 2 * tq:3 * tq] + pt[:, 3 * tq:4 * tq]
        hi = psum.astype(BF16)
        lo = (psum - hi.astype(F32)).astype(BF16)
        sel_t = _select_blocks(_dot(cover_t, hi) + _dot(cover_t, lo), tpos, n_blk, axis=0)
        selt_ref[...] = sel_t
        qaug_ref[0:LANES, :] = qmt
        qaug_ref[LANES + 16:2 * LANES, :] = jnp.zeros((LANES - 16, NSA_REP * tq), BF16)

        m_ref[...] = jnp.full_like(m_ref, -1e29)
        l_ref[...] = jnp.zeros_like(l_ref)
        acc_ref[...] = jnp.zeros_like(acc_ref)
        row8 = _iota((8, NSA_REP * tq), 0)
        tpos4 = _lane_tile(tpos, NSA_REP)

        def tile(j, _):
            picked = selt_ref[pl.ds(pl.multiple_of(j * bpt, bpt), bpt), :]

            def attend(diagonal):
                shift = (j * tk - t0).astype(F32)
                mask_rows = (_lane_tile(picked, NSA_REP) - 1.0) * 1e30
                bias_rows = jnp.where(row8 < 2, slope, jnp.where(row8 == 2, slope * shift, 0.0))
                qaug_ref[LANES:LANES + 16, :] = jnp.concatenate([mask_rows, bias_rows], axis=0).astype(BF16)
                r = pl.multiple_of(j * tk, tk)
                st = _dot(kaug_ref[pl.ds(r, tk), :], qaug_ref[...])
                if diagonal:
                    st = jnp.where(j * tk + _iota((tk, 1), 0) <= tpos4, st, -jnp.inf)
                _online_update_t(st, selvt_bf[j], m_ref, l_ref, acc_ref)

            active = jnp.max(picked) > 0.5
            pl.when(active & (j < n_tiles - 1))(lambda: attend(False))
            pl.when(active & (j == n_tiles - 1))(lambda: attend(True))
            return 0

        lax.fori_loop(0, n_tiles, tile, 0)
        o_sel = acc_ref[...] / jnp.maximum(l_ref[...], 1e-30)

        o_win = _dot(vwt, _softmax_cols(_dot(kw, qmt) - slope * dist_w, valid_w).astype(BF16))

        def gate_row(c):
            return jnp.concatenate([gates_t[(g * NSA_REP + r) * 3 + c:(g * NSA_REP + r) * 3 + c + 1, :]
                                    for r in range(NSA_REP)], axis=1)

        o = gate_row(0) * o_cmp + gate_row(1) * o_sel + gate_row(2) * o_win
        out_rows += [o[g * HEAD_DIM:(g + 1) * HEAD_DIM, r * tq:(r + 1) * tq] for r in range(NSA_REP)]
    o_ref[...] = jnp.concatenate(out_rows, axis=0).T


def _nsa_prompt(u, ut, kc, kvt, batch, seq):
    tq = LANES
    nq = seq // tq
    tk = min(512, seq)
    n_seg = kvt.shape[2]
    n_cmp = seq // CMP_STRIDE - 1
    nwin = WINDOW // tq + 1

    def win_row(b, i, k):
        return b * nq + jnp.maximum(i - WINDOW // tq + k, 0)

    return pl.pallas_call(
        functools.partial(_nsa_prompt_kernel, n_cmp=n_cmp, tk=tk),
        out_shape=jax.ShapeDtypeStruct((batch * seq, 512), F32),
        grid=(batch, nq),
        in_specs=[pl.BlockSpec((512, tq), lambda b, i: (T_NQ // 512, b * nq + i)),
                  pl.BlockSpec((LANES, tq), lambda b, i: (T_NG // LANES, b * nq + i)),
                  pl.BlockSpec((None, n_seg, LANES), lambda b, i: (b, 0, 0)),
                  pl.BlockSpec((None, 4 * HEAD_DIM, n_seg), lambda b, i: (b, 0, 0)),
                  pl.BlockSpec((seq, LANES), lambda b, i: (b, C_SEL // LANES)),
                  pl.BlockSpec((LANES, seq), lambda b, i: (T_SELV // LANES, b))]
                 + [pl.BlockSpec((tq, LANES), lambda b, i, k=k: (win_row(b, i, k), C_WIN // LANES))
                    for k in range(nwin)]
                 + [pl.BlockSpec((LANES, tq), lambda b, i, k=k: (T_WINV // LANES, win_row(b, i, k)))
                    for k in range(nwin)],
        out_specs=pl.BlockSpec((tq, 512), lambda b, i: (b * nq + i, 0)),
        scratch_shapes=[pltpu.VMEM((seq, 2 * LANES), BF16), pltpu.VMEM((seq // tk, LANES, tk), BF16),
                        pltpu.VMEM((2 * LANES, NSA_REP * tq), BF16),
                        pltpu.VMEM((max(LANES, seq // SEL_BLOCK), tq), F32),
                        pltpu.VMEM((1, NSA_REP * tq), F32), pltpu.VMEM((1, NSA_REP * tq), F32),
                        pltpu.VMEM((LANES, NSA_REP * tq), F32)],
        compiler_params=_params(("arbitrary", "arbitrary")),
        name="nsa_prompt",
    )(ut, ut, kc, kvt, u, ut, *([u] * nwin), *([ut] * nwin))


def _nsa_cmp_sample_kernel(q_ref, kvt_ref, o_ref, sel_ref, *, past, n_cmp, n_sel):
    rows = SAMPLE_ROWS
    n_seg = kvt_ref.shape[1]
    blk_pad = sel_ref.shape[1]
    tpos = past + _iota((rows, 1), 0)
    kct = kvt_ref[0:2 * HEAD_DIM, :].astype(BF16)
    vct = kvt_ref[2 * HEAD_DIM:4 * HEAD_DIM, :].astype(BF16)
    n_idx = _iota((rows, n_seg), 1)
    dist = (tpos - (n_idx * CMP_STRIDE + CMP_LEN - 1)).astype(F32)
    valid = (dist >= 0) & (n_idx < n_cmp)
    cover = _cover(n_seg, blk_pad)
    q = q_ref[...]
    for g in range(NSA_KV):
        qm = _group_queries(q, g, rows)
        o, psum = _cmp_branch(qm, kct, vct, dist, valid, _slope_col(g, rows), rows)
        o_ref[g * NSA_REP * rows:(g + 1) * NSA_REP * rows, :] = o
        sel_ref[g * rows:(g + 1) * rows, :] = _select_blocks(_importance(psum, cover), tpos, n_sel)


def _nsa_cmp_sample(u_s, kvt, nseq, past, n_sel, blk_pad):
    n_seg = kvt.shape[2]
    return pl.pallas_call(
        functools.partial(_nsa_cmp_sample_kernel, past=past, n_cmp=n_seg - 1, n_sel=n_sel),
        out_shape=(jax.ShapeDtypeStruct((nseq, NSA_KV * NSA_REP * SAMPLE_ROWS, LANES), F32),
                   jax.ShapeDtypeStruct((nseq, NSA_KV * SAMPLE_ROWS, blk_pad), F32)),
        grid=(nseq,),
        in_specs=[pl.BlockSpec((SAMPLE_ROWS, 512), lambda b: (b, C_NQ // 512)),
                  pl.BlockSpec((None, 4 * HEAD_DIM, n_seg), lambda b: (b, 0, 0))],
        out_specs=(pl.BlockSpec((None, NSA_KV * NSA_REP * SAMPLE_ROWS, LANES), lambda b: (b, 0, 0)),
                   pl.BlockSpec((None, NSA_KV * SAMPLE_ROWS, blk_pad), lambda b: (b, 0, 0))),
        compiler_params=_params(("arbitrary",)),
        name="nsa_cmp_sample",
    )(u_s, kvt)


def _stack_groups(fn):
    return jnp.concatenate([fn(g) for g in range(NSA_KV)], axis=0)


def _nsa_selwin_sample_kernel(pt_ref, src_ref, act_ref, *refs, npg, past):
    pages = refs[:npg]
    (q_ref, gate_ref, selm_ref, ocmp_ref, winbuf_ref, selnew_ref, winnew_ref, o_ref,
     qm_ref, selrows_ref, m_ref, l_ref, acc_ref, pad_ref, kt_ref, vt_ref) = refs[npg:]
    j = pl.program_id(1)
    rows = SAMPLE_ROWS
    nrow = NSA_KV * NSA_REP * rows
    blk_pad = selm_ref.shape[1]
    tk = npg * PAGE
    tpos = past + _mod(_iota((nrow, 1), 0), rows)
    slope = _stack_groups(lambda g: _slope_col(g, rows))

    @pl.when(j == 0)
    def _():
        q = q_ref[...]
        qm_ref[...] = _stack_groups(lambda g: _group_queries(q, g, rows))
        selrows_ref[...] = _stack_groups(
            lambda g: _tile_rows(selm_ref[g * rows:(g + 1) * rows, :], NSA_REP)).astype(BF16)
        m_ref[...] = jnp.full_like(m_ref, NEG_INF)
        l_ref[...] = jnp.zeros_like(l_ref)
        acc_ref[...] = jnp.zeros_like(acc_ref)
        pad_ref[...] = jnp.zeros_like(pad_ref)

    def attend(score_fn, v_fn, v_transposed, kpos):
        n = kpos.shape[1]
        expand = (_iota((blk_pad, n), 0) == _div(jnp.broadcast_to(kpos, (blk_pad, n)), SEL_BLOCK)).astype(BF16)
        chosen = _dot(selrows_ref[...], expand)
        dist = (tpos - kpos).astype(F32)
        ok = (chosen > 0.5) & (dist >= 0)
        s = jnp.where(ok, score_fn() - slope * dist, -jnp.inf)
        _online_update(s, v_fn(), m_ref, l_ref, acc_ref, v_transposed)

    @pl.when(act_ref[pl.program_id(0), j] > 0)
    def _():
        for k in range(npg):
            kt_ref[:, k * PAGE:(k + 1) * PAGE] = pages[k][0].astype(BF16)
            vt_ref[:, k * PAGE:(k + 1) * PAGE] = pages[k][1].astype(BF16)
        attend(lambda: _dot(qm_ref[...], kt_ref[...]), lambda: vt_ref[...], True, j * tk + _iota((1, tk), 1))

    @pl.when(j == pl.num_programs(1) - 1)
    def _():
        pad_ref[0:rows, :] = selnew_ref[...]
        new = pad_ref[...]
        attend(lambda: _nt(qm_ref[...], new[:, 0:LANES].astype(BF16)),
               lambda: new[:, LANES:2 * LANES].astype(BF16), False, past + _iota((1, PAGE), 1))
        o_sel = acc_ref[...] / jnp.maximum(l_ref[...], 1e-30)

        pad_ref[0:rows, :] = winnew_ref[...]
        wnew = pad_ref[...]
        wb = winbuf_ref.shape[2]
        nw = wb + PAGE
        widx = _iota((1, nw), 1)
        wpos = jnp.where(widx < wb, past - wb + widx, past + widx - wb)
        dist = (tpos - wpos).astype(F32)
        valid = (dist >= 0) & (dist < WINDOW) & (wpos >= 0)
        s = jnp.concatenate([_dot(qm_ref[...], winbuf_ref[0].astype(BF16)),
                             _nt(qm_ref[...], wnew[:, 0:LANES].astype(BF16))], axis=1) - slope * dist
        p = _softmax_rows(s, valid).astype(BF16)
        o_win = _nt(p[:, 0:wb], winbuf_ref[1].astype(BF16)) + _dot(p[:, wb:], wnew[:, LANES:2 * LANES].astype(BF16))

        gates = jax.nn.sigmoid(gate_ref[...])
        o_cmp = ocmp_ref[...]
        for g in range(NSA_KV):
            gc, gs, gw = _gate_cols(gates, g, rows)
            sl = slice(g * NSA_REP * rows, (g + 1) * NSA_REP * rows)
            chunks = _place_group(gc * o_cmp[sl] + gs * o_sel[sl] + gw * o_win[sl], g, rows)
            for kk in range(2):
                o_ref[:, g * 2 * LANES + kk * LANES:g * 2 * LANES + (kk + 1) * LANES] = chunks[kk]


def _nsa_selwin_sample(u_s, pool_sel_t, layer, page_table, selm, o_cmp, win_buf_t, past, npg=16):
    nseq, n_pages = page_table.shape
    npg = min(npg, n_pages)
    blk_pad = selm.shape[2]
    nrow = NSA_KV * NSA_REP * SAMPLE_ROWS
    wb = win_buf_t.shape[4]

    steps = n_pages // npg
    bps = npg * PAGE // SEL_BLOCK
    picked = selm[:, :, :steps * bps].reshape(nseq, selm.shape[1], steps, bps)
    active = jnp.max(picked, axis=(1, 3)) > 0.5
    idx = jnp.arange(steps, dtype=jnp.int32)[None, :]
    src = jnp.maximum(lax.cummax(jnp.where(active, idx, -1), axis=1), 0).astype(jnp.int32)
    act = active.astype(jnp.int32)

    def page_spec(k):
        return pl.BlockSpec((None, None, 2, LANES, PAGE),
                            lambda b, j, pt, src, act: (layer, pt[b, src[b, j] * npg + k], 0, 0, 0))

    grid_spec = pltpu.PrefetchScalarGridSpec(
        num_scalar_prefetch=3, grid=(nseq, steps),
        in_specs=[page_spec(k) for k in range(npg)] + [
            pl.BlockSpec((SAMPLE_ROWS, 512), lambda b, j, *_: (b, C_NQ // 512)),
            pl.BlockSpec((SAMPLE_ROWS, LANES), lambda b, j, *_: (b, C_NG // LANES)),
            pl.BlockSpec((None, NSA_KV * SAMPLE_ROWS, blk_pad), lambda b, j, *_: (b, 0, 0)),
            pl.BlockSpec((None, nrow, LANES), lambda b, j, *_: (b, 0, 0)),
            pl.BlockSpec((None, None, 2, LANES, wb), lambda b, j, *_: (layer, b, 0, 0, 0)),
            pl.BlockSpec((SAMPLE_ROWS, 256), lambda b, j, *_: (b, C_SEL // 256)),
            pl.BlockSpec((SAMPLE_ROWS, 256), lambda b, j, *_: (b, C_WIN // 256))],
        out_specs=pl.BlockSpec((SAMPLE_ROWS, 512), lambda b, j, *_: (b, 0)),
        scratch_shapes=[pltpu.VMEM((nrow, LANES), BF16), pltpu.VMEM((nrow, blk_pad), BF16),
                        pltpu.VMEM((nrow, 1), F32), pltpu.VMEM((nrow, 1), F32),
                        pltpu.VMEM((nrow, LANES), F32), pltpu.VMEM((PAGE, 256), F32),
                        pltpu.VMEM((LANES, npg * PAGE), BF16), pltpu.VMEM((LANES, npg * PAGE), BF16)])
    return pl.pallas_call(
        functools.partial(_nsa_selwin_sample_kernel, npg=npg, past=past),
        out_shape=jax.ShapeDtypeStruct((nseq * SAMPLE_ROWS, 512), F32),
        grid_spec=grid_spec,
        compiler_params=_params(("arbitrary", "arbitrary")),
        name="nsa_selwin_sample",
    )(page_table, src, act, *([pool_sel_t] * npg), u_s, u_s, selm, o_cmp, win_buf_t, u_s, u_s)


def _fox_sample_kernel(pt_ref, *refs, npg):
    pages = refs[:npg]
    (q_ref, cpast_ref, clast_ref, knew_ref, vnew_ref, ff_ref, bf_ref, o_ref, logf_ref,
     qbd_ref, m_ref, l_ref, acc_ref, kpad_ref, vpad_ref, fpad_ref) = refs[npg:]
    j = pl.program_id(1)
    rows = SAMPLE_ROWS
    width = FOX_HEADS * HEAD_DIM
    nrow = FOX_HEADS * rows

    @pl.when(j == 0)
    def _():
        q = q_ref[...] * (HEAD_DIM ** -0.5)
        head = _div(_iota((rows, width), 1), HEAD_DIM)
        qbd_ref[...] = jnp.concatenate(
            [jnp.where(head == h, q, 0.0) for h in range(FOX_HEADS)], axis=0).astype(BF16)
        m_ref[...] = jnp.full_like(m_ref, NEG_INF)
        l_ref[...] = jnp.zeros_like(l_ref)
        acc_ref[...] = jnp.zeros_like(acc_ref)
        kpad_ref[...] = jnp.zeros_like(kpad_ref)
        vpad_ref[...] = jnp.zeros_like(vpad_ref)
        fpad_ref[...] = jnp.zeros_like(fpad_ref)

    def head_rows(x):
        return jnp.concatenate([jnp.broadcast_to(x[h:h + 1, :], (rows, x.shape[1])) for h in range(FOX_HEADS)], axis=0)

    c_last = jnp.max(jnp.where(_iota(clast_ref.shape, 1) == clast_ref.shape[1] - 1, clast_ref[...], -jnp.inf),
                     axis=1, keepdims=True)
    qbd = qbd_ref[...]
    s = jnp.concatenate([_dot(qbd, pages[k][0].astype(BF16)) for k in range(npg)], axis=1)
    s = s + head_rows(c_last - cpast_ref[...])
    m_prev = m_ref[...]
    m_new = jnp.maximum(m_prev, jnp.max(s, axis=1, keepdims=True))
    alpha = jnp.exp(m_prev - m_new)
    p = jnp.exp(s - m_new)
    l_ref[...] = alpha * l_ref[...] + jnp.sum(p, axis=1, keepdims=True)
    pb = p.astype(BF16)
    acc = alpha * acc_ref[...]
    for k in range(npg):
        acc = acc + _nt(pb[:, k * PAGE:(k + 1) * PAGE], pages[k][1].astype(BF16))
    acc_ref[...] = acc
    m_ref[...] = m_new

    @pl.when(j == pl.num_programs(1) - 1)
    def _():
        lf = _log_sigmoid(ff_ref[...] + bf_ref[...])
        logf_ref[...] = lf
        fpad_ref[0:rows, :] = lf
        c_new, _ = _head_major_cumsum(fpad_ref[...], 0.0)
        kpad_ref[0:rows, :] = knew_ref[...]
        vpad_ref[0:rows, :] = vnew_ref[...]
        s = _nt(qbd_ref[...], kpad_ref[...].astype(BF16)) - head_rows(c_new)
        causal = _iota((nrow, PAGE), 1) <= _mod(_iota((nrow, PAGE), 0), rows)
        _online_update(jnp.where(causal, s, -jnp.inf), vpad_ref[...].astype(BF16), m_ref, l_ref, acc_ref)
        o = acc_ref[...] / jnp.maximum(l_ref[...], 1e-30)
        head = _div(_iota((rows, width), 1), HEAD_DIM)
        out = jnp.zeros((rows, width), F32)
        for h in range(FOX_HEADS):
            out = jnp.where(head == h, o[h * rows:(h + 1) * rows], out)
        o_ref[...] = out


def _fox_sample(u_s, pool_kv_t, layer, page_table, c_past, bf_pad, npg=32):
    nseq, n_pages = page_table.shape
    npg = min(npg, n_pages)
    width = FOX_HEADS * HEAD_DIM
    nrow = FOX_HEADS * SAMPLE_ROWS
    past = n_pages * PAGE
    tk = npg * PAGE

    def page_spec(k):
        return pl.BlockSpec((None, None, 2, width, PAGE), lambda b, j, pt: (layer, pt[b, j * npg + k], 0, 0, 0))

    grid_spec = pltpu.PrefetchScalarGridSpec(
        num_scalar_prefetch=1, grid=(nseq, n_pages // npg),
        in_specs=[page_spec(k) for k in range(npg)] + [
            pl.BlockSpec((SAMPLE_ROWS, width), lambda b, j, pt: (b, C_FQ // width)),
            pl.BlockSpec((None, FOX_HEADS, tk), lambda b, j, pt: (b, 0, j)),
            pl.BlockSpec((None, FOX_HEADS, LANES), lambda b, j, pt: (b, 0, past // LANES - 1)),
            pl.BlockSpec((SAMPLE_ROWS, width), lambda b, j, pt: (b, C_FKV // width)),
            pl.BlockSpec((SAMPLE_ROWS, width), lambda b, j, pt: (b, C_FKV // width + 1)),
            pl.BlockSpec((SAMPLE_ROWS, LANES), lambda b, j, pt: (b, C_FF // LANES)),
            pl.BlockSpec((1, LANES), lambda b, j, pt: (0, 0))],
        out_specs=(pl.BlockSpec((SAMPLE_ROWS, width), lambda b, j, pt: (b, 0)),
                   pl.BlockSpec((SAMPLE_ROWS, LANES), lambda b, j, pt: (b, 0))),
        scratch_shapes=[pltpu.VMEM((nrow, width), BF16), pltpu.VMEM((nrow, 1), F32), pltpu.VMEM((nrow, 1), F32),
                        pltpu.VMEM((nrow, width), F32), pltpu.VMEM((PAGE, width), F32),
                        pltpu.VMEM((PAGE, width), F32), pltpu.VMEM((PAGE, LANES), F32)])
    return pl.pallas_call(
        functools.partial(_fox_sample_kernel, npg=npg),
        out_shape=(jax.ShapeDtypeStruct((nseq * SAMPLE_ROWS, width), F32),
                   jax.ShapeDtypeStruct((nseq * SAMPLE_ROWS, LANES), F32)),
        grid_spec=grid_spec,
        compiler_params=_params(("arbitrary", "arbitrary")),
        name="fox_sample",
    )(page_table, *([pool_kv_t] * npg), u_s, c_past, c_past, u_s, u_s, u_s, bf_pad)


def _outproj_kernel(on_ref, of_ref, or_ref, nz_ref, fz_ref, rz_ref, g0_ref, g1_ref, g2_ref, x_ref, gate_ref,
                    wb_ref, wo_ref, fg_ref, *out_refs, final):
    merged = None
    for n, (o_ref, z_ref, g_ref) in enumerate(((on_ref, nz_ref, g0_ref), (of_ref, fz_ref, g1_ref),
                                               (or_ref, rz_ref, g2_ref))):
        br = (o_ref[...] * _silu(z_ref[...])).astype(BF16)
        term = jax.nn.sigmoid(g_ref[...]) * _dot(br, wb_ref[n])
        merged = term if merged is None else merged + term
    y = _dot(merged.astype(BF16), wo_ref[...])
    x = x_ref[...] + gate_ref[...] * y
    out_refs[0][...] = x
    if final:
        ms = jnp.mean(x * x, axis=-1, keepdims=True)
        out_refs[1][...] = x * lax.rsqrt(ms + EPS) * fg_ref[...]


def _outproj(u, o_nsa, o_fox, o_ret, x2d, gate, wb, wo, final_g, rows_per_mod, final):
    n, d = x2d.shape
    tm = min(256, rows_per_mod, n)
    bw = 512

    def row(i):
        return (i, 0)

    out_shape = [jax.ShapeDtypeStruct((n, d), F32)] * (2 if final else 1)
    out_specs = [pl.BlockSpec((tm, d), row)] * (2 if final else 1)
    return pl.pallas_call(
        functools.partial(_outproj_kernel, final=final),
        out_shape=tuple(out_shape),
        grid=(n // tm,),
        in_specs=[pl.BlockSpec((tm, bw), row)] * 3
                 + [pl.BlockSpec((tm, bw), lambda i, c=c: (i, c // bw)) for c in (C_NZ, C_FZ, C_RZ)]
                 + [pl.BlockSpec((tm, d), lambda i, c=c: (i, C_MG // d + c)) for c in range(3)]
                 + [pl.BlockSpec((tm, d), row),
                    _mod_spec(gate, tm, rows_per_mod, d, 1),
                    pl.BlockSpec(wb.shape, lambda i: (0, 0, 0)),
                    pl.BlockSpec(wo.shape, lambda i: (0, 0)),
                    pl.BlockSpec((1, d), lambda i: (0, 0))],
        out_specs=tuple(out_specs),
        compiler_params=_params(("arbitrary",)),
        name="outproj_final" if final else "outproj",
    )(o_nsa, o_fox, o_ret, u, u, u, u, u, u, x2d, gate, wb, wo, final_g.reshape(1, d))


def _pad_cols(w, width):
    return jnp.pad(w, ((0, 0), (0, width - w.shape[1])))


def _pack_w_in(w):
    o = 0
    seg = {}
    for name, width in (("nq", 512), ("nkv", 768), ("ng", 24), ("nz", 512), ("fq", 512), ("fk", 512),
                        ("fv", 512), ("ff", 8), ("fz", 512), ("rq", 512), ("rk", 512), ("rv", 512),
                        ("rz", 512), ("mg", 3 * D_MODEL)):
        seg[name] = w[:, o:o + width]
        o += width
    order = [seg["nq"], seg["fq"], seg["rq"], seg["rk"], seg["rv"], seg["nz"], seg["fz"], seg["rz"],
             seg["fk"], seg["fv"], seg["mg"], seg["nkv"], _pad_cols(seg["ng"], LANES), _pad_cols(seg["ff"], LANES)]
    w_pad = jnp.concatenate(order, axis=1).astype(BF16)
    nkv = seg["nkv"]
    t_order = [seg["nq"], seg["fq"], seg["fv"], nkv[:, 256 + LANES:512], nkv[:, 512 + LANES:768],
               _pad_cols(seg["ng"], LANES)]
    wt_sel = jnp.concatenate(t_order, axis=1).T.astype(BF16)
    wt_rows = jnp.concatenate([nkv, seg["fk"], seg["fv"]], axis=1).T.astype(BF16)
    return w_pad, wt_sel, wt_rows


def _pack_cmp(w1k, w1v, posk, posv):
    def parts(w):
        return jnp.stack([w[:CMP_STRIDE], w[CMP_STRIDE:]], axis=2)

    w4 = jnp.stack([parts(w1k), parts(w1k), parts(w1v), parts(w1v)])
    big = jnp.einsum("cC,csdph->scdCph", jnp.eye(4, dtype=w4.dtype), w4)
    wt = big.reshape(CMP_FEAT, 4 * 2 * CMP_HIDDEN).T.astype(BF16)

    def pos_row(part):
        sl = slice(part * CMP_STRIDE, (part + 1) * CMP_STRIDE)
        return jnp.stack([posk[sl], posk[sl], posv[sl], posv[sl]], axis=1).reshape(CMP_FEAT)

    p_pad = jnp.zeros((LANES, CMP_FEAT), F32).at[0].set(pos_row(0)).at[1].set(pos_row(1))
    return wt, p_pad


def kernel(x_prompt, x_sample, cache_nsa_cmp_kv, cache_nsa_sel_kv, cache_nsa_win_kv, cache_fox_kv, cache_fox_logf, state_ret, page_table, c_prompt, c_sample, norm_g, w_ada, b_ada, w_in, b_forget, w_cmp_k1, w_cmp_k2, pos_cmp_k, w_cmp_v1, w_cmp_v2, pos_cmp_v, w_branch, w_out, final_g):
    batch, seq, d = x_prompt.shape
    nseq, dec = x_sample.shape[:2]
    depth = norm_g.shape[0]
    n_pool = cache_nsa_cmp_kv.shape[1]
    n_pages = page_table.shape[1]
    past = n_pages * PAGE
    wb_len = cache_nsa_win_kv.shape[2]
    assert d == D_MODEL and dec <= SAMPLE_ROWS and past >= wb_len and seq % 512 == 0
    n_sel = -(-(past + dec) // SEL_BLOCK)
    blk_pad = -(-n_sel // LANES) * LANES
    srows = nseq * SAMPLE_ROWS

    hp = x_prompt.reshape(batch * seq, d)
    hs = jnp.pad(x_sample, ((0, 0), (0, SAMPLE_ROWS - dec), (0, 0))).reshape(srows, d)
    c_rows = 8
    c_all = jnp.concatenate([jnp.pad(c_prompt, ((0, c_rows - batch), (0, 0))), c_sample], axis=0)
    lg = jnp.log1p(-jnp.exp2(-5.0 - jnp.arange(RET_HEADS, dtype=F32)))
    lg_lanes = jnp.repeat(lg, HEAD_DIM).reshape(RET_HEADS // 2, 1, LANES)
    zero_state = jnp.zeros((batch, RET_HEADS, HEAD_DIM, HEAD_DIM), F32)
    pool_cmp = jnp.transpose(cache_nsa_cmp_kv, (0, 1, 3, 4, 5, 2)).reshape(depth, n_pool, 256, PAGE)
    pool_sel = jnp.transpose(cache_nsa_sel_kv, (0, 1, 3, 4, 5, 2)).reshape(depth, n_pool, 2, LANES, PAGE)
    pool_fox = jnp.transpose(cache_fox_kv, (0, 1, 3, 4, 5, 2)).reshape(depth, n_pool, 2, FOX_HEADS * HEAD_DIM, PAGE)
    pool_logf = jnp.transpose(cache_fox_logf, (0, 1, 3, 2))
    win_buf = jnp.transpose(cache_nsa_win_kv, (0, 1, 3, 4, 5, 2)).reshape(depth, nseq, 2, LANES, wb_len)

    outs_p, outs_s = [], []
    for l in range(depth):
        w_pad, wt_sel, wt_rows = _pack_w_in(w_in[l])
        wt, p_pad = _pack_cmp(w_cmp_k1[l], w_cmp_v1[l], pos_cmp_k[l], pos_cmp_v[l])
        w2t = jnp.stack([w_cmp_k2[l].T, w_cmp_k2[l].T, w_cmp_v2[l].T, w_cmp_v2[l].T]).astype(BF16)
        wb_bf = w_branch[l].astype(BF16)
        wo_bf = w_out[l].astype(BF16)
        bf_pad = _pad_cols(b_forget[l].reshape(1, FOX_HEADS), LANES)
        final = l == depth - 1

        mod = _adaln(c_all, w_ada[l].astype(BF16), b_ada[l])
        shift_p, scale_p, gate_p = [m.reshape(batch, 1, d) for m in jnp.split(mod[:batch], 3, axis=1)]
        shift_s, scale_s, gate_s = [jnp.repeat(m, SAMPLE_ROWS, axis=0) for m in jnp.split(mod[c_rows:], 3, axis=1)]
        cmp_bias = _cmp_bias(wt, p_pad)

        u = _inproj(hp, norm_g[l], scale_p, shift_p, w_pad, seq)
        ut = _inproj_t(hp, norm_g[l], scale_p, shift_p, wt_sel, seq)
        nsa_rows_t = _inproj_t(hp, norm_g[l], scale_p, shift_p, wt_rows[:768], seq, tn=256, per_batch=True)
        fox_rows_t = _inproj_t(hp, norm_g[l], scale_p, shift_p, wt_rows[768:], seq, tn=256, per_batch=True)
        cmp_rows = u[:, C_CMP:C_CMP + 256]
        logf, c3 = _logf_prompt(u, bf_pad, batch, seq)
        o_fox = _fox_prompt(u, ut, c3, batch, seq)
        chunk = min(2 * RET_CHUNK, seq)
        o_ret, ret_state = _retention(u, zero_state, lg_lanes, batch, seq, chunk, chunk)
        pre_t = _cmp_pre_prompt(cmp_rows.reshape(batch, seq // CMP_STRIDE, CMP_FEAT), wt)
        kvt, kc = _cmp_post(pre_t, cmp_bias, w2t)
        o_nsa = _nsa_prompt(u, ut, kc, kvt, batch, seq)
        res = _outproj(u, o_nsa, o_fox, o_ret, hp, gate_p, wb_bf, wo_bf, final_g, seq, final)
        hp = res[0]
        y_prompt = res[-1]

        def cache_rows(x, r0, heads):
            x = x[:, r0:r0 + 2 * heads * HEAD_DIM].reshape(batch, 2, heads, HEAD_DIM, seq)
            return jnp.transpose(x, (0, 4, 1, 2, 3))

        win_rows = cache_rows(nsa_rows_t, 512, NSA_KV)
        win_state = jnp.concatenate(
            [jnp.zeros((batch, wb_len) + win_rows.shape[2:], F32), win_rows], axis=1)[:, -wb_len:]
        outs_p.append((cache_rows(nsa_rows_t, 0, NSA_KV), cache_rows(nsa_rows_t, 256, NSA_KV), win_state,
                       cache_rows(fox_rows_t, 0, FOX_HEADS), logf, ret_state))

        us = _inproj(hs, norm_g[l], scale_s, shift_s, w_pad, srows)
        kvt_s, _ = _cmp_post(_cmp_pre_paged(pool_cmp, l, page_table, wt), cmp_bias, w2t)
        o_cmp_s, selm = _nsa_cmp_sample(us, kvt_s, nseq, past, n_sel, blk_pad)
        o_nsa_s = _nsa_selwin_sample(us, pool_sel, l, page_table, selm, o_cmp_s, win_buf, past)
        c_past = _logf_past(pool_logf, l, page_table)
        o_fox_s, logf_s = _fox_sample(us, pool_fox, l, page_table, c_past, bf_pad)
        o_ret_s, ret_state_s = _retention(us, state_ret[l], lg_lanes, nseq, SAMPLE_ROWS, SAMPLE_ROWS, dec)
        res = _outproj(us, o_nsa_s, o_fox_s, o_ret_s, hs, gate_s, wb_bf, wo_bf, final_g, srows, final)
        hs = res[0]
        y_sample = res[-1]

        def new_rows(c0, width, shape):
            return us[:, c0:c0 + width].reshape(nseq, SAMPLE_ROWS, width)[:, :dec].reshape((nseq, dec) + shape)

        win_new = new_rows(C_WIN, 256, (2, NSA_KV, HEAD_DIM))
        win_all = jnp.concatenate([cache_nsa_win_kv[l], win_new], axis=1)
        outs_s.append((new_rows(C_CMP, 256, (2, NSA_KV, HEAD_DIM)),
                       new_rows(C_SEL, 256, (2, NSA_KV, HEAD_DIM)),
                       win_all[:, dec:],
                       new_rows(C_FKV, 1024, (2, FOX_HEADS, HEAD_DIM)),
                       logf_s.reshape(nseq, SAMPLE_ROWS, LANES)[:, :dec, :FOX_HEADS],
                       ret_state_s))

    def stacked(rows, i):
        return jnp.stack([r[i] for r in rows], axis=0)

    y_prompt = y_prompt.reshape(batch, seq, d)
    y_sample = y_sample.reshape(nseq, SAMPLE_ROWS, d)[:, :dec]
    return (y_prompt, y_sample,
            stacked(outs_p, 0), stacked(outs_s, 0), stacked(outs_p, 1), stacked(outs_s, 1),
            stacked(outs_p, 2), stacked(outs_s, 2), stacked(outs_p, 3), stacked(outs_s, 3),
            stacked(outs_p, 4), stacked(outs_s, 4), stacked(outs_p, 5), stacked(outs_s, 5))
```
